```python
import jax
import jax.numpy as jnp
from jax import lax
import numpy as np

D_MODEL = 1024
BATCH = 8
SEQ = 2048
DEPTH = 1

HEAD_DIM = 64
ROPE_THETA = 10000.0
NORM_EPS = 1e-6
NEG_INF = -1e30

NSA_HEADS = 8
NSA_KV_HEADS = 2
NSA_GROUP = NSA_HEADS // NSA_KV_HEADS
CMP_BLOCK = 32
CMP_STRIDE = 16
SLC_BLOCK = 64
SLC_TOPN = 16
WINDOW = 512
SLC_QBLOCK = 64
WIN_QBLOCK = 128

DSA_HEADS = 8
DSA_KV_RANK = 128
IDX_HEADS = 8
IDX_DIM = 64
DSA_TOPK = 256
DSA_QBLOCK = 128

PEER_HEADS = 8
PEER_KEYS = 128
PEER_EXPERTS = PEER_KEYS * PEER_KEYS
PEER_KEY_DIM = 128
PEER_HALF = PEER_KEY_DIM // 2
PEER_TOPK = 16
PEER_CHUNK = 128

NSA_Q_DIM = NSA_HEADS * HEAD_DIM
NSA_KV_DIM = NSA_KV_HEADS * HEAD_DIM
DSA_Q_DIM = DSA_HEADS * HEAD_DIM
IDX_Q_DIM = IDX_HEADS * IDX_DIM
IN_SPLITS = (NSA_Q_DIM, NSA_KV_DIM, NSA_KV_DIM, NSA_KV_DIM, NSA_KV_DIM, NSA_KV_DIM, NSA_KV_DIM,
             3 * NSA_HEADS, DSA_Q_DIM, DSA_KV_RANK, IDX_Q_DIM, IDX_DIM, IDX_HEADS, 2 * D_MODEL)
D_IN = sum(IN_SPLITS)

kernel_name = "hybrid_nsa_dsa_peer_adaln_block"


def rms_norm(x, g):
    xf = x.astype(jnp.float32)
    y = xf * lax.rsqrt(jnp.mean(xf * xf, axis=-1, keepdims=True) + NORM_EPS)
    return (y * g.astype(jnp.float32)).astype(x.dtype)


def rope(x, pos):
    half = x.shape[-1] // 2
    inv = ROPE_THETA ** (-jnp.arange(half, dtype=jnp.float32) / half)
    ang = pos[:, None] * inv[None, :]
    cos = jnp.cos(ang)[:, None, :]
    sin = jnp.sin(ang)[:, None, :]
    xf = x.astype(jnp.float32)
    x1, x2 = xf[..., :half], xf[..., half:]
    return jnp.concatenate([x1 * cos - x2 * sin, x2 * cos + x1 * sin], axis=-1).astype(x.dtype)


def masked_softmax(s, mask):
    s = jnp.where(mask, s.astype(jnp.float32), NEG_INF)
    m = jnp.max(s, axis=-1, keepdims=True)
    p = jnp.where(mask, jnp.exp(s - m), 0.0)
    return p / jnp.maximum(jnp.sum(p, axis=-1, keepdims=True), 1e-30)


def compress_blocks(x, pos_emb, w1, w2):
    B, S, G, dh = x.shape
    r = CMP_BLOCK // CMP_STRIDE
    n_chunk = S // CMP_STRIDE
    ch = x.reshape(B, n_chunk, CMP_STRIDE, G, dh)
    blk = jnp.concatenate([ch[:, i:n_chunk - r + 1 + i] for i in range(r)], axis=2)
    blk = blk + pos_emb[:, None, :]
    hid = jax.nn.gelu(jnp.einsum("bnlgd,lde->bnge", blk, w1))
    return jnp.einsum("bnge,ef->bngf", hid, w2)


def selected_attention(qg, ks, vs, sel, scale):
    B, S, G, hg, dh = qg.shape
    n_slc = S // SLC_BLOCK
    nqb = S // SLC_QBLOCK
    kb = ks.reshape(B, n_slc, SLC_BLOCK, G, dh).transpose(0, 3, 1, 2, 4)
    vb = vs.reshape(B, n_slc, SLC_BLOCK, G, dh).transpose(0, 3, 1, 2, 4)
    q_blk = jnp.moveaxis(qg.reshape(B, nqb, SLC_QBLOCK, G, hg, dh), 1, 0)
    sel_blk = jnp.moveaxis(sel.reshape(B, G, nqb, SLC_QBLOCK, -1), 2, 0)
    t_blk = jnp.arange(S).reshape(nqb, SLC_QBLOCK)
    bi = jnp.arange(B)[:, None, None, None]
    gi = jnp.arange(G)[None, :, None, None]

    def one_block(args):
        qb, sb, tb = args
        n = sb.shape[-1]
        kg = kb[bi, gi, sb].reshape(B, G, SLC_QBLOCK, n * SLC_BLOCK, dh)
        vg = vb[bi, gi, sb].reshape(B, G, SLC_QBLOCK, n * SLC_BLOCK, dh)
        kpos = (sb[..., None] * SLC_BLOCK + jnp.arange(SLC_BLOCK)).reshape(B, G, SLC_QBLOCK, n * SLC_BLOCK)
        mask = (kpos <= tb[None, None, :, None])[:, :, None]
        s = jnp.einsum("bqghd,bgqkd->bghqk", qb, kg) * scale
        p = masked_softmax(s, mask)
        return jnp.einsum("bghqk,bgqkd->bqghd", p.astype(vg.dtype), vg)

    o = lax.map(one_block, (q_blk, sel_blk, t_blk))
    return jnp.moveaxis(o, 0, 1).reshape(B, S, G, hg, dh)


def window_attention(qg, kw, vw, scale):
    B, S, G, hg, dh = qg.shape
    nwb = S // WIN_QBLOCK
    kspan = WINDOW + WIN_QBLOCK
    pad = ((0, 0), (WINDOW, 0), (0, 0), (0, 0))
    kpad = jnp.pad(kw, pad)
    vpad = jnp.pad(vw, pad)
    q_blk = jnp.moveaxis(qg.reshape(B, nwb, WIN_QBLOCK, G, hg, dh), 1, 0)

    def one_block(args):
        qb, i = args
        start = i * WIN_QBLOCK
        kb = lax.dynamic_slice_in_dim(kpad, start, kspan, axis=1)
        vb = lax.dynamic_slice_in_dim(vpad, start, kspan, axis=1)
        tq = start + jnp.arange(WIN_QBLOCK)
        tk = start - WINDOW + jnp.arange(kspan)
        diff = tq[:, None] - tk[None, :]
        mask = (tk[None, :] >= 0) & (diff >= 0) & (diff < WINDOW)
        s = jnp.einsum("bqghd,bkgd->bghqk", qb, kb) * scale
        p = masked_softmax(s, mask)
        return jnp.einsum("bghqk,bkgd->bqghd", p.astype(vb.dtype), vb)

    o = lax.map(one_block, (q_blk, jnp.arange(nwb)))
    return jnp.moveaxis(o, 0, 1).reshape(B, S, G, hg, dh)


def nsa_attention(q, kc, vc, ks, vs, kw, vw, gate_logits, cmp_pos, w1k, w2k, w1v, w2v):
    B, S, H, dh = q.shape
    G, hg = NSA_KV_HEADS, NSA_GROUP
    scale = dh ** -0.5
    t = jnp.arange(S)
    qg = q.reshape(B, S, G, hg, dh)

    k_cmp = compress_blocks(kc, cmp_pos, w1k, w2k)
    v_cmp = compress_blocks(vc, cmp_pos, w1v, w2v)
    nc = k_cmp.shape[1]
    blk_end = jnp.arange(nc) * CMP_STRIDE + CMP_BLOCK - 1
    m_cmp = blk_end[None, :] <= t[:, None]
    s_cmp = jnp.einsum("bsghd,bngd->bghsn", qg, k_cmp) * scale
    p_cmp = masked_softmax(s_cmp, m_cmp)
    o_cmp = jnp.einsum("bghsn,bngd->bsghd", p_cmp.astype(v_cmp.dtype), v_cmp)

    r = CMP_BLOCK // CMP_STRIDE
    imp = jnp.sum(p_cmp, axis=2)
    imp = sum(jnp.pad(imp, ((0, 0), (0, 0), (0, 0), (i, r - 1 - i))) for i in range(r))
    n_slc = S // SLC_BLOCK
    imp = imp.reshape(B, G, S, n_slc, SLC_BLOCK // CMP_STRIDE).sum(-1)
    j = jnp.arange(n_slc)[None, :]
    cur = (t // SLC_BLOCK)[:, None]
    forced = (j == 0) | (j == cur) | (j == cur - 1)
    admissible = j * SLC_BLOCK <= t[:, None]
    imp = jnp.where(forced, jnp.inf, jnp.where(admissible, imp, -jnp.inf))
    n_sel = min(SLC_TOPN, n_slc)
    _, sel = lax.top_k(imp, n_sel)
    o_slc = selected_attention(qg, ks, vs, sel, scale)

    o_win = window_attention(qg, kw, vw, scale)

    g = jax.nn.sigmoid(gate_logits)
    o = (g[..., 0:1] * o_cmp.reshape(B, S, H, dh)
         + g[..., 1:2] * o_slc.reshape(B, S, H, dh)
         + g[..., 2:3] * o_win.reshape(B, S, H, dh))
    return o.reshape(B, S, H * dh)


def dsa_attention(q, k, v, qi, ki, wi):
    B, S, H, dh = q.shape
    scale = dh ** -0.5
    topk = min(DSA_TOPK, S // 4)
    nqb = S // DSA_QBLOCK
    q_blk = jnp.moveaxis(q.reshape(B, nqb, DSA_QBLOCK, H, dh), 1, 0)
    qi_blk = jnp.moveaxis(qi.reshape(B, nqb, DSA_QBLOCK, IDX_HEADS, IDX_DIM), 1, 0)
    wi_blk = jnp.moveaxis(wi.reshape(B, nqb, DSA_QBLOCK, IDX_HEADS), 1, 0)
    t_blk = jnp.arange(S).reshape(nqb, DSA_QBLOCK)
    kpos = jnp.arange(S)
    ki32 = ki.astype(jnp.float32)

    def one_block(args):
        qb, qib, wib, tb = args
        logits = jnp.einsum("bqhd,bkd->bqhk", qib.astype(jnp.float32), ki32) * (IDX_DIM ** -0.5)
        score = jnp.einsum("bqh,bqhk->bqk", wib.astype(jnp.float32) * (IDX_HEADS ** -0.5), jax.nn.relu(logits))
        score = jnp.where(kpos[None, None, :] <= tb[None, :, None], score, -jnp.inf)
        _, sel = lax.top_k(score, topk)
        kg = jax.vmap(lambda kk, ss: kk[ss])(k, sel)
        vg = jax.vmap(lambda vv, ss: vv[ss])(v, sel)
        mask = (sel <= tb[None, :, None])[:, None]
        s = jnp.einsum("bqhd,bqkd->bhqk", qb, kg) * scale
        p = masked_softmax(s, mask)
        return jnp.einsum("bhqk,bqkd->bqhd", p.astype(vg.dtype), vg)

    o = lax.map(one_block, (q_blk, qi_blk, wi_blk, t_blk))
    return jnp.moveaxis(o, 0, 1).reshape(B, S, H * dh)


def token_mixer(h, w_in, cmp_pos, cmp_w1_k, cmp_w2_k, cmp_w1_v, cmp_w2_v,
                kv_norm_g, w_uk, w_uv, w_br_nsa, w_br_dsa, w_out):
    B, S, _ = h.shape
    pos = jnp.arange(S, dtype=jnp.float32)
    z = h @ w_in
    offs = [int(o) for o in np.cumsum(IN_SPLITS)[:-1]]
    (q_n, kc, vc, ks, vs, kw, vw, g_n, q_d, ckv, qi, ki, wi, g_m) = jnp.split(z, offs, axis=-1)
    G = NSA_KV_HEADS
    heads = lambda a, n: a.reshape(B, S, n, -1)

    o_nsa = nsa_attention(
        rope(heads(q_n, NSA_HEADS), pos),
        rope(heads(kc, G), pos), heads(vc, G),
        rope(heads(ks, G), pos), heads(vs, G),
        rope(heads(kw, G), pos), heads(vw, G),
        g_n.reshape(B, S, NSA_HEADS, 3), cmp_pos, cmp_w1_k, cmp_w2_k, cmp_w1_v, cmp_w2_v)

    c_kv = rms_norm(ckv, kv_norm_g)
    k_d = rope((c_kv @ w_uk)[:, :, None, :], pos)[:, :, 0]
    v_d = c_kv @ w_uv
    o_dsa = dsa_attention(
        rope(heads(q_d, DSA_HEADS), pos), k_d, v_d,
        rope(heads(qi, IDX_HEADS), pos), rope(ki[:, :, None, :], pos)[:, :, 0], wi)

    g_nsa, g_dsa = jnp.split(g_m, 2, axis=-1)
    merged = jax.nn.sigmoid(g_nsa) * (o_nsa @ w_br_nsa) + jax.nn.sigmoid(g_dsa) * (o_dsa @ w_br_dsa)
    return merged @ w_out


def peer_ffn(h, wq, subkeys, u, v):
    B, S, D = h.shape
    T = B * S
    xf = h.reshape(T, D)
    q = (xf @ wq).reshape(T, PEER_HEADS, 2, PEER_HALF)
    s_sub = jnp.einsum("thcd,hcnd->thcn", q, subkeys).astype(jnp.float32)
    s_half, i_half = lax.top_k(s_sub, PEER_TOPK)
    cand = s_half[:, :, 0, :, None] + s_half[:, :, 1, None, :]
    cidx = i_half[:, :, 0, :, None] * PEER_KEYS + i_half[:, :, 1, None, :]
    cand = cand.reshape(T, PEER_HEADS, PEER_TOPK * PEER_TOPK)
    cidx = cidx.reshape(T, PEER_HEADS, PEER_TOPK * PEER_TOPK)
    top_s, top_p = lax.top_k(cand, PEER_TOPK)
    eidx = jnp.take_along_axis(cidx, top_p, axis=-1)
    gate = jax.nn.softmax(top_s, axis=-1)
    nch = T // PEER_CHUNK

    def one_chunk(args):
        xc, ec, gc = args
        ue = u[ec]
        ve = v[ec]
        act = jax.nn.gelu(jnp.einsum("cd,chkd->chk", xc, ue), approximate=False)
        return jnp.einsum("chk,chkd->cd", (gc * act).astype(ve.dtype), ve)

    out = lax.map(one_chunk, (xf.reshape(nch, PEER_CHUNK, D),
                              eidx.reshape(nch, PEER_CHUNK, PEER_HEADS, PEER_TOPK),
                              gate.reshape(nch, PEER_CHUNK, PEER_HEADS, PEER_TOPK)))
    return out.reshape(B, S, D).astype(h.dtype)


def setup_inputs(seed: int = 0) -> dict:
    key = jax.random.key(seed)
    ks = jax.random.split(key, 24)
    L, D = DEPTH, D_MODEL
    nrm = lambda k, shape, std: jax.random.normal(k, shape, jnp.float32) * std
    gain = lambda k, shape: 1.0 + 0.05 * jax.random.normal(k, shape, jnp.float32)
    return {
        "x": nrm(ks[0], (BATCH, SEQ, D), 1.0),
        "c": nrm(ks[1], (BATCH, D), 1.0),
        "ada_w": nrm(ks[2], (L, D, 6 * D), 0.5 * D ** -0.5),
        "ada_b": nrm(ks[3], (L, 6 * D), 0.01),
        "norm1_g": gain(ks[4], (L, D)),
        "w_in": nrm(ks[5], (L, D, D_IN), D ** -0.5),
        "cmp_pos": nrm(ks[6], (L, CMP_BLOCK, HEAD_DIM), 0.1),
        "cmp_w1_k": nrm(ks[7], (L, CMP_BLOCK, HEAD_DIM, HEAD_DIM), (CMP_BLOCK * HEAD_DIM) ** -0.5),
        "cmp_w2_k": nrm(ks[8], (L, HEAD_DIM, HEAD_DIM), HEAD_DIM ** -0.5),
        "cmp_w1_v": nrm(ks[9], (L, CMP_BLOCK, HEAD_DIM, HEAD_DIM), (CMP_BLOCK * HEAD_DIM) ** -0.5),
        "cmp_w2_v": nrm(ks[10], (L, HEAD_DIM, HEAD_DIM), HEAD_DIM ** -0.5),
        "kv_norm_g": gain(ks[11], (L, DSA_KV_RANK)),
        "w_uk": nrm(ks[12], (L, DSA_KV_RANK, HEAD_DIM), DSA_KV_RANK ** -0.5),
        "w_uv": nrm(ks[13], (L, DSA_KV_RANK, HEAD_DIM), DSA_KV_RANK ** -0.5),
        "w_br_nsa": nrm(ks[14], (L, NSA_Q_DIM, D), NSA_Q_DIM ** -0.5),
        "w_br_dsa": nrm(ks[15], (L, DSA_Q_DIM, D), DSA_Q_DIM ** -0.5),
        "w_out": nrm(ks[16], (L, D, D), D ** -0.5),
        "norm2_g": gain(ks[17], (L, D)),
        "peer_wq": nrm(ks[18], (L, D, PEER_HEADS * PEER_KEY_DIM), D ** -0.5),
        "peer_subkeys": nrm(ks[19], (L, PEER_HEADS, 2, PEER_KEYS, PEER_HALF), PEER_HALF ** -0.5),
        "peer_u": nrm(ks[20], (L, PEER_EXPERTS, D), D ** -0.5),
        "peer_v": nrm(ks[21], (L, PEER_EXPERTS, D), 0.5),
        "final_g": gain(ks[22], (D,)),
    }


def reference(x, c, ada_w, ada_b, norm1_g, w_in, cmp_pos, cmp_w1_k, cmp_w2_k, cmp_w1_v, cmp_w2_v,
              kv_norm_g, w_uk, w_uv, w_br_nsa, w_br_dsa, w_out, norm2_g,
              peer_wq, peer_subkeys, peer_u, peer_v, final_g):
    for l in range(DEPTH):
        mod = jax.nn.silu(c) @ ada_w[l] + ada_b[l]
        sh1, sc1, g1, sh2, sc2, g2 = jnp.split(mod[:, None, :], 6, axis=-1)
        h = rms_norm(x, norm1_g[l]) * (1.0 + sc1) + sh1
        x = x + g1 * token_mixer(h, w_in[l], cmp_pos[l], cmp_w1_k[l], cmp_w2_k[l], cmp_w1_v[l], cmp_w2_v[l],
                                 kv_norm_g[l], w_uk[l], w_uv[l], w_br_nsa[l], w_br_dsa[l], w_out[l])
        h = rms_norm(x, norm2_g[l]) * (1.0 + sc2) + sh2
        x = x + g2 * peer_ffn(h, peer_wq[l], peer_subkeys[l], peer_u[l], peer_v[l])
    return rms_norm(x, final_g)
```

```python
import functools

import jax
import jax.numpy as jnp
from jax import lax
from jax.experimental import pallas as pl
from jax.experimental.pallas import tpu as pltpu

F32 = jnp.float32
BF16 = jnp.bfloat16
I32 = jnp.int32

D_MODEL = 1024
HEAD_DIM = 64
ROPE_THETA = 10000.0
NORM_EPS = 1e-6
NEG_INF = -1e30

NSA_HEADS = 8
NSA_KV_HEADS = 2
CMP_BLOCK = 32
CMP_STRIDE = 16
SLC_BLOCK = 64
SLC_TOPN = 16
WINDOW = 512

DSA_HEADS = 8
DSA_KV_RANK = 128
IDX_HEADS = 8
IDX_DIM = 64
DSA_TOPK = 256

PEER_HEADS = 8
PEER_KEYS = 128
PEER_HALF = 64
PEER_TOPK = 16

LANES = 128
W3_PITCH = PEER_KEYS + 8
VMEM_LIMIT = 56 * 1024 * 1024

NA = 2176
NV = 512
NM = 256
NG = 2 * D_MODEL
OFF_KC = NA
OFF_V = OFF_KC + 128
OFF_VC = OFF_V + NV
OFF_M = OFF_VC + 128
OFF_G = OFF_M + NM
NZ = OFF_G + NG
MISC_GATE = 0
MISC_WI = 24


def _dot(a, b):
    return jnp.dot(a, b, preferred_element_type=F32)


def _dot_nt(a, b):
    return lax.dot_general(a, b, (((1,), (1,)), ((), ())), preferred_element_type=F32)


def _cparams(*sem):
    return pltpu.CompilerParams(dimension_semantics=sem, vmem_limit_bytes=VMEM_LIMIT)


def _rope(z, c, s1, s2):
    return z * c + pltpu.roll(z, 96, 1) * s1 + pltpu.roll(z, 32, 1) * s2


def _half_mask(x, odd):
    lane = lax.broadcasted_iota(I32, x.shape, 1)
    keep = (lane >= HEAD_DIM) if odd else (lane < HEAD_DIM)
    return jnp.where(keep, x, jnp.zeros_like(x))


def _ada_kernel(c_ref, w_ref, b_ref, o_ref):
    c = c_ref[...]
    sc = c * jax.nn.sigmoid(c)
    o_ref[...] = _dot(sc.astype(BF16), w_ref[...].astype(BF16)) + b_ref[...]


def _ada(c, w, b):
    bsz, d = c.shape
    n = w.shape[1]
    tn = 1024
    return pl.pallas_call(
        _ada_kernel,
        grid=(n // tn,),
        in_specs=[pl.BlockSpec((bsz, d), lambda j: (0, 0)),
                  pl.BlockSpec((d, tn), lambda j: (0, j)),
                  pl.BlockSpec((1, tn), lambda j: (0, j))],
        out_specs=pl.BlockSpec((bsz, tn), lambda j: (0, j)),
        out_shape=jax.ShapeDtypeStruct((bsz, n), F32),
        compiler_params=_cparams("arbitrary"),
        name="ada",
    )(c, w, b.reshape(1, n))


def _proj_kernel(x_ref, sc_ref, sh_ref, g_ref, w_ref, cos_ref, s1_ref, s2_ref,
                 a_ref, kc_ref, v_ref, vc_ref, m_ref, gm_ref):
    x = x_ref[...]
    y = x * lax.rsqrt(jnp.mean(x * x, axis=-1, keepdims=True) + NORM_EPS) * g_ref[...]
    h = y * (1.0 + sc_ref[...]) + sh_ref[...]
    z = _dot(h.astype(BF16), w_ref[...])
    c, s1, s2 = cos_ref[...], s1_ref[...], s2_ref[...]
    for k in range(NA // LANES):
        sl = slice(k * LANES, (k + 1) * LANES)
        a_ref[:, sl] = _rope(z[:, sl], c, s1, s2).astype(BF16)
    kc_ref[...] = _rope(z[:, OFF_KC:OFF_KC + 128], c, s1, s2)
    v_ref[...] = z[:, OFF_V:OFF_V + NV].astype(BF16)
    vc_ref[...] = z[:, OFF_VC:OFF_VC + 128]
    m_ref[...] = z[:, OFF_M:OFF_M + NM]
    gm_ref[...] = z[:, OFF_G:OFF_G + NG]


def _proj(x2, sc, sh, g, w, cos, s1, s2, seq):
    t, d = x2.shape
    tm = 256
    nb = seq // tm
    row = lambda i: (i, 0)
    bat = lambda i: (i // nb, 0, 0)
    pos = lambda i: (i % nb, 0)
    full = lambda i: (0, 0)
    widths = (NA, 128, NV, 128, NM, NG)
    dtypes = (BF16, F32, BF16, F32, F32, F32)
    return pl.pallas_call(
        _proj_kernel,
        grid=(t // tm,),
        in_specs=[pl.BlockSpec((tm, d), row),
                  pl.BlockSpec((None, 1, d), bat),
                  pl.BlockSpec((None, 1, d), bat),
                  pl.BlockSpec((1, d), full),
                  pl.BlockSpec((d, NZ), full),
                  pl.BlockSpec((tm, LANES), pos),
                  pl.BlockSpec((tm, LANES), pos),
                  pl.BlockSpec((tm, LANES), pos)],
        out_specs=[pl.BlockSpec((tm, n), row) for n in widths],
        out_shape=[jax.ShapeDtypeStruct((t, n), dt) for n, dt in zip(widths, dtypes)],
        compiler_params=_cparams("arbitrary"),
        name="proj",
    )(x2, sc, sh, g, w, cos, s1, s2)


def _masked_attend(s, mask, v):
    s = jnp.where(mask, s, NEG_INF)
    m = jnp.max(s, axis=-1, keepdims=True)
    p = jnp.where(mask, jnp.exp(s - m), 0.0)
    den = jnp.maximum(jnp.sum(p, axis=-1, keepdims=True), 1e-30)
    return _dot(p.astype(BF16), v) / den, p, den


def _nsa_kernel(q_ref, ks_ref, kw_ref, vs_ref, vw_ref, kc_ref, vc_ref, misc_ref,
                wak_ref, wbk_ref, w2k_ref, wav_ref, wbv_ref, w2v_ref, posa_ref, posb_ref,
                mimp_ref, eexp_ref, o_ref, kcmp_ref, vcmp_ref, *, tq, seq):
    i = pl.program_id(1)
    ncmp = seq // CMP_STRIDE
    scale = HEAD_DIM ** -0.5

    @pl.when(i == 0)
    def _():
        def compress(c_ref, wa_ref, wb_ref, w2_ref):
            c = c_ref[...]
            ha = _dot((c + posa_ref[...]).astype(BF16), wa_ref[...])
            hb = _dot((c + posb_ref[...]).astype(BF16), wb_ref[...])
            hid = ha + pltpu.roll(hb, ncmp - 1, 0)
            return _dot(jax.nn.gelu(hid).astype(BF16), w2_ref[...])
        kcmp_ref[...] = compress(kc_ref, wak_ref, wbk_ref, w2k_ref).astype(BF16)
        vcmp_ref[...] = compress(vc_ref, wav_ref, wbv_ref, w2v_ref).astype(BF16)

    t0 = i * tq
    trow = t0 + lax.broadcasted_iota(I32, (tq, LANES), 0)
    lane = lax.broadcasted_iota(I32, (tq, LANES), 1)
    sig = jax.nn.sigmoid(misc_ref[...])

    cmp_mask = (lane * CMP_STRIDE + (CMP_BLOCK - 1)) <= trow
    o_cmp, imp = [], []
    for h in range(NSA_HEADS):
        g = h // (NSA_HEADS // NSA_KV_HEADS)
        qm = _half_mask(q_ref[:, (h // 2) * LANES:(h // 2 + 1) * LANES], h % 2)
        s = _dot_nt(qm, kcmp_ref[:, g * LANES:(g + 1) * LANES]) * scale
        o, p, den = _masked_attend(s, cmp_mask, vcmp_ref[:, g * LANES:(g + 1) * LANES])
        o_cmp.append(o)
        pn = p / den
        if h % (NSA_HEADS // NSA_KV_HEADS) == 0:
            imp.append(pn)
        else:
            imp[g] = imp[g] + pn

    kspan = WINDOW + tq
    kstart = pl.multiple_of(jnp.maximum(t0 - WINDOW, 0), tq)
    tq_w = t0 + lax.broadcasted_iota(I32, (tq, kspan), 0)
    tk_w = kstart + lax.broadcasted_iota(I32, (tq, kspan), 1)
    diff_w = tq_w - tk_w
    win_mask = (diff_w >= 0) & (diff_w < WINDOW)
    tq_s = t0 + lax.broadcasted_iota(I32, (tq, seq), 0)
    tk_s = lax.broadcasted_iota(I32, (tq, seq), 1)
    causal = tk_s <= tq_s

    out_pairs = [None] * (NSA_HEADS // 2)
    for g in range(NSA_KV_HEADS):
        pg = imp[g]
        p_hi = pg.astype(BF16)
        r1 = pg - p_hi.astype(F32)
        p_mid = r1.astype(BF16)
        p_lo = (r1 - p_mid.astype(F32)).astype(BF16)
        blk = _dot(p_hi, mimp_ref[...]) + _dot(p_mid, mimp_ref[...]) + _dot(p_lo, mimp_ref[...])
        cur = trow // SLC_BLOCK
        forced = (lane == 0) | (lane == cur) | (lane == cur - 1)
        admissible = lane * SLC_BLOCK <= trow
        val = jnp.where(forced, jnp.inf, jnp.where(admissible, blk, -jnp.inf))
        rank = jnp.zeros((tq, LANES), I32)
        for j in range(seq // SLC_BLOCK):
            vj = jnp.broadcast_to(val[:, j:j + 1], (tq, LANES))
            ahead = (vj > val) | ((vj == val) & (lane > j))
            rank = rank + ahead.astype(I32)
        chosen = (rank < SLC_TOPN).astype(BF16)
        slc_mask = (_dot(chosen, eexp_ref[...]) > 0.5) & causal

        ksd = ks_ref[:, g * LANES:(g + 1) * LANES]
        vsd = vs_ref[:, g * LANES:(g + 1) * LANES]
        kwd = kw_ref[pl.ds(kstart, kspan), g * LANES:(g + 1) * LANES]
        vwd = vw_ref[pl.ds(kstart, kspan), g * LANES:(g + 1) * LANES]
        for hh in range(NSA_HEADS // NSA_KV_HEADS):
            h = g * (NSA_HEADS // NSA_KV_HEADS) + hh
            qm = _half_mask(q_ref[:, (h // 2) * LANES:(h // 2 + 1) * LANES], h % 2)
            o_s, _, _ = _masked_attend(_dot_nt(qm, ksd) * scale, slc_mask, vsd)
            o_w, _, _ = _masked_attend(_dot_nt(qm, kwd) * scale, win_mask, vwd)
            c0 = MISC_GATE + 3 * h
            o_h = (sig[:, c0:c0 + 1] * o_cmp[h] + sig[:, c0 + 1:c0 + 2] * o_s + sig[:, c0 + 2:c0 + 3] * o_w)
            if h % 2 == 0:
                out_pairs[h // 2] = o_h
            else:
                out_pairs[h // 2] = jnp.where(lane < HEAD_DIM, out_pairs[h // 2], o_h)
    for m in range(NSA_HEADS // 2):
        o_ref[:, m * LANES:(m + 1) * LANES] = out_pairs[m].astype(BF16)


def _nsa(a3, v3, kcr, vcr, m3, wts, seq):
    bsz = a3.shape[0]
    tq = 128
    ncmp = seq // CMP_STRIDE
    per_b = lambda blk: (lambda b, i: (b, 0, blk))
    full2 = lambda b, i: (0, 0)
    w_specs = [pl.BlockSpec(w.shape, full2) for w in wts]
    return pl.pallas_call(
        functools.partial(_nsa_kernel, tq=tq, seq=seq),
        grid=(bsz, seq // tq),
        in_specs=[pl.BlockSpec((None, tq, 512), lambda b, i: (b, i, 0)),
                  pl.BlockSpec((None, seq, 256), per_b(2)),
                  pl.BlockSpec((None, seq, 256), per_b(3)),
                  pl.BlockSpec((None, seq, 256), per_b(0)),
                  pl.BlockSpec((None, seq, 256), per_b(1)),
                  pl.BlockSpec((None, ncmp, CMP_STRIDE * 128), per_b(0)),
                  pl.BlockSpec((None, ncmp, CMP_STRIDE * 128), per_b(0)),
                  pl.BlockSpec((None, tq, LANES), lambda b, i: (b, i, 1)),
                  ] + w_specs,
        out_specs=pl.BlockSpec((None, tq, 512), lambda b, i: (b, i, 0)),
        out_shape=jax.ShapeDtypeStruct((bsz, seq, 512), BF16),
        scratch_shapes=[pltpu.VMEM((ncmp, 256), BF16), pltpu.VMEM((ncmp, 256), BF16)],
        compiler_params=_cparams("arbitrary", "arbitrary"),
        name="nsa",
    )(a3, a3, a3, v3, v3, kcr, vcr, m3, *wts)


def _dsa_kernel(qd_ref, qi_ref, ki_ref, ckv_ref, misc_ref, gkv_ref, wuk_ref, wuv_ref,
                cos_ref, s1_ref, s2_ref, o_ref, kd_ref, vt_ref, *, tq, seq, topk):
    i = pl.program_id(1)

    @pl.when(i == 0)
    def _():
        ck = ckv_ref[...]
        cn = ck * lax.rsqrt(jnp.mean(ck * ck, axis=-1, keepdims=True) + NORM_EPS) * gkv_ref[...]
        cb = cn.astype(BF16)
        kd_ref[...] = _rope(_dot(cb, wuk_ref[...]), cos_ref[...], s1_ref[...], s2_ref[...]).astype(BF16)
        vt_ref[...] = _dot(cb, wuv_ref[...]).T.astype(BF16)

    t0 = i * tq
    kpos = lax.broadcasted_iota(I32, (seq, tq), 0)
    tpos = t0 + lax.broadcasted_iota(I32, (seq, tq), 1)
    causal = kpos <= tpos
    misc_t = misc_ref[...].T

    score = jnp.zeros((seq, tq), F32)
    for h in range(IDX_HEADS):
        qm = _half_mask(qi_ref[:, (h // 2) * LANES:(h // 2 + 1) * LANES], h % 2)
        logit = _dot_nt(ki_ref[...], qm) * (IDX_DIM ** -0.5)
        w_h = misc_t[MISC_WI + h:MISC_WI + h + 1, :] * (IDX_HEADS ** -0.5)
        score = score + w_h * jnp.maximum(logit, 0.0)
    score = jnp.where(causal, score, -jnp.inf)

    bits = lax.bitcast_convert_type(score, I32)
    key = bits ^ ((bits >> 31) & jnp.int32(0x7FFFFFFF))
    int_min = jnp.int32(-2 ** 31)

    def bit_step(it, tu):
        cu = tu | lax.shift_left(jnp.int32(1), 31 - it)
        cnt = jnp.sum((key >= (cu ^ int_min)).astype(F32), axis=0, keepdims=True)
        return jnp.where(cnt >= topk, cu, tu)
    thr = lax.fori_loop(0, 32, bit_step, jnp.zeros((1, tq), I32)) ^ int_min
    above = key > thr
    tie = key == thr
    need = topk - jnp.sum(above.astype(F32), axis=0, keepdims=True)

    def tie_step(it, lim):
        cl = lim | lax.shift_left(jnp.int32(1), 11 - it)
        cnt = jnp.sum((tie & (kpos < cl)).astype(F32), axis=0, keepdims=True)
        return jnp.where(cnt <= need, cl, lim)
    lim = lax.fori_loop(0, 12, tie_step, jnp.zeros((1, tq), I32))
    sel = causal & (above | (tie & (kpos < lim)))

    scale = HEAD_DIM ** -0.5
    row = lax.broadcasted_iota(I32, (LANES, tq), 0)
    pair = None
    for h in range(DSA_HEADS):
        qm = _half_mask(qd_ref[:, (h // 2) * LANES:(h // 2 + 1) * LANES], h % 2)
        s = jnp.where(sel, _dot_nt(kd_ref[...], qm) * scale, NEG_INF)
        m = jnp.max(s, axis=0, keepdims=True)
        p = jnp.where(sel, jnp.exp(s - m), 0.0)
        den = jnp.maximum(jnp.sum(p, axis=0, keepdims=True), 1e-30)
        o_t = _dot(vt_ref[...], p.astype(BF16)) / den
        if h % 2 == 0:
            pair = o_t
        else:
            o_ref[:, (h // 2) * LANES:(h // 2 + 1) * LANES] = jnp.where(row < HEAD_DIM, pair, o_t).T.astype(BF16)


def _dsa(a3, m3, gkv, wuk, wuv, cos, s1, s2, seq):
    bsz = a3.shape[0]
    tq = 128
    topk = min(DSA_TOPK, seq // 4)
    full2 = lambda b, i: (0, 0)
    return pl.pallas_call(
        functools.partial(_dsa_kernel, tq=tq, seq=seq, topk=topk),
        grid=(bsz, seq // tq),
        in_specs=[pl.BlockSpec((None, tq, 512), lambda b, i: (b, i, 2)),
                  pl.BlockSpec((None, tq, 512), lambda b, i: (b, i, 3)),
                  pl.BlockSpec((None, seq, LANES), lambda b, i: (b, 0, 16)),
                  pl.BlockSpec((None, seq, LANES), lambda b, i: (b, 0, 0)),
                  pl.BlockSpec((None, tq, LANES), lambda b, i: (b, i, 1)),
                  pl.BlockSpec((1, LANES), full2),
                  pl.BlockSpec((LANES, LANES), full2),
                  pl.BlockSpec((LANES, LANES), full2),
                  pl.BlockSpec((seq, LANES), full2),
                  pl.BlockSpec((seq, LANES), full2),
                  pl.BlockSpec((seq, LANES), full2)],
        out_specs=pl.BlockSpec((None, tq, 512), lambda b, i: (b, i, 0)),
        out_shape=jax.ShapeDtypeStruct((bsz, seq, 512), BF16),
        scratch_shapes=[pltpu.VMEM((seq, LANES), BF16), pltpu.VMEM((LANES, seq), BF16)],
        compiler_params=_cparams("arbitrary", "arbitrary"),
        name="dsa",
    )(a3, a3, a3, m3, m3, gkv, wuk, wuv, cos, s1, s2)


def _merge_kernel(on_ref, od_ref, gm_ref, x_ref, g1_ref, sc_ref, sh_ref, n2_ref,
                  wn_ref, wd_ref, wo_ref, x1_ref, h2_ref):
    gm = gm_ref[...]
    merged = (jax.nn.sigmoid(gm[:, :D_MODEL]) * _dot(on_ref[...], wn_ref[...])
              + jax.nn.sigmoid(gm[:, D_MODEL:]) * _dot(od_ref[...], wd_ref[...]))
    x1 = x_ref[...] + g1_ref[...] * _dot(merged.astype(BF16), wo_ref[...])
    x1_ref[...] = x1
    y = x1 * lax.rsqrt(jnp.mean(x1 * x1, axis=-1, keepdims=True) + NORM_EPS) * n2_ref[...]
    h2_ref[...] = (y * (1.0 + sc_ref[...]) + sh_ref[...]).astype(BF16)


def _merge(o_nsa, o_dsa, gm, x2, g1, sc2, sh2, n2, wn, wd, wo, seq):
    t, d = x2.shape
    tm = 256
    nb = seq // tm
    row = lambda i: (i, 0)
    bat = lambda i: (i // nb, 0, 0)
    full = lambda i: (0, 0)
    return pl.pallas_call(
        _merge_kernel,
        grid=(t // tm,),
        in_specs=[pl.BlockSpec((tm, 512), row), pl.BlockSpec((tm, 512), row),
                  pl.BlockSpec((tm, NG), row), pl.BlockSpec((tm, d), row),
                  pl.BlockSpec((None, 1, d), bat), pl.BlockSpec((None, 1, d), bat),
                  pl.BlockSpec((None, 1, d), bat), pl.BlockSpec((1, d), full),
                  pl.BlockSpec(wn.shape, full), pl.BlockSpec(wd.shape, full), pl.BlockSpec(wo.shape, full)],
        out_specs=[pl.BlockSpec((tm, d), row), pl.BlockSpec((tm, d), row)],
        out_shape=[jax.ShapeDtypeStruct((t, d), F32), jax.ShapeDtypeStruct((t, d), BF16)],
        compiler_params=_cparams("arbitrary"),
        name="merge",
    )(o_nsa, o_dsa, gm, x2, g1, sc2, sh2, n2, wn, wd, wo)


def _topk_rows(vals, k):
    n, t = vals.shape
    ridx = lax.broadcasted_iota(I32, (n, t), 0).astype(F32)
    slot = lax.broadcasted_iota(I32, (k, t), 0)
    top_v = jnp.zeros((k, t), F32)
    top_i = jnp.zeros((k, t), F32)
    for j in range(k):
        m = jnp.max(vals, axis=0, keepdims=True)
        idx = jnp.min(jnp.where(vals == m, ridx, float(n)), axis=0, keepdims=True)
        top_v = jnp.where(slot == j, m, top_v)
        top_i = jnp.where(slot == j, idx, top_i)
        vals = jnp.where(ridx == idx, -jnp.inf, vals)
    return top_v, top_i


def _route_kernel(h_ref, wq_ref, k2_ref, i_ref, j_ref, g_ref, q3_ref, is_ref, js_ref, gs_ref, *, tm):
    q = _dot(h_ref[...], wq_ref[...]).astype(BF16)
    for h in range(PEER_HEADS):
        q3_ref[h] = q[:, h * LANES:(h + 1) * LANES]

    def head(h, carry):
        s = _dot_nt(k2_ref[h], q3_ref[h])
        s0, i0 = _topk_rows(s[:PEER_KEYS], PEER_TOPK)
        s1, i1 = _topk_rows(s[PEER_KEYS:], PEER_TOPK)
        cand = jnp.concatenate([s0[a:a + 1, :] + s1 for a in range(PEER_TOPK)], axis=0)
        top_s, top_p = _topk_rows(cand, PEER_TOPK)
        pa = jnp.floor(top_p * (1.0 / PEER_TOPK))
        pb = top_p - pa * PEER_TOPK
        ii = jnp.zeros((PEER_TOPK, tm), F32)
        jj = jnp.zeros((PEER_TOPK, tm), F32)
        for a in range(PEER_TOPK):
            ii = jnp.where(pa == a, i0[a:a + 1, :], ii)
            jj = jnp.where(pb == a, i1[a:a + 1, :], jj)
        e = jnp.exp(top_s - jnp.max(top_s, axis=0, keepdims=True))
        gate = e / jnp.sum(e, axis=0, keepdims=True)
        rows = pl.ds(pl.multiple_of(h * PEER_TOPK, PEER_TOPK), PEER_TOPK)
        is_ref[rows, :] = ii
        js_ref[rows, :] = jj
        gs_ref[rows, :] = gate
        return carry
    lax.fori_loop(0, PEER_HEADS, head, 0)
    i_ref[...] = is_ref[...].T
    j_ref[...] = js_ref[...].T
    g_ref[...] = gs_ref[...].T


def _route(h2, wq, k2):
    t, d = h2.shape
    tm = 128
    row = lambda i: (i, 0)
    nk = PEER_HEADS * PEER_TOPK
    return pl.pallas_call(
        functools.partial(_route_kernel, tm=tm),
        grid=(t // tm,),
        in_specs=[pl.BlockSpec((tm, d), row),
                  pl.BlockSpec(wq.shape, lambda i: (0, 0)),
                  pl.BlockSpec(k2.shape, lambda i: (0, 0, 0))],
        out_specs=[pl.BlockSpec((tm, nk), row)] * 3,
        out_shape=[jax.ShapeDtypeStruct((t, nk), F32)] * 3,
        scratch_shapes=[pltpu.VMEM((PEER_HEADS, tm, LANES), BF16),
                        pltpu.VMEM((nk, tm), F32), pltpu.VMEM((nk, tm), F32), pltpu.VMEM((nk, tm), F32)],
        compiler_params=_cparams("arbitrary"),
        name="route",
    )(h2, wq, k2)


def _expert_kernel(h_ref, i_ref, j_ref, g_ref, u_ref, v_ref, x1_ref, g2_ref, fg_ref, o_ref,
                   w3_ref, y_ref, acc_ref, *, tm, ne, final_norm):
    c = pl.program_id(1)
    nsub = ne // PEER_KEYS

    @pl.when(c == 0)
    def _():
        acc_ref[...] = jnp.zeros_like(acc_ref)
        sub = lax.broadcasted_iota(I32, (PEER_KEYS, LANES), 0).astype(F32)

        def group(t8, carry):
            r0 = pl.multiple_of(t8 * 8, 8)
            ii = i_ref[pl.ds(r0, 8), :]
            jj = j_ref[pl.ds(r0, 8), :]
            gg = g_ref[pl.ds(r0, 8), :]
            for r in range(8):
                a = (sub == ii[r:r + 1, :]).astype(BF16)
                rg = jnp.where(sub == jj[r:r + 1, :], gg[r:r + 1, :], 0.0)
                r_hi = rg.astype(BF16)
                r_lo = (rg - r_hi.astype(F32)).astype(BF16)
                w = _dot_nt(jnp.concatenate([a, a], axis=1), jnp.concatenate([r_hi, r_lo], axis=1))
                w3_ref[pl.ds(pl.multiple_of((r0 + r) * W3_PITCH, 8), PEER_KEYS), :] = w
            return carry
        lax.fori_loop(0, tm // 8, group, 0)

    act = _dot_nt(h_ref[...], u_ref[...])
    for s in range(nsub):
        i_idx = c * nsub + s
        w = w3_ref[pl.ds(i_idx, tm, stride=W3_PITCH), :]
        a = act[:, s * LANES:(s + 1) * LANES]
        gelu = 0.5 * a * (1.0 + lax.erf(a * (2.0 ** -0.5)))
        y_ref[:, s * LANES:(s + 1) * LANES] = (w * gelu).astype(BF16)
    acc_ref[...] += _dot(y_ref[...], v_ref[...])

    @pl.when(c == pl.num_programs(1) - 1)
    def _():
        x2 = x1_ref[...] + g2_ref[...] * acc_ref[...]
        if final_norm:
            x2 = x2 * lax.rsqrt(jnp.mean(x2 * x2, axis=-1, keepdims=True) + NORM_EPS) * fg_ref[...]
        o_ref[...] = x2


def _experts(h2, ii, jj, gg, u, v, x1, g2, fg, seq, final_norm):
    t, d = h2.shape
    tm = 256
    ne = 2048
    nb = seq // tm
    nexp = u.shape[0]
    row = lambda i, c: (i, 0)
    return pl.pallas_call(
        functools.partial(_expert_kernel, tm=tm, ne=ne, final_norm=final_norm),
        grid=(t // tm, nexp // ne),
        in_specs=[pl.BlockSpec((tm, d), row),
                  pl.BlockSpec((tm, LANES), row), pl.BlockSpec((tm, LANES), row), pl.BlockSpec((tm, LANES), row),
                  pl.BlockSpec((ne, d), lambda i, c: (c, 0)),
                  pl.BlockSpec((ne, d), lambda i, c: (c, 0)),
                  pl.BlockSpec((tm, d), row),
                  pl.BlockSpec((None, 1, d), lambda i, c: (i // nb, 0, 0)),
                  pl.BlockSpec((1, d), lambda i, c: (0, 0))],
        out_specs=pl.BlockSpec((tm, d), row),
        out_shape=jax.ShapeDtypeStruct((t, d), F32),
        scratch_shapes=[pltpu.VMEM((tm * W3_PITCH, LANES), F32),
                        pltpu.VMEM((tm, ne), BF16),
                        pltpu.VMEM((tm, d), F32)],
        compiler_params=_cparams("arbitrary", "arbitrary"),
        name="experts",
    )(h2, ii, jj, gg, u, v, x1, g2, fg)


def _dup(w):
    return jnp.concatenate([w, w], axis=1)


def _win_layout(w_in):
    nq, nkv = NSA_HEADS * HEAD_DIM, NSA_KV_HEADS * HEAD_DIM
    sizes = (nq, nkv, nkv, nkv, nkv, nkv, nkv, 3 * NSA_HEADS, DSA_HEADS * HEAD_DIM, DSA_KV_RANK,
             IDX_HEADS * IDX_DIM, IDX_DIM, IDX_HEADS, 2 * D_MODEL)
    offs = [0]
    for s in sizes:
        offs.append(offs[-1] + s)
    q_n, kc, vc, ks, vs, kw, vw, g_n, q_d, ckv, qi, ki, wi, g_m = [w_in[:, offs[k]:offs[k + 1]] for k in range(14)]
    dup2 = lambda w: jnp.concatenate([_dup(w[:, :HEAD_DIM]), _dup(w[:, HEAD_DIM:])], axis=1)
    misc = jnp.concatenate([g_n, wi, jnp.zeros((w_in.shape[0], LANES - 32), w_in.dtype)], axis=1)
    cols = [q_n, dup2(ks), dup2(kw), q_d, qi, _dup(ki), kc, dup2(vs), dup2(vw), vc, ckv, misc, g_m]
    return jnp.concatenate(cols, axis=1).astype(BF16)


def _rope_tables(seq):
    half = HEAD_DIM // 2
    pos = jnp.arange(seq, dtype=F32)
    inv = ROPE_THETA ** (-jnp.arange(half, dtype=F32) / half)
    ang = pos[:, None] * inv[None, :]
    cos, sin, zero = jnp.cos(ang), jnp.sin(ang), jnp.zeros((seq, half), F32)
    c = jnp.concatenate([cos, cos, cos, cos], axis=1)
    s1 = jnp.concatenate([-sin, zero, -sin, zero], axis=1)
    s2 = jnp.concatenate([zero, sin, zero, sin], axis=1)
    return c, s1, s2


def _cmp_weights(pos, w1, w2):
    g = NSA_KV_HEADS
    eye = jnp.eye(g, dtype=F32)
    big = jnp.einsum("lde,gh->lgdhe", w1, eye).reshape(CMP_BLOCK, g * HEAD_DIM, g * HEAD_DIM)
    wa = big[:CMP_STRIDE].reshape(CMP_STRIDE * g * HEAD_DIM, g * HEAD_DIM)
    wb = big[CMP_STRIDE:].reshape(CMP_STRIDE * g * HEAD_DIM, g * HEAD_DIM)
    z = jnp.zeros_like(w2)
    w2d = jnp.concatenate([jnp.concatenate([w2, w2, z, z], axis=1),
                           jnp.concatenate([z, z, w2, w2], axis=1)], axis=0)
    pos_t = jnp.broadcast_to(pos[:, None, :], (CMP_BLOCK, g, HEAD_DIM))
    pa = pos_t[:CMP_STRIDE].reshape(1, -1)
    pb = pos_t[CMP_STRIDE:].reshape(1, -1)
    return wa.astype(BF16), wb.astype(BF16), w2d.astype(BF16), pa, pb


def _selection_constants(seq):
    ncmp = seq // CMP_STRIDE
    nslc = seq // SLC_BLOCK
    per = SLC_BLOCK // CMP_STRIDE
    n = jnp.arange(ncmp)[:, None]
    j = jnp.arange(LANES)[None, :]
    mimp = ((n // per == j).astype(F32) + ((n + 1) // per == j).astype(F32)) * (j < nslc)
    mimp = mimp * (n < ncmp - 1)
    key = jnp.arange(seq)[None, :]
    eexp = (key // SLC_BLOCK == jnp.arange(LANES)[:, None]).astype(F32)
    return mimp.astype(BF16), eexp.astype(BF16)


def _subkey_blocks(subkeys):
    z = jnp.zeros_like(subkeys[:, 0])
    top = jnp.concatenate([subkeys[:, 0], z], axis=2)
    bot = jnp.concatenate([z, subkeys[:, 1]], axis=2)
    return jnp.concatenate([top, bot], axis=1).astype(BF16)


def kernel(x, c, ada_w, ada_b, norm1_g, w_in, cmp_pos, cmp_w1_k, cmp_w2_k, cmp_w1_v, cmp_w2_v,
           kv_norm_g, w_uk, w_uv, w_br_nsa, w_br_dsa, w_out, norm2_g,
           peer_wq, peer_subkeys, peer_u, peer_v, final_g):
    bsz, seq, d = x.shape
    depth = ada_w.shape[0]
    t = bsz * seq
    cos, s1, s2 = _rope_tables(seq)
    mimp, eexp = _selection_constants(seq)
    xt = x.reshape(t, d)
    out = xt
    for l in range(depth):
        mod = _ada(c, ada_w[l], ada_b[l]).reshape(bsz, 1, 6 * d)
        sh1, sc1, g1, sh2, sc2, g2 = [mod[:, :, k * d:(k + 1) * d] for k in range(6)]
        a, kc, vv, vc, mm, gm = _proj(xt, sc1, sh1, norm1_g[l].reshape(1, d), _win_layout(w_in[l]),
                                      cos, s1, s2, seq)
        a3 = a.reshape(bsz, seq, NA)
        m3 = mm.reshape(bsz, seq, NM)
        ncmp = seq // CMP_STRIDE
        wak, wbk, w2k, pa, pb = _cmp_weights(cmp_pos[l], cmp_w1_k[l], cmp_w2_k[l])
        wav, wbv, w2v, _, _ = _cmp_weights(cmp_pos[l], cmp_w1_v[l], cmp_w2_v[l])
        o_nsa = _nsa(a3, vv.reshape(bsz, seq, NV), kc.reshape(bsz, ncmp, CMP_STRIDE * 128),
                     vc.reshape(bsz, ncmp, CMP_STRIDE * 128), m3,
                     (wak, wbk, w2k, wav, wbv, w2v, pa, pb, mimp, eexp), seq)
        o_dsa = _dsa(a3, m3, kv_norm_g[l].reshape(1, DSA_KV_RANK), _dup(w_uk[l]).astype(BF16),
                     _dup(w_uv[l]).astype(BF16), cos, s1, s2, seq)
        x1, h2 = _merge(o_nsa.reshape(t, 512), o_dsa.reshape(t, 512), gm, xt, g1, sc2, sh2,
                        norm2_g[l].reshape(1, d), w_br_nsa[l].astype(BF16), w_br_dsa[l].astype(BF16),
                        w_out[l].astype(BF16), seq)
        ii, jj, gg = _route(h2, peer_wq[l].astype(BF16), _subkey_blocks(peer_subkeys[l]))
        out = _experts(h2, ii, jj, gg, peer_u[l].astype(BF16), peer_v[l].astype(BF16), x1, g2,
                       final_g.reshape(1, d), seq, final_norm=(l == depth - 1))
        xt = out
    return out.reshape(bsz, seq, d)
```

```python
import functools

import jax
import jax.numpy as jnp
from jax import lax
from jax.experimental import pallas as pl
from jax.experimental.pallas import tpu as pltpu

F32 = jnp.float32
BF16 = jnp.bfloat16
I32 = jnp.int32

D_MODEL = 1024
HEAD_DIM = 64
ROPE_THETA = 10000.0
NORM_EPS = 1e-6
NEG_INF = -1e30

NSA_HEADS = 8
NSA_KV_HEADS = 2
CMP_BLOCK = 32
CMP_STRIDE = 16
SLC_BLOCK = 64
SLC_TOPN = 16
WINDOW = 512
NSA_KEY_VARIANTS = 8

DSA_HEADS = 8
DSA_KV_RANK = 128
IDX_HEADS = 8
IDX_DIM = 64
DSA_TOPK = 256
DSA_KEY_VARIANTS = 8

PEER_HEADS = 8
PEER_KEYS = 128
PEER_HALF = 64
PEER_TOPK = 16

LANES = 128
W3_PITCH = PEER_KEYS + 8
VMEM_LIMIT = 56 * 1024 * 1024

NA = 2176
NV = 512
NM = 256
NG = 2 * D_MODEL
OFF_KC = NA
OFF_V = OFF_KC + 128
OFF_VC = OFF_V + NV
OFF_M = OFF_VC + 128
OFF_G = OFF_M + NM
NZ = OFF_G + NG
MISC_GATE = 0
MISC_WI = 24


def _dot(a, b):
    return jnp.dot(a, b, preferred_element_type=F32)


def _dot_nt(a, b):
    return lax.dot_general(a, b, (((1,), (1,)), ((), ())), preferred_element_type=F32)


def _cparams(*sem):
    return pltpu.CompilerParams(dimension_semantics=sem, vmem_limit_bytes=VMEM_LIMIT)


def _rope(z, c, s1, s2):
    return z * c + pltpu.roll(z, 96, 1) * s1 + pltpu.roll(z, 32, 1) * s2


def _col_reduce(x, op, reduce):
    n = x.shape[0]
    while n > 8 and (n // 8) % 2 == 0:
        n //= 2
        x = op(x[:n], x[n:])
    return reduce(x, axis=0, keepdims=True)


def _colsum(x):
    return _col_reduce(x, jnp.add, jnp.sum)


def _colmax(x):
    return _col_reduce(x, jnp.maximum, jnp.max)


def _half_mask(x, odd):
    lane = lax.broadcasted_iota(I32, x.shape, 1)
    keep = (lane >= HEAD_DIM) if odd else (lane < HEAD_DIM)
    return jnp.where(keep, x, jnp.zeros_like(x))


def _ada_kernel(c_ref, w_ref, b_ref, o_ref):
    c = c_ref[...]
    sc = c * jax.nn.sigmoid(c)
    o_ref[...] = _dot(sc.astype(BF16), w_ref[...].astype(BF16)) + b_ref[...]


def _ada(c, w, b):
    bsz, d = c.shape
    n = w.shape[1]
    tn = 1024
    return pl.pallas_call(
        _ada_kernel,
        grid=(n // tn,),
        in_specs=[pl.BlockSpec((bsz, d), lambda j: (0, 0)),
                  pl.BlockSpec((d, tn), lambda j: (0, j)),
                  pl.BlockSpec((1, tn), lambda j: (0, j))],
        out_specs=pl.BlockSpec((bsz, tn), lambda j: (0, j)),
        out_shape=jax.ShapeDtypeStruct((bsz, n), F32),
        compiler_params=_cparams("arbitrary"),
        name="ada",
    )(c, w, b.reshape(1, n))


def _proj_kernel(x_ref, sc_ref, sh_ref, g_ref, w_ref, cos_ref, s1_ref, s2_ref,
                 a_ref, kc_ref, v_ref, vc_ref, m_ref, gm_ref):
    x = x_ref[...]
    y = x * lax.rsqrt(jnp.mean(x * x, axis=-1, keepdims=True) + NORM_EPS) * g_ref[...]
    h = y * (1.0 + sc_ref[...]) + sh_ref[...]
    z = _dot(h.astype(BF16), w_ref[...])
    c, s1, s2 = cos_ref[...], s1_ref[...], s2_ref[...]
    for k in range(NA // LANES):
        sl = slice(k * LANES, (k + 1) * LANES)
        a_ref[:, sl] = _rope(z[:, sl], c, s1, s2).astype(BF16)
    kc_ref[...] = _rope(z[:, OFF_KC:OFF_KC + 128], c, s1, s2)
    v_ref[...] = z[:, OFF_V:OFF_V + NV].astype(BF16)
    vc_ref[...] = z[:, OFF_VC:OFF_VC + 128]
    m_ref[...] = z[:, OFF_M:OFF_M + NM]
    gm_ref[...] = z[:, OFF_G:OFF_G + NG]


def _proj(x2, sc, sh, g, w, cos, s1, s2, seq):
    t, d = x2.shape
    tm = 256
    nb = seq // tm
    row = lambda i: (i, 0)
    bat = lambda i: (i // nb, 0, 0)
    pos = lambda i: (i % nb, 0)
    full = lambda i: (0, 0)
    widths = (NA, 128, NV, 128, NM, NG)
    dtypes = (BF16, F32, BF16, F32, F32, F32)
    return pl.pallas_call(
        _proj_kernel,
        grid=(t // tm,),
        in_specs=[pl.BlockSpec((tm, d), row),
                  pl.BlockSpec((None, 1, d), bat),
                  pl.BlockSpec((None, 1, d), bat),
                  pl.BlockSpec((1, d), full),
                  pl.BlockSpec((d, NZ), full),
                  pl.BlockSpec((tm, LANES), pos),
                  pl.BlockSpec((tm, LANES), pos),
                  pl.BlockSpec((tm, LANES), pos)],
        out_specs=[pl.BlockSpec((tm, n), row) for n in widths],
        out_shape=[jax.ShapeDtypeStruct((t, n), dt) for n, dt in zip(widths, dtypes)],
        compiler_params=_cparams("arbitrary"),
        name="proj",
    )(x2, sc, sh, g, w, cos, s1, s2)


def _masked_attend(s, mask, v):
    s = jnp.where(mask, s, NEG_INF)
    m = jnp.max(s, axis=-1, keepdims=True)
    p = jnp.where(mask, jnp.exp(s - m), 0.0)
    den = jnp.maximum(jnp.sum(p, axis=-1, keepdims=True), 1e-30)
    return _dot(p.astype(BF16), v) / den, p, den


def _nsa_kernel(q_ref, ks_ref, kw_ref, vs_ref, vw_ref, kc_ref, vc_ref, misc_ref,
                wak_ref, wbk_ref, w2k_ref, wav_ref, wbv_ref, w2v_ref, posa_ref, posb_ref,
                mimp_ref, eexp_ref, o_ref, kcmp_ref, vcmp_ref, oslc_ref, *, tq, seq):
    i = pl.program_id(1)
    ncmp = seq // CMP_STRIDE
    scale = HEAD_DIM ** -0.5

    @pl.when(i == 0)
    def _():
        def compress(c_ref, wa_ref, wb_ref, w2_ref):
            c = c_ref[...]
            ha = _dot((c + posa_ref[...]).astype(BF16), wa_ref[...])
            hb = _dot((c + posb_ref[...]).astype(BF16), wb_ref[...])
            hid = ha + pltpu.roll(hb, ncmp - 1, 0)
            return _dot(jax.nn.gelu(hid).astype(BF16), w2_ref[...])
        kcmp_ref[...] = compress(kc_ref, wak_ref, wbk_ref, w2k_ref).astype(BF16)
        vcmp_ref[...] = compress(vc_ref, wav_ref, wbv_ref, w2v_ref).astype(BF16)

    t0 = i * tq
    trow = t0 + lax.broadcasted_iota(I32, (tq, LANES), 0)
    lane = lax.broadcasted_iota(I32, (tq, LANES), 1)
    sig = jax.nn.sigmoid(misc_ref[...])

    hpg = NSA_HEADS // NSA_KV_HEADS
    qs = [_half_mask(q_ref[:, (h // 2) * LANES:(h // 2 + 1) * LANES], h % 2) * scale for h in range(NSA_HEADS)]

    cmp_mask = (lane * CMP_STRIDE + (CMP_BLOCK - 1)) <= trow
    o_cmp, imp = [], []
    for h in range(NSA_HEADS):
        g = h // hpg
        s = _dot_nt(qs[h], kcmp_ref[:, g * LANES:(g + 1) * LANES])
        o, p, den = _masked_attend(s, cmp_mask, vcmp_ref[:, g * LANES:(g + 1) * LANES])
        o_cmp.append(o)
        pn = p / den
        if h % hpg == 0:
            imp.append(pn)
        else:
            imp[g] = imp[g] + pn

    chosen = []
    for g in range(NSA_KV_HEADS):
        pg = imp[g]
        p_hi = pg.astype(BF16)
        r1 = pg - p_hi.astype(F32)
        p_mid = r1.astype(BF16)
        p_lo = (r1 - p_mid.astype(F32)).astype(BF16)
        blk = _dot(p_hi, mimp_ref[...]) + _dot(p_mid, mimp_ref[...]) + _dot(p_lo, mimp_ref[...])
        cur = trow // SLC_BLOCK
        forced = (lane == 0) | (lane == cur) | (lane == cur - 1)
        admissible = lane * SLC_BLOCK <= trow
        val = jnp.where(forced, jnp.inf, jnp.where(admissible, blk, -jnp.inf))
        rank = jnp.zeros((tq, LANES), I32)
        for j in range(seq // SLC_BLOCK):
            vj = jnp.broadcast_to(val[:, j:j + 1], (tq, LANES))
            ahead = (vj > val) | ((vj == val) & (lane > j))
            rank = rank + ahead.astype(I32)
        chosen.append((rank < SLC_TOPN).astype(BF16))

    def biased_attend(q, k, v, bias):
        s = _dot_nt(q, k) + bias
        p = jnp.exp(s - jnp.max(s, axis=-1, keepdims=True))
        den = jnp.maximum(jnp.sum(p, axis=-1, keepdims=True), 1e-30)
        return _dot(p.astype(BF16), v) / den

    def selected_branch(nk):
        causal = lax.broadcasted_iota(I32, (tq, nk), 1) <= t0 + lax.broadcasted_iota(I32, (tq, nk), 0)
        for g in range(NSA_KV_HEADS):
            picked = _dot(chosen[g], eexp_ref[:, 0:nk]) > 0.5
            bias = jnp.where(picked & causal, 0.0, NEG_INF)
            ksd = ks_ref[0:nk, g * LANES:(g + 1) * LANES]
            vsd = vs_ref[0:nk, g * LANES:(g + 1) * LANES]
            for h in range(g * hpg, (g + 1) * hpg):
                oslc_ref[h] = biased_attend(qs[h], ksd, vsd, bias)

    step = seq // NSA_KEY_VARIANTS
    per = step // tq
    for v in range(NSA_KEY_VARIANTS):
        pl.when(i // per == v)(functools.partial(selected_branch, (v + 1) * step))

    kspan = WINDOW + tq
    kstart = pl.multiple_of(jnp.maximum(t0 - WINDOW, 0), tq)
    diff_w = (t0 + lax.broadcasted_iota(I32, (tq, kspan), 0)) - (kstart + lax.broadcasted_iota(I32, (tq, kspan), 1))
    win_bias = jnp.where((diff_w >= 0) & (diff_w < WINDOW), 0.0, NEG_INF)
    pair = None
    for h in range(NSA_HEADS):
        g = h // hpg
        kwd = kw_ref[pl.ds(kstart, kspan), g * LANES:(g + 1) * LANES]
        vwd = vw_ref[pl.ds(kstart, kspan), g * LANES:(g + 1) * LANES]
        o_w = biased_attend(qs[h], kwd, vwd, win_bias)
        c0 = MISC_GATE + 3 * h
        o_h = sig[:, c0:c0 + 1] * o_cmp[h] + sig[:, c0 + 1:c0 + 2] * oslc_ref[h] + sig[:, c0 + 2:c0 + 3] * o_w
        if h % 2 == 0:
            pair = o_h
        else:
            o_ref[:, (h // 2) * LANES:(h // 2 + 1) * LANES] = jnp.where(lane < HEAD_DIM, pair, o_h).astype(BF16)


def _nsa(a3, v3, kcr, vcr, m3, wts, seq):
    bsz = a3.shape[0]
    tq = 128
    ncmp = seq // CMP_STRIDE
    per_b = lambda blk: (lambda b, i: (b, 0, blk))
    full2 = lambda b, i: (0, 0)
    w_specs = [pl.BlockSpec(w.shape, full2) for w in wts]
    return pl.pallas_call(
        functools.partial(_nsa_kernel, tq=tq, seq=seq),
        grid=(bsz, seq // tq),
        in_specs=[pl.BlockSpec((None, tq, 512), lambda b, i: (b, i, 0)),
                  pl.BlockSpec((None, seq, 256), per_b(2)),
                  pl.BlockSpec((None, seq, 256), per_b(3)),
                  pl.BlockSpec((None, seq, 256), per_b(0)),
                  pl.BlockSpec((None, seq, 256), per_b(1)),
                  pl.BlockSpec((None, ncmp, CMP_STRIDE * 128), per_b(0)),
                  pl.BlockSpec((None, ncmp, CMP_STRIDE * 128), per_b(0)),
                  pl.BlockSpec((None, tq, LANES), lambda b, i: (b, i, 1)),
                  ] + w_specs,
        out_specs=pl.BlockSpec((None, tq, 512), lambda b, i: (b, i, 0)),
        out_shape=jax.ShapeDtypeStruct((bsz, seq, 512), BF16),
        scratch_shapes=[pltpu.VMEM((ncmp, 256), BF16), pltpu.VMEM((ncmp, 256), BF16),
                        pltpu.VMEM((NSA_HEADS, tq, LANES), F32)],
        compiler_params=_cparams("arbitrary", "arbitrary"),
        name="nsa",
    )(a3, a3, a3, v3, v3, kcr, vcr, m3, *wts)


def _dsa_kernel(qd_ref, qi_ref, ki_ref, ckv_ref, misc_ref, gkv_ref, wuk_ref, wuv_ref,
                cos_ref, s1_ref, s2_ref, o_ref, kd_ref, vt_ref, *, tq, seq, topk):
    i = pl.program_id(1)

    @pl.when(i == 0)
    def _():
        ck = ckv_ref[...]
        cn = ck * lax.rsqrt(jnp.mean(ck * ck, axis=-1, keepdims=True) + NORM_EPS) * gkv_ref[...]
        cb = cn.astype(BF16)
        kd_ref[...] = _rope(_dot(cb, wuk_ref[...]), cos_ref[...], s1_ref[...], s2_ref[...]).astype(BF16)
        vt_ref[...] = _dot(cb, wuv_ref[...]).T.astype(BF16)

    t0 = i * tq
    misc_t = misc_ref[...].T
    int_min = jnp.int32(-2 ** 31)
    idx_bits = max(1, (seq - 1).bit_length()) + 1

    def attend(nk):
        kpos = lax.broadcasted_iota(I32, (nk, tq), 0)
        tpos = t0 + lax.broadcasted_iota(I32, (nk, tq), 1)
        causal = kpos <= tpos

        score = jnp.zeros((nk, tq), F32)
        for h in range(IDX_HEADS):
            qm = _half_mask(qi_ref[:, (h // 2) * LANES:(h // 2 + 1) * LANES], h % 2) * (IDX_DIM ** -0.5)
            logit = _dot_nt(ki_ref[0:nk, :], qm)
            w_h = misc_t[MISC_WI + h:MISC_WI + h + 1, :] * (IDX_HEADS ** -0.5)
            score = score + w_h * jnp.maximum(logit, 0.0)
        score = jnp.where(causal, score, -jnp.inf)

        bits = lax.bitcast_convert_type(score, I32)
        key = bits ^ ((bits >> 31) & jnp.int32(0x7FFFFFFF))

        def bit_step(it, tu):
            cu = tu | lax.shift_left(jnp.int32(1), 31 - it)
            cnt = _colsum((key >= (cu ^ int_min)).astype(F32))
            return jnp.where(cnt >= topk, cu, tu)
        thr = lax.fori_loop(0, 32, bit_step, jnp.zeros((1, tq), I32)) ^ int_min
        above = key > thr
        need = topk - _colsum(above.astype(F32))
        tie_pos = jnp.where(key == thr, kpos, jnp.int32(2 * seq))

        def tie_step(it, lim):
            cl = lim | lax.shift_left(jnp.int32(1), idx_bits - 1 - it)
            cnt = _colsum((tie_pos < cl).astype(F32))
            return jnp.where(cnt <= need, cl, lim)
        lim = lax.fori_loop(0, idx_bits, tie_step, jnp.zeros((1, tq), I32))
        sel = causal & (above | (tie_pos < lim))
        bias = jnp.where(sel, 0.0, NEG_INF)

        row = lax.broadcasted_iota(I32, (LANES, tq), 0)
        pair = None
        for h in range(DSA_HEADS):
            qm = _half_mask(qd_ref[:, (h // 2) * LANES:(h // 2 + 1) * LANES], h % 2) * (HEAD_DIM ** -0.5)
            s = _dot_nt(kd_ref[0:nk, :], qm) + bias
            p = jnp.exp(s - _colmax(s))
            den = jnp.maximum(_colsum(p), 1e-30)
            o_t = _dot(vt_ref[:, 0:nk], p.astype(BF16)) / den
            if h % 2 == 0:
                pair = o_t
            else:
                o_ref[:, (h // 2) * LANES:(h // 2 + 1) * LANES] = (
                    jnp.where(row < HEAD_DIM, pair, o_t).T.astype(BF16))

    step = seq // DSA_KEY_VARIANTS
    per = step // tq
    for v in range(DSA_KEY_VARIANTS):
        pl.when(i // per == v)(functools.partial(attend, (v + 1) * step))


def _dsa(a3, m3, gkv, wuk, wuv, cos, s1, s2, seq):
    bsz = a3.shape[0]
    tq = 128
    topk = min(DSA_TOPK, seq // 4)
    full2 = lambda b, i: (0, 0)
    return pl.pallas_call(
        functools.partial(_dsa_kernel, tq=tq, seq=seq, topk=topk),
        grid=(bsz, seq // tq),
        in_specs=[pl.BlockSpec((None, tq, 512), lambda b, i: (b, i, 2)),
                  pl.BlockSpec((None, tq, 512), lambda b, i: (b, i, 3)),
                  pl.BlockSpec((None, seq, LANES), lambda b, i: (b, 0, 16)),
                  pl.BlockSpec((None, seq, LANES), lambda b, i: (b, 0, 0)),
                  pl.BlockSpec((None, tq, LANES), lambda b, i: (b, i, 1)),
                  pl.BlockSpec((1, LANES), full2),
                  pl.BlockSpec((LANES, LANES), full2),
                  pl.BlockSpec((LANES, LANES), full2),
                  pl.BlockSpec((seq, LANES), full2),
                  pl.BlockSpec((seq, LANES), full2),
                  pl.BlockSpec((seq, LANES), full2)],
        out_specs=pl.BlockSpec((None, tq, 512), lambda b, i: (b, i, 0)),
        out_shape=jax.ShapeDtypeStruct((bsz, seq, 512), BF16),
        scratch_shapes=[pltpu.VMEM((seq, LANES), BF16), pltpu.VMEM((LANES, seq), BF16)],
        compiler_params=_cparams("arbitrary", "arbitrary"),
        name="dsa",
    )(a3, a3, a3, m3, m3, gkv, wuk, wuv, cos, s1, s2)


def _merge_kernel(on_ref, od_ref, gm_ref, x_ref, g1_ref, sc_ref, sh_ref, n2_ref,
                  wn_ref, wd_ref, wo_ref, x1_ref, h2_ref):
    gm = gm_ref[...]
    merged = (jax.nn.sigmoid(gm[:, :D_MODEL]) * _dot(on_ref[...], wn_ref[...])
              + jax.nn.sigmoid(gm[:, D_MODEL:]) * _dot(od_ref[...], wd_ref[...]))
    x1 = x_ref[...] + g1_ref[...] * _dot(merged.astype(BF16), wo_ref[...])
    x1_ref[...] = x1
    y = x1 * lax.rsqrt(jnp.mean(x1 * x1, axis=-1, keepdims=True) + NORM_EPS) * n2_ref[...]
    h2_ref[...] = (y * (1.0 + sc_ref[...]) + sh_ref[...]).astype(BF16)


def _merge(o_nsa, o_dsa, gm, x2, g1, sc2, sh2, n2, wn, wd, wo, seq):
    t, d = x2.shape
    tm = 256
    nb = seq // tm
    row = lambda i: (i, 0)
    bat = lambda i: (i // nb, 0, 0)
    full = lambda i: (0, 0)
    return pl.pallas_call(
        _merge_kernel,
        grid=(t // tm,),
        in_specs=[pl.BlockSpec((tm, 512), row), pl.BlockSpec((tm, 512), row),
                  pl.BlockSpec((tm, NG), row), pl.BlockSpec((tm, d), row),
                  pl.BlockSpec((None, 1, d), bat), pl.BlockSpec((None, 1, d), bat),
                  pl.BlockSpec((None, 1, d), bat), pl.BlockSpec((1, d), full),
                  pl.BlockSpec(wn.shape, full), pl.BlockSpec(wd.shape, full), pl.BlockSpec(wo.shape, full)],
        out_specs=[pl.BlockSpec((tm, d), row), pl.BlockSpec((tm, d), row)],
        out_shape=[jax.ShapeDtypeStruct((t, d), F32), jax.ShapeDtypeStruct((t, d), BF16)],
        compiler_params=_cparams("arbitrary"),
        name="merge",
    )(o_nsa, o_dsa, gm, x2, g1, sc2, sh2, n2, wn, wd, wo)


def _topk_rows(vals, k):
    n, t = vals.shape
    ridx = lax.broadcasted_iota(I32, (n, t), 0).astype(F32)
    slot = lax.broadcasted_iota(I32, (k, t), 0)
    top_v = jnp.zeros((k, t), F32)
    top_i = jnp.zeros((k, t), F32)
    for j in range(k):
        m = jnp.max(vals, axis=0, keepdims=True)
        idx = jnp.min(jnp.where(vals == m, ridx, float(n)), axis=0, keepdims=True)
        top_v = jnp.where(slot == j, m, top_v)
        top_i = jnp.where(slot == j, idx, top_i)
        vals = jnp.where(ridx == idx, -jnp.inf, vals)
    return top_v, top_i


def _route_kernel(h_ref, wq_ref, k2_ref, i_ref, j_ref, g_ref, q3_ref, is_ref, js_ref, gs_ref, cand_ref, *, tm):
    q = _dot(h_ref[...], wq_ref[...]).astype(BF16)
    for h in range(PEER_HEADS):
        q3_ref[h] = q[:, h * LANES:(h + 1) * LANES]

    def head(h, carry):
        s = _dot_nt(k2_ref[h], q3_ref[h])
        s0, i0 = _topk_rows(s[:PEER_KEYS], PEER_TOPK)
        s1, i1 = _topk_rows(s[PEER_KEYS:], PEER_TOPK)
        r0 = 0
        for a in range(PEER_TOPK):
            nb = PEER_TOPK // (a + 1)
            cand_ref[r0:r0 + nb, :] = s0[a:a + 1, :] + s1[0:nb, :]
            r0 += nb
        cand_ref[r0:, :] = jnp.full((cand_ref.shape[0] - r0, tm), -jnp.inf, F32)
        top_s, top_r = _topk_rows(cand_ref[...], PEER_TOPK)
        pa = jnp.zeros((PEER_TOPK, tm), F32)
        pb = jnp.zeros((PEER_TOPK, tm), F32)
        r = 0
        for a in range(PEER_TOPK):
            for b in range(PEER_TOPK // (a + 1)):
                hit = top_r == r
                pa = jnp.where(hit, float(a), pa)
                pb = jnp.where(hit, float(b), pb)
                r += 1
        ii = jnp.zeros((PEER_TOPK, tm), F32)
        jj = jnp.zeros((PEER_TOPK, tm), F32)
        for a in range(PEER_TOPK):
            ii = jnp.where(pa == a, i0[a:a + 1, :], ii)
            jj = jnp.where(pb == a, i1[a:a + 1, :], jj)
        e = jnp.exp(top_s - jnp.max(top_s, axis=0, keepdims=True))
        gate = e / jnp.sum(e, axis=0, keepdims=True)
        rows = pl.ds(pl.multiple_of(h * PEER_TOPK, PEER_TOPK), PEER_TOPK)
        is_ref[rows, :] = ii
        js_ref[rows, :] = jj
        gs_ref[rows, :] = gate
        return carry
    lax.fori_loop(0, PEER_HEADS, head, 0)
    i_ref[...] = is_ref[...].T
    j_ref[...] = js_ref[...].T
    g_ref[...] = gs_ref[...].T


def _route(h2, wq, k2):
    t, d = h2.shape
    tm = 128
    row = lambda i: (i, 0)
    nk = PEER_HEADS * PEER_TOPK
    ncand = -(-sum(PEER_TOPK // (a + 1) for a in range(PEER_TOPK)) // 8) * 8
    return pl.pallas_call(
        functools.partial(_route_kernel, tm=tm),
        grid=(t // tm,),
        in_specs=[pl.BlockSpec((tm, d), row),
                  pl.BlockSpec(wq.shape, lambda i: (0, 0)),
                  pl.BlockSpec(k2.shape, lambda i: (0, 0, 0))],
        out_specs=[pl.BlockSpec((tm, nk), row)] * 3,
        out_shape=[jax.ShapeDtypeStruct((t, nk), F32)] * 3,
        scratch_shapes=[pltpu.VMEM((PEER_HEADS, tm, LANES), BF16),
                        pltpu.VMEM((nk, tm), F32), pltpu.VMEM((nk, tm), F32), pltpu.VMEM((nk, tm), F32),
                        pltpu.VMEM((ncand, tm), F32)],
        compiler_params=_cparams("arbitrary"),
        name="route",
    )(h2, wq, k2)


def _expert_kernel(h_ref, i_ref, j_ref, g_ref, u_ref, v_ref, x1_ref, g2_ref, fg_ref, o_ref,
                   w3_ref, y_ref, acc_ref, *, tm, ne, final_norm):
    c = pl.program_id(1)
    nsub = ne // PEER_KEYS

    @pl.when(c == 0)
    def _():
        acc_ref[...] = jnp.zeros_like(acc_ref)
        sub = lax.broadcasted_iota(I32, (PEER_KEYS, LANES), 0).astype(F32)

        def group(t8, carry):
            r0 = pl.multiple_of(t8 * 8, 8)
            ii = i_ref[pl.ds(r0, 8), :]
            jj = j_ref[pl.ds(r0, 8), :]
            gg = g_ref[pl.ds(r0, 8), :]
            for r in range(8):
                a = (sub == ii[r:r + 1, :]).astype(BF16)
                rg = jnp.where(sub == jj[r:r + 1, :], gg[r:r + 1, :], 0.0)
                r_hi = rg.astype(BF16)
                r_lo = (rg - r_hi.astype(F32)).astype(BF16)
                w = _dot_nt(jnp.concatenate([a, a], axis=1), jnp.concatenate([r_hi, r_lo], axis=1))
                w3_ref[pl.ds(pl.multiple_of((r0 + r) * W3_PITCH, 8), PEER_KEYS), :] = w
            return carry
        lax.fori_loop(0, tm // 8, group, 0)

    act = _dot_nt(h_ref[...], u_ref[...])
    for s in range(nsub):
        i_idx = c * nsub + s
        w = w3_ref[pl.ds(i_idx, tm, stride=W3_PITCH), :]
        a = act[:, s * LANES:(s + 1) * LANES]
        gelu = 0.5 * a * (1.0 + lax.erf(a * (2.0 ** -0.5)))
        y_ref[:, s * LANES:(s + 1) * LANES] = (w * gelu).astype(BF16)
    acc_ref[...] += _dot(y_ref[...], v_ref[...])

    @pl.when(c == pl.num_programs(1) - 1)
    def _():
        x2 = x1_ref[...] + g2_ref[...] * acc_ref[...]
        if final_norm:
            x2 = x2 * lax.rsqrt(jnp.mean(x2 * x2, axis=-1, keepdims=True) + NORM_EPS) * fg_ref[...]
        o_ref[...] = x2


def _experts(h2, ii, jj, gg, u, v, x1, g2, fg, seq, final_norm):
    t, d = h2.shape
    tm = 256
    ne = 2048
    nb = seq // tm
    nexp = u.shape[0]
    row = lambda i, c: (i, 0)
    return pl.pallas_call(
        functools.partial(_expert_kernel, tm=tm, ne=ne, final_norm=final_norm),
        grid=(t // tm, nexp // ne),
        in_specs=[pl.BlockSpec((tm, d), row),
                  pl.BlockSpec((tm, LANES), row), pl.BlockSpec((tm, LANES), row), pl.BlockSpec((tm, LANES), row),
                  pl.BlockSpec((ne, d), lambda i, c: (c, 0)),
                  pl.BlockSpec((ne, d), lambda i, c: (c, 0)),
                  pl.BlockSpec((tm, d), row),
                  pl.BlockSpec((None, 1, d), lambda i, c: (i // nb, 0, 0)),
                  pl.BlockSpec((1, d), lambda i, c: (0, 0))],
        out_specs=pl.BlockSpec((tm, d), row),
        out_shape=jax.ShapeDtypeStruct((t, d), F32),
        scratch_shapes=[pltpu.VMEM((tm * W3_PITCH, LANES), F32),
                        pltpu.VMEM((tm, ne), BF16),
                        pltpu.VMEM((tm, d), F32)],
        compiler_params=_cparams("arbitrary", "arbitrary"),
        name="experts",
    )(h2, ii, jj, gg, u, v, x1, g2, fg)


def _dup(w):
    return jnp.concatenate([w, w], axis=1)


def _win_layout(w_in):
    nq, nkv = NSA_HEADS * HEAD_DIM, NSA_KV_HEADS * HEAD_DIM
    sizes = (nq, nkv, nkv, nkv, nkv, nkv, nkv, 3 * NSA_HEADS, DSA_HEADS * HEAD_DIM, DSA_KV_RANK,
             IDX_HEADS * IDX_DIM, IDX_DIM, IDX_HEADS, 2 * D_MODEL)
    offs = [0]
    for s in sizes:
        offs.append(offs[-1] + s)
    q_n, kc, vc, ks, vs, kw, vw, g_n, q_d, ckv, qi, ki, wi, g_m = [w_in[:, offs[k]:offs[k + 1]] for k in range(14)]
    dup2 = lambda w: jnp.concatenate([_dup(w[:, :HEAD_DIM]), _dup(w[:, HEAD_DIM:])], axis=1)
    misc = jnp.concatenate([g_n, wi, jnp.zeros((w_in.shape[0], LANES - 32), w_in.dtype)], axis=1)
    cols = [q_n, dup2(ks), dup2(kw), q_d, qi, _dup(ki), kc, dup2(vs), dup2(vw), vc, ckv, misc, g_m]
    return jnp.concatenate(cols, axis=1).astype(BF16)


def _rope_tables(seq):
    half = HEAD_DIM // 2
    pos = jnp.arange(seq, dtype=F32)
    inv = ROPE_THETA ** (-jnp.arange(half, dtype=F32) / half)
    ang = pos[:, None] * inv[None, :]
    cos, sin, zero = jnp.cos(ang), jnp.sin(ang), jnp.zeros((seq, half), F32)
    c = jnp.concatenate([cos, cos, cos, cos], axis=1)
    s1 = jnp.concatenate([-sin, zero, -sin, zero], axis=1)
    s2 = jnp.concatenate([zero, sin, zero, sin], axis=1)
    return c, s1, s2


def _cmp_weights(pos, w1, w2):
    g = NSA_KV_HEADS
    eye = jnp.eye(g, dtype=F32)
    big = jnp.einsum("lde,gh->lgdhe", w1, eye).reshape(CMP_BLOCK, g * HEAD_DIM, g * HEAD_DIM)
    wa = big[:CMP_STRIDE].reshape(CMP_STRIDE * g * HEAD_DIM, g * HEAD_DIM)
    wb = big[CMP_STRIDE:].reshape(CMP_STRIDE * g * HEAD_DIM, g * HEAD_DIM)
    z = jnp.zeros_like(w2)
    w2d = jnp.concatenate([jnp.concatenate([w2, w2, z, z], axis=1),
                           jnp.concatenate([z, z, w2, w2], axis=1)], axis=0)
    pos_t = jnp.broadcast_to(pos[:, None, :], (CMP_BLOCK, g, HEAD_DIM))
    pa = pos_t[:CMP_STRIDE].reshape(1, -1)
    pb = pos_t[CMP_STRIDE:].reshape(1, -1)
    return wa.astype(BF16), wb.astype(BF16), w2d.astype(BF16), pa, pb


def _selection_constants(seq):
    ncmp = seq // CMP_STRIDE
    nslc = seq // SLC_BLOCK
    per = SLC_BLOCK // CMP_STRIDE
    n = jnp.arange(ncmp)[:, None]
    j = jnp.arange(LANES)[None, :]
    mimp = ((n // per == j).astype(F32) + ((n + 1) // per == j).astype(F32)) * (j < nslc)
    mimp = mimp * (n < ncmp - 1)
    key = jnp.arange(seq)[None, :]
    eexp = (key // SLC_BLOCK == jnp.arange(LANES)[:, None]).astype(F32)
    return mimp.astype(BF16), eexp.astype(BF16)


def _subkey_blocks(subkeys):
    z = jnp.zeros_like(subkeys[:, 0])
    top = jnp.concatenate([subkeys[:, 0], z], axis=2)
    bot = jnp.concatenate([z, subkeys[:, 1]], axis=2)
    return jnp.concatenate([top, bot], axis=1).astype(BF16)


def kernel(x, c, ada_w, ada_b, norm1_g, w_in, cmp_pos, cmp_w1_k, cmp_w2_k, cmp_w1_v, cmp_w2_v,
           kv_norm_g, w_uk, w_uv, w_br_nsa, w_br_dsa, w_out, norm2_g,
           peer_wq, peer_subkeys, peer_u, peer_v, final_g):
    bsz, seq, d = x.shape
    depth = ada_w.shape[0]
    t = bsz * seq
    cos, s1, s2 = _rope_tables(seq)
    mimp, eexp = _selection_constants(seq)
    xt = x.reshape(t, d)
    out = xt
    for l in range(depth):
        mod = _ada(c, ada_w[l], ada_b[l]).reshape(bsz, 1, 6 * d)
        sh1, sc1, g1, sh2, sc2, g2 = [mod[:, :, k * d:(k + 1) * d] for k in range(6)]
        a, kc, vv, vc, mm, gm = _proj(xt, sc1, sh1, norm1_g[l].reshape(1, d), _win_layout(w_in[l]),
                                      cos, s1, s2, seq)
        a3 = a.reshape(bsz, seq, NA)
        m3 = mm.reshape(bsz, seq, NM)
        ncmp = seq // CMP_STRIDE
        wak, wbk, w2k, pa, pb = _cmp_weights(cmp_pos[l], cmp_w1_k[l], cmp_w2_k[l])
        wav, wbv, w2v, _, _ = _cmp_weights(cmp_pos[l], cmp_w1_v[l], cmp_w2_v[l])
        o_nsa = _nsa(a3, vv.reshape(bsz, seq, NV), kc.reshape(bsz, ncmp, CMP_STRIDE * 128),
                     vc.reshape(bsz, ncmp, CMP_STRIDE * 128), m3,
                     (wak, wbk, w2k, wav, wbv, w2v, pa, pb, mimp, eexp), seq)
        o_dsa = _dsa(a3, m3, kv_norm_g[l].reshape(1, DSA_KV_RANK), _dup(w_uk[l]).astype(BF16),
                     _dup(w_uv[l]).astype(BF16), cos, s1, s2, seq)
        x1, h2 = _merge(o_nsa.reshape(t, 512), o_dsa.reshape(t, 512), gm, xt, g1, sc2, sh2,
                        norm2_g[l].reshape(1, d), w_br_nsa[l].astype(BF16), w_br_dsa[l].astype(BF16),
                        w_out[l].astype(BF16), seq)
        ii, jj, gg = _route(h2, peer_wq[l].astype(BF16), _subkey_blocks(peer_subkeys[l]))
        out = _experts(h2, ii, jj, gg, peer_u[l].astype(BF16), peer_v[l].astype(BF16), x1, g2,
                       final_g.reshape(1, d), seq, final_norm=(l == depth - 1))
        xt = out
    return out.reshape(bsz, seq, d)
```

```python
import functools

import jax
import jax.numpy as jnp
from jax import lax
from jax.experimental import pallas as pl
from jax.experimental.pallas import tpu as pltpu

F32 = jnp.float32
BF16 = jnp.bfloat16
I32 = jnp.int32

D_MODEL = 1024
HEAD_DIM = 64
ROPE_THETA = 10000.0
NORM_EPS = 1e-6
NEG_INF = -1e30

NSA_HEADS = 8
NSA_KV_HEADS = 2
CMP_BLOCK = 32
CMP_STRIDE = 16
SLC_BLOCK = 64
SLC_TOPN = 16
WINDOW = 512
NSA_KEY_VARIANTS = 8

DSA_HEADS = 8
DSA_KV_RANK = 128
IDX_HEADS = 8
IDX_DIM = 64
DSA_TOPK = 256
DSA_KEY_VARIANTS = 8

PEER_HEADS = 8
PEER_KEYS = 128
PEER_HALF = 64
PEER_TOPK = 16

LANES = 128
W3_PITCH = PEER_KEYS // 2 + 8
GATE_GROUP = 32
VMEM_LIMIT = 56 * 1024 * 1024

NA = 2176
NV = 512
NM = 256
NG = 2 * D_MODEL
OFF_KC = NA
OFF_V = OFF_KC + 128
OFF_VC = OFF_V + NV
OFF_M = OFF_VC + 128
OFF_G = OFF_M + NM
NZ = OFF_G + NG
MISC_GATE = 0
MISC_WI = 24


def _dot(a, b):
    return jnp.dot(a, b, preferred_element_type=F32)


def _dot_nt(a, b):
    return lax.dot_general(a, b, (((1,), (1,)), ((), ())), preferred_element_type=F32)


def _cparams(*sem):
    return pltpu.CompilerParams(dimension_semantics=sem, vmem_limit_bytes=VMEM_LIMIT)


def _rope(z, c, s1, s2):
    return z * c + pltpu.roll(z, 96, 1) * s1 + pltpu.roll(z, 32, 1) * s2


def _col_reduce(x, op, reduce):
    n = x.shape[0]
    while n > 8 and (n // 8) % 2 == 0:
        n //= 2
        x = op(x[:n], x[n:])
    return reduce(x, axis=0, keepdims=True)


def _colsum(x):
    return _col_reduce(x, jnp.add, jnp.sum)


def _colmax(x):
    return _col_reduce(x, jnp.maximum, jnp.max)


def _half_mask(x, odd):
    lane = lax.broadcasted_iota(I32, x.shape, 1)
    keep = (lane >= HEAD_DIM) if odd else (lane < HEAD_DIM)
    return jnp.where(keep, x, jnp.zeros_like(x))


def _ada_kernel(c_ref, w_ref, b_ref, o_ref):
    c = c_ref[...]
    sc = c * jax.nn.sigmoid(c)
    o_ref[...] = _dot(sc.astype(BF16), w_ref[...].astype(BF16)) + b_ref[...]


def _ada(c, w, b):
    bsz, d = c.shape
    n = w.shape[1]
    tn = 1024
    return pl.pallas_call(
        _ada_kernel,
        grid=(n // tn,),
        in_specs=[pl.BlockSpec((bsz, d), lambda j: (0, 0)),
                  pl.BlockSpec((d, tn), lambda j: (0, j)),
                  pl.BlockSpec((1, tn), lambda j: (0, j))],
        out_specs=pl.BlockSpec((bsz, tn), lambda j: (0, j)),
        out_shape=jax.ShapeDtypeStruct((bsz, n), F32),
        compiler_params=_cparams("arbitrary"),
        name="ada",
    )(c, w, b.reshape(1, n))


def _proj_kernel(x_ref, sc_ref, sh_ref, g_ref, w_ref, cos_ref, s1_ref, s2_ref,
                 a_ref, kc_ref, v_ref, vc_ref, m_ref, gm_ref):
    x = x_ref[...]
    y = x * lax.rsqrt(jnp.mean(x * x, axis=-1, keepdims=True) + NORM_EPS) * g_ref[...]
    h = y * (1.0 + sc_ref[...]) + sh_ref[...]
    z = _dot(h.astype(BF16), w_ref[...])
    c, s1, s2 = cos_ref[...], s1_ref[...], s2_ref[...]
    for k in range(NA // LANES):
        sl = slice(k * LANES, (k + 1) * LANES)
        a_ref[:, sl] = _rope(z[:, sl], c, s1, s2).astype(BF16)
    kc_ref[...] = _rope(z[:, OFF_KC:OFF_KC + 128], c, s1, s2)
    v_ref[...] = z[:, OFF_V:OFF_V + NV].astype(BF16)
    vc_ref[...] = z[:, OFF_VC:OFF_VC + 128]
    m_ref[...] = z[:, OFF_M:OFF_M + NM]
    gm_ref[...] = z[:, OFF_G:OFF_G + NG]


def _proj(x2, sc, sh, g, w, cos, s1, s2, seq):
    t, d = x2.shape
    tm = 256
    nb = seq // tm
    row = lambda i: (i, 0)
    bat = lambda i: (i // nb, 0, 0)
    pos = lambda i: (i % nb, 0)
    full = lambda i: (0, 0)
    widths = (NA, 128, NV, 128, NM, NG)
    dtypes = (BF16, F32, BF16, F32, F32, F32)
    return pl.pallas_call(
        _proj_kernel,
        grid=(t // tm,),
        in_specs=[pl.BlockSpec((tm, d), row),
                  pl.BlockSpec((None, 1, d), bat),
                  pl.BlockSpec((None, 1, d), bat),
                  pl.BlockSpec((1, d), full),
                  pl.BlockSpec((d, NZ), full),
                  pl.BlockSpec((tm, LANES), pos),
                  pl.BlockSpec((tm, LANES), pos),
                  pl.BlockSpec((tm, LANES), pos)],
        out_specs=[pl.BlockSpec((tm, n), row) for n in widths],
        out_shape=[jax.ShapeDtypeStruct((t, n), dt) for n, dt in zip(widths, dtypes)],
        compiler_params=_cparams("arbitrary"),
        name="proj",
    )(x2, sc, sh, g, w, cos, s1, s2)


def _masked_attend(s, mask, v):
    s = jnp.where(mask, s, NEG_INF)
    m = jnp.max(s, axis=-1, keepdims=True)
    p = jnp.where(mask, jnp.exp(s - m), 0.0)
    den = jnp.maximum(jnp.sum(p, axis=-1, keepdims=True), 1e-30)
    return _dot(p.astype(BF16), v) / den, p, den


def _nsa_kernel(q_ref, ks_ref, kw_ref, vs_ref, vw_ref, kc_ref, vc_ref, misc_ref,
                wak_ref, wbk_ref, w2k_ref, wav_ref, wbv_ref, w2v_ref, posa_ref, posb_ref,
                mimp_ref, eexp_ref, o_ref, kcmp_ref, vcmp_ref, oslc_ref, *, tq, seq):
    i = pl.program_id(1)
    ncmp = seq // CMP_STRIDE
    scale = HEAD_DIM ** -0.5

    @pl.when(i == 0)
    def _():
        def compress(c_ref, wa_ref, wb_ref, w2_ref):
            c = c_ref[...]
            ha = _dot((c + posa_ref[...]).astype(BF16), wa_ref[...])
            hb = _dot((c + posb_ref[...]).astype(BF16), wb_ref[...])
            hid = ha + pltpu.roll(hb, ncmp - 1, 0)
            return _dot(jax.nn.gelu(hid).astype(BF16), w2_ref[...])
        kcmp_ref[...] = compress(kc_ref, wak_ref, wbk_ref, w2k_ref).astype(BF16)
        vcmp_ref[...] = compress(vc_ref, wav_ref, wbv_ref, w2v_ref).astype(BF16)

    t0 = i * tq
    trow = t0 + lax.broadcasted_iota(I32, (tq, LANES), 0)
    lane = lax.broadcasted_iota(I32, (tq, LANES), 1)
    sig = jax.nn.sigmoid(misc_ref[...])

    hpg = NSA_HEADS // NSA_KV_HEADS
    qs = [_half_mask(q_ref[:, (h // 2) * LANES:(h // 2 + 1) * LANES], h % 2) * scale for h in range(NSA_HEADS)]

    cmp_mask = (lane * CMP_STRIDE + (CMP_BLOCK - 1)) <= trow
    o_cmp, imp = [], []
    for h in range(NSA_HEADS):
        g = h // hpg
        s = _dot_nt(qs[h], kcmp_ref[:, g * LANES:(g + 1) * LANES])
        o, p, den = _masked_attend(s, cmp_mask, vcmp_ref[:, g * LANES:(g + 1) * LANES])
        o_cmp.append(o)
        pn = p / den
        if h % hpg == 0:
            imp.append(pn)
        else:
            imp[g] = imp[g] + pn

    chosen = []
    for g in range(NSA_KV_HEADS):
        pg = imp[g]
        p_hi = pg.astype(BF16)
        r1 = pg - p_hi.astype(F32)
        p_mid = r1.astype(BF16)
        p_lo = (r1 - p_mid.astype(F32)).astype(BF16)
        blk_t = (_dot_nt(mimp_ref[...], p_hi) + _dot_nt(mimp_ref[...], p_mid) + _dot_nt(mimp_ref[...], p_lo))
        nslc = seq // SLC_BLOCK
        jrow = lax.broadcasted_iota(I32, (nslc, tq), 0)
        tcol = t0 + lax.broadcasted_iota(I32, (nslc, tq), 1)
        cur = tcol // SLC_BLOCK
        forced = (jrow == 0) | (jrow == cur) | (jrow == cur - 1)
        admissible = jrow * SLC_BLOCK <= tcol
        val = jnp.where(forced, jnp.inf, jnp.where(admissible, blk_t[0:nslc, :], -jnp.inf))
        rank = jnp.zeros((nslc, tq), I32)
        for j in range(nslc):
            vj = val[j:j + 1, :]
            ahead = (vj > val) | ((vj == val) & (jrow > j))
            rank = rank + ahead.astype(I32)
        chosen_t = jnp.concatenate([(rank < SLC_TOPN).astype(F32), jnp.zeros((LANES - nslc, tq), F32)], axis=0)
        chosen.append(chosen_t.T.astype(BF16))

    def biased_attend(q, k, v, bias):
        s = _dot_nt(q, k) + bias
        p = jnp.exp(s - jnp.max(s, axis=-1, keepdims=True))
        den = jnp.maximum(jnp.sum(p, axis=-1, keepdims=True), 1e-30)
        return _dot(p.astype(BF16), v) / den

    def selected_branch(nk):
        causal = lax.broadcasted_iota(I32, (tq, nk), 1) <= t0 + lax.broadcasted_iota(I32, (tq, nk), 0)
        for g in range(NSA_KV_HEADS):
            picked = _dot(chosen[g], eexp_ref[:, 0:nk]) > 0.5
            bias = jnp.where(picked & causal, 0.0, NEG_INF)
            ksd = ks_ref[0:nk, g * LANES:(g + 1) * LANES]
            vsd = vs_ref[0:nk, g * LANES:(g + 1) * LANES]
            for h in range(g * hpg, (g + 1) * hpg):
                oslc_ref[h] = biased_attend(qs[h], ksd, vsd, bias)

    step = seq // NSA_KEY_VARIANTS
    per = step // tq
    for v in range(NSA_KEY_VARIANTS):
        pl.when(i // per == v)(functools.partial(selected_branch, (v + 1) * step))

    kspan = WINDOW + tq
    kstart = pl.multiple_of(jnp.maximum(t0 - WINDOW, 0), tq)
    diff_w = (t0 + lax.broadcasted_iota(I32, (tq, kspan), 0)) - (kstart + lax.broadcasted_iota(I32, (tq, kspan), 1))
    win_bias = jnp.where((diff_w >= 0) & (diff_w < WINDOW), 0.0, NEG_INF)
    pair = None
    for h in range(NSA_HEADS):
        g = h // hpg
        kwd = kw_ref[pl.ds(kstart, kspan), g * LANES:(g + 1) * LANES]
        vwd = vw_ref[pl.ds(kstart, kspan), g * LANES:(g + 1) * LANES]
        o_w = biased_attend(qs[h], kwd, vwd, win_bias)
        c0 = MISC_GATE + 3 * h
        o_h = sig[:, c0:c0 + 1] * o_cmp[h] + sig[:, c0 + 1:c0 + 2] * oslc_ref[h] + sig[:, c0 + 2:c0 + 3] * o_w
        if h % 2 == 0:
            pair = o_h
        else:
            o_ref[:, (h // 2) * LANES:(h // 2 + 1) * LANES] = jnp.where(lane < HEAD_DIM, pair, o_h).astype(BF16)


def _nsa(a3, v3, kcr, vcr, m3, wts, seq):
    bsz = a3.shape[0]
    tq = 128
    ncmp = seq // CMP_STRIDE
    per_b = lambda blk: (lambda b, i: (b, 0, blk))
    full2 = lambda b, i: (0, 0)
    w_specs = [pl.BlockSpec(w.shape, full2) for w in wts]
    return pl.pallas_call(
        functools.partial(_nsa_kernel, tq=tq, seq=seq),
        grid=(bsz, seq // tq),
        in_specs=[pl.BlockSpec((None, tq, 512), lambda b, i: (b, i, 0)),
                  pl.BlockSpec((None, seq, 256), per_b(2)),
                  pl.BlockSpec((None, seq, 256), per_b(3)),
                  pl.BlockSpec((None, seq, 256), per_b(0)),
                  pl.BlockSpec((None, seq, 256), per_b(1)),
                  pl.BlockSpec((None, ncmp, CMP_STRIDE * 128), per_b(0)),
                  pl.BlockSpec((None, ncmp, CMP_STRIDE * 128), per_b(0)),
                  pl.BlockSpec((None, tq, LANES), lambda b, i: (b, i, 1)),
                  ] + w_specs,
        out_specs=pl.BlockSpec((None, tq, 512), lambda b, i: (b, i, 0)),
        out_shape=jax.ShapeDtypeStruct((bsz, seq, 512), BF16),
        scratch_shapes=[pltpu.VMEM((ncmp, 256), BF16), pltpu.VMEM((ncmp, 256), BF16),
                        pltpu.VMEM((NSA_HEADS, tq, LANES), F32)],
        compiler_params=_cparams("arbitrary", "arbitrary"),
        name="nsa",
    )(a3, a3, a3, v3, v3, kcr, vcr, m3, *wts)


def _dsa_kernel(qd_ref, qi_ref, ki_ref, ckv_ref, misc_ref, gkv_ref, wuk_ref, wuv_ref,
                cos_ref, s1_ref, s2_ref, o_ref, kd_ref, vt_ref, *, tq, seq, topk):
    i = pl.program_id(1)

    @pl.when(i == 0)
    def _():
        ck = ckv_ref[...]
        cn = ck * lax.rsqrt(jnp.mean(ck * ck, axis=-1, keepdims=True) + NORM_EPS) * gkv_ref[...]
        cb = cn.astype(BF16)
        kd_ref[...] = _rope(_dot(cb, wuk_ref[...]), cos_ref[...], s1_ref[...], s2_ref[...]).astype(BF16)
        vt_ref[...] = _dot(cb, wuv_ref[...]).T.astype(BF16)

    t0 = i * tq
    misc_t = misc_ref[...].T
    int_min = jnp.int32(-2 ** 31)
    idx_bits = max(1, (seq - 1).bit_length()) + 1

    def attend(nk):
        kpos = lax.broadcasted_iota(I32, (nk, tq), 0)
        tpos = t0 + lax.broadcasted_iota(I32, (nk, tq), 1)
        causal = kpos <= tpos

        score = jnp.zeros((nk, tq), F32)
        for h in range(IDX_HEADS):
            qm = _half_mask(qi_ref[:, (h // 2) * LANES:(h // 2 + 1) * LANES], h % 2) * (IDX_DIM ** -0.5)
            logit = _dot_nt(ki_ref[0:nk, :], qm)
            w_h = misc_t[MISC_WI + h:MISC_WI + h + 1, :] * (IDX_HEADS ** -0.5)
            score = score + w_h * jnp.maximum(logit, 0.0)
        score = jnp.where(causal, score, -jnp.inf)

        def as_float(cu):
            ks = cu ^ int_min
            return lax.bitcast_convert_type(ks ^ ((ks >> 31) & jnp.int32(0x7FFFFFFF)), F32)

        def bit_step(it, tu):
            cu = tu | lax.shift_left(jnp.int32(1), 31 - it)
            cnt = _colsum((score >= as_float(cu)).astype(F32))
            below_neg_inf = (cu >= 0) & (cu < jnp.int32(0x007FFFFF))
            return jnp.where((cnt >= topk) | below_neg_inf, cu, tu)
        thr = as_float(lax.fori_loop(0, 32, bit_step, jnp.zeros((1, tq), I32)))
        above = score > thr
        need = topk - _colsum(above.astype(F32))
        tie_pos = jnp.where(score == thr, kpos, jnp.int32(2 * seq))

        def tie_step(it, lim):
            cl = lim | lax.shift_left(jnp.int32(1), idx_bits - 1 - it)
            cnt = _colsum((tie_pos < cl).astype(F32))
            return jnp.where(cnt <= need, cl, lim)
        lim = lax.fori_loop(0, idx_bits, tie_step, jnp.zeros((1, tq), I32))
        sel = causal & (above | (tie_pos < lim))
        bias = jnp.where(sel, 0.0, NEG_INF)

        row = lax.broadcasted_iota(I32, (LANES, tq), 0)
        pair = None
        for h in range(DSA_HEADS):
            qm = _half_mask(qd_ref[:, (h // 2) * LANES:(h // 2 + 1) * LANES], h % 2) * (HEAD_DIM ** -0.5)
            s = _dot_nt(kd_ref[0:nk, :], qm) + bias
            p = jnp.exp(s - _colmax(s))
            den = jnp.maximum(_colsum(p), 1e-30)
            o_t = _dot(vt_ref[:, 0:nk], p.astype(BF16)) / den
            if h % 2 == 0:
                pair = o_t
            else:
                o_ref[:, (h // 2) * LANES:(h // 2 + 1) * LANES] = (
                    jnp.where(row < HEAD_DIM, pair, o_t).T.astype(BF16))

    step = seq // DSA_KEY_VARIANTS
    per = step // tq
    for v in range(DSA_KEY_VARIANTS):
        pl.when(i // per == v)(functools.partial(attend, (v + 1) * step))


def _dsa(a3, m3, gkv, wuk, wuv, cos, s1, s2, seq):
    bsz = a3.shape[0]
    tq = 128
    topk = min(DSA_TOPK, seq // 4)
    full2 = lambda b, i: (0, 0)
    return pl.pallas_call(
        functools.partial(_dsa_kernel, tq=tq, seq=seq, topk=topk),
        grid=(bsz, seq // tq),
        in_specs=[pl.BlockSpec((None, tq, 512), lambda b, i: (b, i, 2)),
                  pl.BlockSpec((None, tq, 512), lambda b, i: (b, i, 3)),
                  pl.BlockSpec((None, seq, LANES), lambda b, i: (b, 0, 16)),
                  pl.BlockSpec((None, seq, LANES), lambda b, i: (b, 0, 0)),
                  pl.BlockSpec((None, tq, LANES), lambda b, i: (b, i, 1)),
                  pl.BlockSpec((1, LANES), full2),
                  pl.BlockSpec((LANES, LANES), full2),
                  pl.BlockSpec((LANES, LANES), full2),
                  pl.BlockSpec((seq, LANES), full2),
                  pl.BlockSpec((seq, LANES), full2),
                  pl.BlockSpec((seq, LANES), full2)],
        out_specs=pl.BlockSpec((None, tq, 512), lambda b, i: (b, i, 0)),
        out_shape=jax.ShapeDtypeStruct((bsz, seq, 512), BF16),
        scratch_shapes=[pltpu.VMEM((seq, LANES), BF16), pltpu.VMEM((LANES, seq), BF16)],
        compiler_params=_cparams("arbitrary", "arbitrary"),
        name="dsa",
    )(a3, a3, a3, m3, m3, gkv, wuk, wuv, cos, s1, s2)


def _merge_kernel(on_ref, od_ref, gm_ref, x_ref, g1_ref, sc_ref, sh_ref, n2_ref,
                  wn_ref, wd_ref, wo_ref, x1_ref, h2_ref):
    gm = gm_ref[...]
    merged = (jax.nn.sigmoid(gm[:, :D_MODEL]) * _dot(on_ref[...], wn_ref[...])
              + jax.nn.sigmoid(gm[:, D_MODEL:]) * _dot(od_ref[...], wd_ref[...]))
    x1 = x_ref[...] + g1_ref[...] * _dot(merged.astype(BF16), wo_ref[...])
    x1_ref[...] = x1
    y = x1 * lax.rsqrt(jnp.mean(x1 * x1, axis=-1, keepdims=True) + NORM_EPS) * n2_ref[...]
    h2_ref[...] = (y * (1.0 + sc_ref[...]) + sh_ref[...]).astype(BF16)


def _merge(o_nsa, o_dsa, gm, x2, g1, sc2, sh2, n2, wn, wd, wo, seq):
    t, d = x2.shape
    tm = 256
    nb = seq // tm
    row = lambda i: (i, 0)
    bat = lambda i: (i // nb, 0, 0)
    full = lambda i: (0, 0)
    return pl.pallas_call(
        _merge_kernel,
        grid=(t // tm,),
        in_specs=[pl.BlockSpec((tm, 512), row), pl.BlockSpec((tm, 512), row),
                  pl.BlockSpec((tm, NG), row), pl.BlockSpec((tm, d), row),
                  pl.BlockSpec((None, 1, d), bat), pl.BlockSpec((None, 1, d), bat),
                  pl.BlockSpec((None, 1, d), bat), pl.BlockSpec((1, d), full),
                  pl.BlockSpec(wn.shape, full), pl.BlockSpec(wd.shape, full), pl.BlockSpec(wo.shape, full)],
        out_specs=[pl.BlockSpec((tm, d), row), pl.BlockSpec((tm, d), row)],
        out_shape=[jax.ShapeDtypeStruct((t, d), F32), jax.ShapeDtypeStruct((t, d), BF16)],
        compiler_params=_cparams("arbitrary"),
        name="merge",
    )(o_nsa, o_dsa, gm, x2, g1, sc2, sh2, n2, wn, wd, wo)


def _topk_rows(vals, k):
    n, t = vals.shape
    ridx = lax.broadcasted_iota(I32, (n, t), 0).astype(F32)
    slot = lax.broadcasted_iota(I32, (k, t), 0)
    top_v = jnp.zeros((k, t), F32)
    top_i = jnp.zeros((k, t), F32)
    for j in range(k):
        m = jnp.max(vals, axis=0, keepdims=True)
        idx = jnp.min(jnp.where(vals == m, ridx, float(n)), axis=0, keepdims=True)
        top_v = jnp.where(slot == j, m, top_v)
        top_i = jnp.where(slot == j, idx, top_i)
        vals = jnp.where(ridx == idx, -jnp.inf, vals)
    return top_v, top_i


def _route_kernel(h_ref, wq_ref, k2_ref, i_ref, j_ref, g_ref, q3_ref, is_ref, js_ref, gs_ref, cand_ref, *, tm):
    q = _dot(h_ref[...], wq_ref[...]).astype(BF16)
    for h in range(PEER_HEADS):
        q3_ref[h] = q[:, h * LANES:(h + 1) * LANES]

    def head(h, carry):
        s = _dot_nt(k2_ref[h], q3_ref[h])
        s0, i0 = _topk_rows(s[:PEER_KEYS], PEER_TOPK)
        s1, i1 = _topk_rows(s[PEER_KEYS:], PEER_TOPK)
        r0 = 0
        for a in range(PEER_TOPK):
            nb = PEER_TOPK // (a + 1)
            cand_ref[r0:r0 + nb, :] = s0[a:a + 1, :] + s1[0:nb, :]
            r0 += nb
        cand_ref[r0:, :] = jnp.full((cand_ref.shape[0] - r0, tm), -jnp.inf, F32)
        top_s, top_r = _topk_rows(cand_ref[...], PEER_TOPK)
        pa = jnp.zeros((PEER_TOPK, tm), F32)
        pb = jnp.zeros((PEER_TOPK, tm), F32)
        r = 0
        for a in range(PEER_TOPK):
            for b in range(PEER_TOPK // (a + 1)):
                hit = top_r == r
                pa = jnp.where(hit, float(a), pa)
                pb = jnp.where(hit, float(b), pb)
                r += 1
        ii = jnp.zeros((PEER_TOPK, tm), F32)
        jj = jnp.zeros((PEER_TOPK, tm), F32)
        for a in range(PEER_TOPK):
            ii = jnp.where(pa == a, i0[a:a + 1, :], ii)
            jj = jnp.where(pb == a, i1[a:a + 1, :], jj)
        e = jnp.exp(top_s - jnp.max(top_s, axis=0, keepdims=True))
        gate = e / jnp.sum(e, axis=0, keepdims=True)
        rows = pl.ds(pl.multiple_of(h * PEER_TOPK, PEER_TOPK), PEER_TOPK)
        is_ref[rows, :] = ii
        js_ref[rows, :] = jj
        gs_ref[rows, :] = gate
        return carry
    lax.fori_loop(0, PEER_HEADS, head, 0)
    i_ref[...] = is_ref[...].T
    j_ref[...] = js_ref[...].T
    g_ref[...] = gs_ref[...].T


def _route(h2, wq, k2):
    t, d = h2.shape
    tm = 128
    row = lambda i: (i, 0)
    nk = PEER_HEADS * PEER_TOPK
    ncand = -(-sum(PEER_TOPK // (a + 1) for a in range(PEER_TOPK)) // 8) * 8
    return pl.pallas_call(
        functools.partial(_route_kernel, tm=tm),
        grid=(t // tm,),
        in_specs=[pl.BlockSpec((tm, d), row),
                  pl.BlockSpec(wq.shape, lambda i: (0, 0)),
                  pl.BlockSpec(k2.shape, lambda i: (0, 0, 0))],
        out_specs=[pl.BlockSpec((tm, nk), row)] * 3,
        out_shape=[jax.ShapeDtypeStruct((t, nk), F32)] * 3,
        scratch_shapes=[pltpu.VMEM((PEER_HEADS, tm, LANES), BF16),
                        pltpu.VMEM((nk, tm), F32), pltpu.VMEM((nk, tm), F32), pltpu.VMEM((nk, tm), F32),
                        pltpu.VMEM((ncand, tm), F32)],
        compiler_params=_cparams("arbitrary"),
        name="route",
    )(h2, wq, k2)


def _gelu_exact(a):
    return 0.5 * a * (1.0 + lax.erf(a * (2.0 ** -0.5)))


def _expert_kernel(h_ref, i_ref, j_ref, g_ref, u_ref, v_ref, x1_ref, g2_ref, fg_ref, o_ref,
                   w3_ref, y_ref, acc_ref, *, tm, nsub, final_norm):
    c = pl.program_id(1)
    half = PEER_KEYS // 2
    per_half = half // nsub

    @pl.when(c == 0)
    def _():
        acc_ref[...] = jnp.zeros_like(acc_ref)

    @pl.when(c % per_half == 0)
    def _():
        i0 = (c // per_half) * half
        sub_i = (lax.broadcasted_iota(I32, (half, LANES), 0) + i0).astype(F32)
        sub_j = lax.broadcasted_iota(I32, (PEER_KEYS, LANES), 0).astype(F32)

        def group(tg, carry):
            r0 = pl.multiple_of(tg * GATE_GROUP, GATE_GROUP)
            ii = i_ref[pl.ds(r0, GATE_GROUP), :]
            jj = j_ref[pl.ds(r0, GATE_GROUP), :]
            gg = g_ref[pl.ds(r0, GATE_GROUP), :]
            for r in range(GATE_GROUP):
                a = (sub_i == ii[r:r + 1, :]).astype(BF16)
                rg = jnp.where(sub_j == jj[r:r + 1, :], gg[r:r + 1, :], 0.0).astype(BF16)
                w3_ref[pl.ds(pl.multiple_of((r0 + r) * W3_PITCH, 8), half), :] = _dot_nt(a, rg)
            return carry
        lax.fori_loop(0, tm // GATE_GROUP, group, 0)

    act = _dot_nt(h_ref[...], u_ref[...])
    for s in range(nsub):
        row_i = (c % per_half) * nsub + s
        w = w3_ref[pl.ds(row_i, tm, stride=W3_PITCH), :]
        sl = slice(s * LANES, (s + 1) * LANES)
        y_ref[:, sl] = (w * _gelu_exact(act[:, sl])).astype(BF16)
    acc_ref[...] += _dot(y_ref[...], v_ref[...])

    @pl.when(c == pl.num_programs(1) - 1)
    def _():
        x2 = x1_ref[...] + g2_ref[...] * acc_ref[...]
        if final_norm:
            x2 = x2 * lax.rsqrt(jnp.mean(x2 * x2, axis=-1, keepdims=True) + NORM_EPS) * fg_ref[...]
        o_ref[...] = x2


def _experts(h2, ii, jj, gg, u, v, x1, g2, fg, seq, final_norm):
    t, d = h2.shape
    tm = 512
    nsub = 8
    ne = nsub * PEER_KEYS
    nb = seq // tm
    row = lambda i, c: (i, 0)
    chunk = lambda i, c: (c, 0)
    return pl.pallas_call(
        functools.partial(_expert_kernel, tm=tm, nsub=nsub, final_norm=final_norm),
        grid=(t // tm, PEER_KEYS // nsub),
        in_specs=[pl.BlockSpec((tm, d), row),
                  pl.BlockSpec((tm, LANES), row), pl.BlockSpec((tm, LANES), row), pl.BlockSpec((tm, LANES), row),
                  pl.BlockSpec((ne, d), chunk), pl.BlockSpec((ne, d), chunk),
                  pl.BlockSpec((tm, d), row),
                  pl.BlockSpec((None, 1, d), lambda i, c: (i // nb, 0, 0)),
                  pl.BlockSpec((1, d), lambda i, c: (0, 0))],
        out_specs=pl.BlockSpec((tm, d), row),
        out_shape=jax.ShapeDtypeStruct((t, d), F32),
        scratch_shapes=[pltpu.VMEM((tm * W3_PITCH, LANES), F32),
                        pltpu.VMEM((tm, ne), BF16),
                        pltpu.VMEM((tm, d), F32)],
        compiler_params=_cparams("arbitrary", "arbitrary"),
        name="experts",
    )(h2, ii, jj, gg, u, v, x1, g2, fg)


def _dup(w):
    return jnp.concatenate([w, w], axis=1)


def _win_layout(w_in):
    nq, nkv = NSA_HEADS * HEAD_DIM, NSA_KV_HEADS * HEAD_DIM
    sizes = (nq, nkv, nkv, nkv, nkv, nkv, nkv, 3 * NSA_HEADS, DSA_HEADS * HEAD_DIM, DSA_KV_RANK,
             IDX_HEADS * IDX_DIM, IDX_DIM, IDX_HEADS, 2 * D_MODEL)
    offs = [0]
    for s in sizes:
        offs.append(offs[-1] + s)
    q_n, kc, vc, ks, vs, kw, vw, g_n, q_d, ckv, qi, ki, wi, g_m = [w_in[:, offs[k]:offs[k + 1]] for k in range(14)]
    dup2 = lambda w: jnp.concatenate([_dup(w[:, :HEAD_DIM]), _dup(w[:, HEAD_DIM:])], axis=1)
    misc = jnp.concatenate([g_n, wi, jnp.zeros((w_in.shape[0], LANES - 32), w_in.dtype)], axis=1)
    cols = [q_n, dup2(ks), dup2(kw), q_d, qi, _dup(ki), kc, dup2(vs), dup2(vw), vc, ckv, misc, g_m]
    return jnp.concatenate(cols, axis=1).astype(BF16)


def _rope_tables(seq):
    half = HEAD_DIM // 2
    pos = jnp.arange(seq, dtype=F32)
    inv = ROPE_THETA ** (-jnp.arange(half, dtype=F32) / half)
    ang = pos[:, None] * inv[None, :]
    cos, sin, zero = jnp.cos(ang), jnp.sin(ang), jnp.zeros((seq, half), F32)
    c = jnp.concatenate([cos, cos, cos, cos], axis=1)
    s1 = jnp.concatenate([-sin, zero, -sin, zero], axis=1)
    s2 = jnp.concatenate([zero, sin, zero, sin], axis=1)
    return c, s1, s2


def _cmp_weights(pos, w1, w2):
    g = NSA_KV_HEADS
    eye = jnp.eye(g, dtype=F32)
    big = jnp.einsum("lde,gh->lgdhe", w1, eye).reshape(CMP_BLOCK, g * HEAD_DIM, g * HEAD_DIM)
    wa = big[:CMP_STRIDE].reshape(CMP_STRIDE * g * HEAD_DIM, g * HEAD_DIM)
    wb = big[CMP_STRIDE:].reshape(CMP_STRIDE * g * HEAD_DIM, g * HEAD_DIM)
    z = jnp.zeros_like(w2)
    w2d = jnp.concatenate([jnp.concatenate([w2, w2, z, z], axis=1),
                           jnp.concatenate([z, z, w2, w2], axis=1)], axis=0)
    pos_t = jnp.broadcast_to(pos[:, None, :], (CMP_BLOCK, g, HEAD_DIM))
    pa = pos_t[:CMP_STRIDE].reshape(1, -1)
    pb = pos_t[CMP_STRIDE:].reshape(1, -1)
    return wa.astype(BF16), wb.astype(BF16), w2d.astype(BF16), pa, pb


def _selection_constants(seq):
    ncmp = seq // CMP_STRIDE
    nslc = seq // SLC_BLOCK
    per = SLC_BLOCK // CMP_STRIDE
    n = jnp.arange(ncmp)[:, None]
    j = jnp.arange(LANES)[None, :]
    mimp = ((n // per == j).astype(F32) + ((n + 1) // per == j).astype(F32)) * (j < nslc)
    mimp = mimp * (n < ncmp - 1)
    key = jnp.arange(seq)[None, :]
    eexp = (key // SLC_BLOCK == jnp.arange(LANES)[:, None]).astype(F32)
    return mimp.T.astype(BF16), eexp.astype(BF16)


def _subkey_blocks(subkeys):
    z = jnp.zeros_like(subkeys[:, 0])
    top = jnp.concatenate([subkeys[:, 0], z], axis=2)
    bot = jnp.concatenate([z, subkeys[:, 1]], axis=2)
    return jnp.concatenate([top, bot], axis=1).astype(BF16)


def kernel(x, c, ada_w, ada_b, norm1_g, w_in, cmp_pos, cmp_w1_k, cmp_w2_k, cmp_w1_v, cmp_w2_v,
           kv_norm_g, w_uk, w_uv, w_br_nsa, w_br_dsa, w_out, norm2_g,
           peer_wq, peer_subkeys, peer_u, peer_v, final_g):
    bsz, seq, d = x.shape
    depth = ada_w.shape[0]
    t = bsz * seq
    cos, s1, s2 = _rope_tables(seq)
    mimp, eexp = _selection_constants(seq)
    xt = x.reshape(t, d)
    out = xt
    for l in range(depth):
        mod = _ada(c, ada_w[l], ada_b[l]).reshape(bsz, 1, 6 * d)
        sh1, sc1, g1, sh2, sc2, g2 = [mod[:, :, k * d:(k + 1) * d] for k in range(6)]
        a, kc, vv, vc, mm, gm = _proj(xt, sc1, sh1, norm1_g[l].reshape(1, d), _win_layout(w_in[l]),
                                      cos, s1, s2, seq)
        a3 = a.reshape(bsz, seq, NA)
        m3 = mm.reshape(bsz, seq, NM)
        ncmp = seq // CMP_STRIDE
        wak, wbk, w2k, pa, pb = _cmp_weights(cmp_pos[l], cmp_w1_k[l], cmp_w2_k[l])
        wav, wbv, w2v, _, _ = _cmp_weights(cmp_pos[l], cmp_w1_v[l], cmp_w2_v[l])
        o_nsa = _nsa(a3, vv.reshape(bsz, seq, NV), kc.reshape(bsz, ncmp, CMP_STRIDE * 128),
                     vc.reshape(bsz, ncmp, CMP_STRIDE * 128), m3,
                     (wak, wbk, w2k, wav, wbv, w2v, pa, pb, mimp, eexp), seq)
        o_dsa = _dsa(a3, m3, kv_norm_g[l].reshape(1, DSA_KV_RANK), _dup(w_uk[l]).astype(BF16),
                     _dup(w_uv[l]).astype(BF16), cos, s1, s2, seq)
        x1, h2 = _merge(o_nsa.reshape(t, 512), o_dsa.reshape(t, 512), gm, xt, g1, sc2, sh2,
                        norm2_g[l].reshape(1, d), w_br_nsa[l].astype(BF16), w_br_dsa[l].astype(BF16),
                        w_out[l].astype(BF16), seq)
        ii, jj, gg = _route(h2, peer_wq[l].astype(BF16), _subkey_blocks(peer_subkeys[l]))
        out = _experts(h2, ii, jj, gg, peer_u[l].astype(BF16), peer_v[l].astype(BF16), x1, g2,
                       final_g.reshape(1, d), seq, final_norm=(l == depth - 1))
        xt = out
    return out.reshape(bsz, seq, d)
```

```python
import functools

import jax
import jax.numpy as jnp
from jax import lax
from jax.experimental import pallas as pl
from jax.experimental.pallas import tpu as pltpu

F32 = jnp.float32
BF16 = jnp.bfloat16
I32 = jnp.int32

D_MODEL = 1024
HEAD_DIM = 64
ROPE_THETA = 10000.0
NORM_EPS = 1e-6
NEG_INF = -1e30

NSA_HEADS = 8
NSA_KV_HEADS = 2
CMP_BLOCK = 32
CMP_STRIDE = 16
SLC_BLOCK = 64
SLC_TOPN = 16
WINDOW = 512
NSA_KEY_VARIANTS = 8

DSA_HEADS = 8
DSA_KV_RANK = 128
IDX_HEADS = 8
IDX_DIM = 64
DSA_TOPK = 256
DSA_KEY_VARIANTS = 8

PEER_HEADS = 8
PEER_KEYS = 128
PEER_HALF = 64
PEER_TOPK = 16

LANES = 128
W3_PITCH = PEER_KEYS // 2 + 8
GATE_GROUP = 32
VMEM_LIMIT = 56 * 1024 * 1024

NA = 2176
NV = 512
NM = 256
NG = 2 * D_MODEL
OFF_KC = NA
OFF_V = OFF_KC + 128
OFF_VC = OFF_V + NV
OFF_M = OFF_VC + 128
OFF_G = OFF_M + NM
NZ = OFF_G + NG
MISC_GATE = 0
MISC_WI = 24


def _dot(a, b):
    return jnp.dot(a, b, preferred_element_type=F32)


def _dot_nt(a, b):
    return lax.dot_general(a, b, (((1,), (1,)), ((), ())), preferred_element_type=F32)


def _cparams(*sem):
    return pltpu.CompilerParams(dimension_semantics=sem, vmem_limit_bytes=VMEM_LIMIT)


def _rope(z, c, s1, s2):
    return z * c + pltpu.roll(z, 96, 1) * s1 + pltpu.roll(z, 32, 1) * s2


def _col_reduce(x, op, reduce):
    n = x.shape[0]
    while n > 8 and (n // 8) % 2 == 0:
        n //= 2
        x = op(x[:n], x[n:])
    return reduce(x, axis=0, keepdims=True)


def _colsum(x):
    return _col_reduce(x, jnp.add, jnp.sum)


def _colmax(x):
    return _col_reduce(x, jnp.maximum, jnp.max)


def _half_mask(x, odd):
    lane = lax.broadcasted_iota(I32, x.shape, 1)
    keep = (lane >= HEAD_DIM) if odd else (lane < HEAD_DIM)
    return jnp.where(keep, x, jnp.zeros_like(x))


def _ada_kernel(c_ref, w_ref, b_ref, o_ref):
    c = c_ref[...]
    sc = c * jax.nn.sigmoid(c)
    o_ref[...] = _dot(sc.astype(BF16), w_ref[...].astype(BF16)) + b_ref[...]


def _ada(c, w, b):
    bsz, d = c.shape
    n = w.shape[1]
    tn = 1024
    return pl.pallas_call(
        _ada_kernel,
        grid=(n // tn,),
        in_specs=[pl.BlockSpec((bsz, d), lambda j: (0, 0)),
                  pl.BlockSpec((d, tn), lambda j: (0, j)),
                  pl.BlockSpec((1, tn), lambda j: (0, j))],
        out_specs=pl.BlockSpec((bsz, tn), lambda j: (0, j)),
        out_shape=jax.ShapeDtypeStruct((bsz, n), F32),
        compiler_params=_cparams("arbitrary"),
        name="ada",
    )(c, w, b.reshape(1, n))


def _proj_kernel(x_ref, sc_ref, sh_ref, g_ref, w_ref, cos_ref, s1_ref, s2_ref,
                 a_ref, kc_ref, v_ref, vc_ref, m_ref, gm_ref):
    x = x_ref[...]
    y = x * lax.rsqrt(jnp.mean(x * x, axis=-1, keepdims=True) + NORM_EPS) * g_ref[...]
    h = y * (1.0 + sc_ref[...]) + sh_ref[...]
    z = _dot(h.astype(BF16), w_ref[...])
    c, s1, s2 = cos_ref[...], s1_ref[...], s2_ref[...]
    for k in range(NA // LANES):
        sl = slice(k * LANES, (k + 1) * LANES)
        a_ref[:, sl] = _rope(z[:, sl], c, s1, s2).astype(BF16)
    kc_ref[...] = _rope(z[:, OFF_KC:OFF_KC + 128], c, s1, s2)
    v_ref[...] = z[:, OFF_V:OFF_V + NV].astype(BF16)
    vc_ref[...] = z[:, OFF_VC:OFF_VC + 128]
    m_ref[...] = z[:, OFF_M:OFF_M + NM]
    gm_ref[...] = z[:, OFF_G:OFF_G + NG]


def _proj(x2, sc, sh, g, w, cos, s1, s2, seq):
    t, d = x2.shape
    tm = 256
    nb = seq // tm
    row = lambda i: (i, 0)
    bat = lambda i: (i // nb, 0, 0)
    pos = lambda i: (i % nb, 0)
    full = lambda i: (0, 0)
    widths = (NA, 128, NV, 128, NM, NG)
    dtypes = (BF16, F32, BF16, F32, F32, F32)
    return pl.pallas_call(
        _proj_kernel,
        grid=(t // tm,),
        in_specs=[pl.BlockSpec((tm, d), row),
                  pl.BlockSpec((None, 1, d), bat),
                  pl.BlockSpec((None, 1, d), bat),
                  pl.BlockSpec((1, d), full),
                  pl.BlockSpec((d, NZ), full),
                  pl.BlockSpec((tm, LANES), pos),
                  pl.BlockSpec((tm, LANES), pos),
                  pl.BlockSpec((tm, LANES), pos)],
        out_specs=[pl.BlockSpec((tm, n), row) for n in widths],
        out_shape=[jax.ShapeDtypeStruct((t, n), dt) for n, dt in zip(widths, dtypes)],
        compiler_params=_cparams("arbitrary"),
        name="proj",
    )(x2, sc, sh, g, w, cos, s1, s2)


def _masked_attend(s, mask, v):
    s = jnp.where(mask, s, NEG_INF)
    m = jnp.max(s, axis=-1, keepdims=True)
    p = jnp.where(mask, jnp.exp(s - m), 0.0)
    den = jnp.maximum(jnp.sum(p, axis=-1, keepdims=True), 1e-30)
    return _dot(p.astype(BF16), v) / den, p, den


def _nsa_kernel(q_ref, ks_ref, kw_ref, vs_ref, vw_ref, kc_ref, vc_ref, misc_ref,
                wak_ref, wbk_ref, w2k_ref, wav_ref, wbv_ref, w2v_ref, posa_ref, posb_ref,
                mimp_ref, eexp_ref, o_ref, kcmp_ref, vcmp_ref, oslc_ref, *, tq, seq):
    i = pl.program_id(1)
    ncmp = seq // CMP_STRIDE
    scale = HEAD_DIM ** -0.5

    @pl.when(i == 0)
    def _():
        def compress(c_ref, wa_ref, wb_ref, w2_ref):
            c = c_ref[...]
            ha = _dot((c + posa_ref[...]).astype(BF16), wa_ref[...])
            hb = _dot((c + posb_ref[...]).astype(BF16), wb_ref[...])
            hid = ha + pltpu.roll(hb, ncmp - 1, 0)
            return _dot(jax.nn.gelu(hid).astype(BF16), w2_ref[...])
        kcmp_ref[...] = compress(kc_ref, wak_ref, wbk_ref, w2k_ref).astype(BF16)
        vcmp_ref[...] = compress(vc_ref, wav_ref, wbv_ref, w2v_ref).astype(BF16)

    t0 = i * tq
    trow = t0 + lax.broadcasted_iota(I32, (tq, LANES), 0)
    lane = lax.broadcasted_iota(I32, (tq, LANES), 1)
    sig = jax.nn.sigmoid(misc_ref[...])

    hpg = NSA_HEADS // NSA_KV_HEADS
    qs = [_half_mask(q_ref[:, (h // 2) * LANES:(h // 2 + 1) * LANES], h % 2) * scale for h in range(NSA_HEADS)]

    cmp_mask = (lane * CMP_STRIDE + (CMP_BLOCK - 1)) <= trow
    o_cmp, imp = [], []
    for h in range(NSA_HEADS):
        g = h // hpg
        s = _dot_nt(qs[h], kcmp_ref[:, g * LANES:(g + 1) * LANES])
        o, p, den = _masked_attend(s, cmp_mask, vcmp_ref[:, g * LANES:(g + 1) * LANES])
        o_cmp.append(o)
        pn = p / den
        if h % hpg == 0:
            imp.append(pn)
        else:
            imp[g] = imp[g] + pn

    chosen = []
    for g in range(NSA_KV_HEADS):
        pg = imp[g]
        p_hi = pg.astype(BF16)
        r1 = pg - p_hi.astype(F32)
        p_mid = r1.astype(BF16)
        p_lo = (r1 - p_mid.astype(F32)).astype(BF16)
        blk_t = (_dot_nt(mimp_ref[...], p_hi) + _dot_nt(mimp_ref[...], p_mid) + _dot_nt(mimp_ref[...], p_lo))
        nslc = seq // SLC_BLOCK
        jrow = lax.broadcasted_iota(I32, (nslc, tq), 0)
        tcol = t0 + lax.broadcasted_iota(I32, (nslc, tq), 1)
        cur = tcol // SLC_BLOCK
        forced = (jrow == 0) | (jrow == cur) | (jrow == cur - 1)
        admissible = jrow * SLC_BLOCK <= tcol
        val = jnp.where(forced, jnp.inf, jnp.where(admissible, blk_t[0:nslc, :], -jnp.inf))
        rank = jnp.zeros((nslc, tq), I32)
        for j in range(nslc):
            vj = val[j:j + 1, :]
            ahead = (vj > val) | ((vj == val) & (jrow > j))
            rank = rank + ahead.astype(I32)
        chosen_t = jnp.concatenate([(rank < SLC_TOPN).astype(F32), jnp.zeros((LANES - nslc, tq), F32)], axis=0)
        chosen.append(chosen_t.T.astype(BF16))

    def biased_attend(q, k, v, bias):
        s = _dot_nt(q, k) + bias
        p = jnp.exp(s - jnp.max(s, axis=-1, keepdims=True))
        den = jnp.maximum(jnp.sum(p, axis=-1, keepdims=True), 1e-30)
        return _dot(p.astype(BF16), v) / den

    def selected_branch(nk):
        causal = lax.broadcasted_iota(I32, (tq, nk), 1) <= t0 + lax.broadcasted_iota(I32, (tq, nk), 0)
        for g in range(NSA_KV_HEADS):
            picked = _dot(chosen[g], eexp_ref[:, 0:nk]) > 0.5
            bias = jnp.where(picked & causal, 0.0, NEG_INF)
            ksd = ks_ref[0:nk, g * LANES:(g + 1) * LANES]
            vsd = vs_ref[0:nk, g * LANES:(g + 1) * LANES]
            for h in range(g * hpg, (g + 1) * hpg):
                oslc_ref[h] = biased_attend(qs[h], ksd, vsd, bias)

    step = seq // NSA_KEY_VARIANTS
    per = step // tq
    for v in range(NSA_KEY_VARIANTS):
        pl.when(i // per == v)(functools.partial(selected_branch, (v + 1) * step))

    kspan = WINDOW + tq
    kstart = pl.multiple_of(jnp.maximum(t0 - WINDOW, 0), tq)
    diff_w = (t0 + lax.broadcasted_iota(I32, (tq, kspan), 0)) - (kstart + lax.broadcasted_iota(I32, (tq, kspan), 1))
    win_bias = jnp.where((diff_w >= 0) & (diff_w < WINDOW), 0.0, NEG_INF)
    pair = None
    for h in range(NSA_HEADS):
        g = h // hpg
        kwd = kw_ref[pl.ds(kstart, kspan), g * LANES:(g + 1) * LANES]
        vwd = vw_ref[pl.ds(kstart, kspan), g * LANES:(g + 1) * LANES]
        o_w = biased_attend(qs[h], kwd, vwd, win_bias)
        c0 = MISC_GATE + 3 * h
        o_h = sig[:, c0:c0 + 1] * o_cmp[h] + sig[:, c0 + 1:c0 + 2] * oslc_ref[h] + sig[:, c0 + 2:c0 + 3] * o_w
        if h % 2 == 0:
            pair = o_h
        else:
            o_ref[:, (h // 2) * LANES:(h // 2 + 1) * LANES] = jnp.where(lane < HEAD_DIM, pair, o_h).astype(BF16)


def _nsa(a3, v3, kcr, vcr, m3, wts, seq):
    bsz = a3.shape[0]
    tq = 256
    ncmp = seq // CMP_STRIDE
    per_b = lambda blk: (lambda b, i: (b, 0, blk))
    full2 = lambda b, i: (0, 0)
    w_specs = [pl.BlockSpec(w.shape, full2) for w in wts]
    return pl.pallas_call(
        functools.partial(_nsa_kernel, tq=tq, seq=seq),
        grid=(bsz, seq // tq),
        in_specs=[pl.BlockSpec((None, tq, 512), lambda b, i: (b, i, 0)),
                  pl.BlockSpec((None, seq, 256), per_b(2)),
                  pl.BlockSpec((None, seq, 256), per_b(3)),
                  pl.BlockSpec((None, seq, 256), per_b(0)),
                  pl.BlockSpec((None, seq, 256), per_b(1)),
                  pl.BlockSpec((None, ncmp, CMP_STRIDE * 128), per_b(0)),
                  pl.BlockSpec((None, ncmp, CMP_STRIDE * 128), per_b(0)),
                  pl.BlockSpec((None, tq, LANES), lambda b, i: (b, i, 1)),
                  ] + w_specs,
        out_specs=pl.BlockSpec((None, tq, 512), lambda b, i: (b, i, 0)),
        out_shape=jax.ShapeDtypeStruct((bsz, seq, 512), BF16),
        scratch_shapes=[pltpu.VMEM((ncmp, 256), BF16), pltpu.VMEM((ncmp, 256), BF16),
                        pltpu.VMEM((NSA_HEADS, tq, LANES), F32)],
        compiler_params=_cparams("arbitrary", "arbitrary"),
        name="nsa",
    )(a3, a3, a3, v3, v3, kcr, vcr, m3, *wts)


def _dsa_kernel(qd_ref, qi_ref, ki_ref, ckv_ref, misc_ref, gkv_ref, wuk_ref, wuv_ref,
                cos_ref, s1_ref, s2_ref, o_ref, kd_ref, vt_ref, *, tq, seq, topk):
    i = pl.program_id(1)

    @pl.when(i == 0)
    def _():
        ck = ckv_ref[...]
        cn = ck * lax.rsqrt(jnp.mean(ck * ck, axis=-1, keepdims=True) + NORM_EPS) * gkv_ref[...]
        cb = cn.astype(BF16)
        kd_ref[...] = _rope(_dot(cb, wuk_ref[...]), cos_ref[...], s1_ref[...], s2_ref[...]).astype(BF16)
        vt_ref[...] = _dot(cb, wuv_ref[...]).T.astype(BF16)

    t0 = i * tq
    misc_t = misc_ref[...].T
    int_min = jnp.int32(-2 ** 31)
    idx_bits = max(1, (seq - 1).bit_length()) + 1

    def attend(nk):
        kpos = lax.broadcasted_iota(I32, (nk, tq), 0)
        tpos = t0 + lax.broadcasted_iota(I32, (nk, tq), 1)
        causal = kpos <= tpos

        score = jnp.zeros((nk, tq), F32)
        for h in range(IDX_HEADS):
            qm = _half_mask(qi_ref[:, (h // 2) * LANES:(h // 2 + 1) * LANES], h % 2) * (IDX_DIM ** -0.5)
            logit = _dot_nt(ki_ref[0:nk, :], qm)
            w_h = misc_t[MISC_WI + h:MISC_WI + h + 1, :] * (IDX_HEADS ** -0.5)
            score = score + w_h * jnp.maximum(logit, 0.0)
        score = jnp.where(causal, score, -jnp.inf)

        def as_float(cu):
            ks = cu ^ int_min
            return lax.bitcast_convert_type(ks ^ ((ks >> 31) & jnp.int32(0x7FFFFFFF)), F32)

        def bit_step(it, tu):
            cu = tu | lax.shift_left(jnp.int32(1), 31 - it)
            cnt = _colsum((score >= as_float(cu)).astype(F32))
            below_neg_inf = (cu >= 0) & (cu < jnp.int32(0x007FFFFF))
            return jnp.where((cnt >= topk) | below_neg_inf, cu, tu)
        thr = as_float(lax.fori_loop(0, 32, bit_step, jnp.zeros((1, tq), I32)))
        above = score > thr
        need = topk - _colsum(above.astype(F32))
        tie_pos = jnp.where(score == thr, kpos, jnp.int32(2 * seq))

        def tie_step(it, lim):
            cl = lim | lax.shift_left(jnp.int32(1), idx_bits - 1 - it)
            cnt = _colsum((tie_pos < cl).astype(F32))
            return jnp.where(cnt <= need, cl, lim)
        lim = lax.fori_loop(0, idx_bits, tie_step, jnp.zeros((1, tq), I32))
        sel = causal & (above | (tie_pos < lim))
        bias = jnp.where(sel, 0.0, NEG_INF)

        row = lax.broadcasted_iota(I32, (LANES, tq), 0)
        pair = None
        for h in range(DSA_HEADS):
            qm = _half_mask(qd_ref[:, (h // 2) * LANES:(h // 2 + 1) * LANES], h % 2) * (HEAD_DIM ** -0.5)
            s = _dot_nt(kd_ref[0:nk, :], qm) + bias
            p = jnp.exp(s - _colmax(s))
            den = jnp.maximum(_colsum(p), 1e-30)
            o_t = _dot(vt_ref[:, 0:nk], p.astype(BF16)) / den
            if h % 2 == 0:
                pair = o_t
            else:
                o_ref[:, (h // 2) * LANES:(h // 2 + 1) * LANES] = (
                    jnp.where(row < HEAD_DIM, pair, o_t).T.astype(BF16))

    step = seq // DSA_KEY_VARIANTS
    per = step // tq
    for v in range(DSA_KEY_VARIANTS):
        pl.when(i // per == v)(functools.partial(attend, (v + 1) * step))


def _dsa(a3, m3, gkv, wuk, wuv, cos, s1, s2, seq):
    bsz = a3.shape[0]
    tq = 128
    topk = min(DSA_TOPK, seq // 4)
    full2 = lambda b, i: (0, 0)
    return pl.pallas_call(
        functools.partial(_dsa_kernel, tq=tq, seq=seq, topk=topk),
        grid=(bsz, seq // tq),
        in_specs=[pl.BlockSpec((None, tq, 512), lambda b, i: (b, i, 2)),
                  pl.BlockSpec((None, tq, 512), lambda b, i: (b, i, 3)),
                  pl.BlockSpec((None, seq, LANES), lambda b, i: (b, 0, 16)),
                  pl.BlockSpec((None, seq, LANES), lambda b, i: (b, 0, 0)),
                  pl.BlockSpec((None, tq, LANES), lambda b, i: (b, i, 1)),
                  pl.BlockSpec((1, LANES), full2),
                  pl.BlockSpec((LANES, LANES), full2),
                  pl.BlockSpec((LANES, LANES), full2),
                  pl.BlockSpec((seq, LANES), full2),
                  pl.BlockSpec((seq, LANES), full2),
                  pl.BlockSpec((seq, LANES), full2)],
        out_specs=pl.BlockSpec((None, tq, 512), lambda b, i: (b, i, 0)),
        out_shape=jax.ShapeDtypeStruct((bsz, seq, 512), BF16),
        scratch_shapes=[pltpu.VMEM((seq, LANES), BF16), pltpu.VMEM((LANES, seq), BF16)],
        compiler_params=_cparams("arbitrary", "arbitrary"),
        name="dsa",
    )(a3, a3, a3, m3, m3, gkv, wuk, wuv, cos, s1, s2)


def _merge_kernel(on_ref, od_ref, gm_ref, x_ref, g1_ref, sc_ref, sh_ref, n2_ref,
                  wn_ref, wd_ref, wo_ref, x1_ref, h2_ref):
    gm = gm_ref[...]
    merged = (jax.nn.sigmoid(gm[:, :D_MODEL]) * _dot(on_ref[...], wn_ref[...])
              + jax.nn.sigmoid(gm[:, D_MODEL:]) * _dot(od_ref[...], wd_ref[...]))
    x1 = x_ref[...] + g1_ref[...] * _dot(merged.astype(BF16), wo_ref[...])
    x1_ref[...] = x1
    y = x1 * lax.rsqrt(jnp.mean(x1 * x1, axis=-1, keepdims=True) + NORM_EPS) * n2_ref[...]
    h2_ref[...] = (y * (1.0 + sc_ref[...]) + sh_ref[...]).astype(BF16)


def _merge(o_nsa, o_dsa, gm, x2, g1, sc2, sh2, n2, wn, wd, wo, seq):
    t, d = x2.shape
    tm = 256
    nb = seq // tm
    row = lambda i: (i, 0)
    bat = lambda i: (i // nb, 0, 0)
    full = lambda i: (0, 0)
    return pl.pallas_call(
        _merge_kernel,
        grid=(t // tm,),
        in_specs=[pl.BlockSpec((tm, 512), row), pl.BlockSpec((tm, 512), row),
                  pl.BlockSpec((tm, NG), row), pl.BlockSpec((tm, d), row),
                  pl.BlockSpec((None, 1, d), bat), pl.BlockSpec((None, 1, d), bat),
                  pl.BlockSpec((None, 1, d), bat), pl.BlockSpec((1, d), full),
                  pl.BlockSpec(wn.shape, full), pl.BlockSpec(wd.shape, full), pl.BlockSpec(wo.shape, full)],
        out_specs=[pl.BlockSpec((tm, d), row), pl.BlockSpec((tm, d), row)],
        out_shape=[jax.ShapeDtypeStruct((t, d), F32), jax.ShapeDtypeStruct((t, d), BF16)],
        compiler_params=_cparams("arbitrary"),
        name="merge",
    )(o_nsa, o_dsa, gm, x2, g1, sc2, sh2, n2, wn, wd, wo)


def _topk_rows(vals, k):
    n, t = vals.shape
    ridx = lax.broadcasted_iota(I32, (n, t), 0).astype(F32)
    slot = lax.broadcasted_iota(I32, (k, t), 0)
    top_v = jnp.zeros((k, t), F32)
    top_i = jnp.zeros((k, t), F32)
    for j in range(k):
        m = jnp.max(vals, axis=0, keepdims=True)
        idx = jnp.min(jnp.where(vals == m, ridx, float(n)), axis=0, keepdims=True)
        top_v = jnp.where(slot == j, m, top_v)
        top_i = jnp.where(slot == j, idx, top_i)
        vals = jnp.where(ridx == idx, -jnp.inf, vals)
    return top_v, top_i


def _route_kernel(h_ref, wq_ref, k2_ref, i_ref, j_ref, g_ref, q3_ref, is_ref, js_ref, gs_ref, cand_ref, *, tm):
    q = _dot(h_ref[...], wq_ref[...]).astype(BF16)
    for h in range(PEER_HEADS):
        q3_ref[h] = q[:, h * LANES:(h + 1) * LANES]

    def head(h, carry):
        s = _dot_nt(k2_ref[h], q3_ref[h])
        s0, i0 = _topk_rows(s[:PEER_KEYS], PEER_TOPK)
        s1, i1 = _topk_rows(s[PEER_KEYS:], PEER_TOPK)
        r0 = 0
        for a in range(PEER_TOPK):
            nb = PEER_TOPK // (a + 1)
            cand_ref[r0:r0 + nb, :] = s0[a:a + 1, :] + s1[0:nb, :]
            r0 += nb
        cand_ref[r0:, :] = jnp.full((cand_ref.shape[0] - r0, tm), -jnp.inf, F32)
        top_s, top_r = _topk_rows(cand_ref[...], PEER_TOPK)
        pa = jnp.zeros((PEER_TOPK, tm), F32)
        pb = jnp.zeros((PEER_TOPK, tm), F32)
        r = 0
        for a in range(PEER_TOPK):
            for b in range(PEER_TOPK // (a + 1)):
                hit = top_r == r
                pa = jnp.where(hit, float(a), pa)
                pb = jnp.where(hit, float(b), pb)
                r += 1
        ii = jnp.zeros((PEER_TOPK, tm), F32)
        jj = jnp.zeros((PEER_TOPK, tm), F32)
        for a in range(PEER_TOPK):
            ii = jnp.where(pa == a, i0[a:a + 1, :], ii)
            jj = jnp.where(pb == a, i1[a:a + 1, :], jj)
        e = jnp.exp(top_s - jnp.max(top_s, axis=0, keepdims=True))
        gate = e / jnp.sum(e, axis=0, keepdims=True)
        rows = pl.ds(pl.multiple_of(h * PEER_TOPK, PEER_TOPK), PEER_TOPK)
        is_ref[rows, :] = ii
        js_ref[rows, :] = jj
        gs_ref[rows, :] = gate
        return carry
    lax.fori_loop(0, PEER_HEADS, head, 0)
    i_ref[...] = is_ref[...].T
    j_ref[...] = js_ref[...].T
    g_ref[...] = gs_ref[...].T


def _route(h2, wq, k2):
    t, d = h2.shape
    tm = 512
    row = lambda i: (i, 0)
    nk = PEER_HEADS * PEER_TOPK
    ncand = -(-sum(PEER_TOPK // (a + 1) for a in range(PEER_TOPK)) // 8) * 8
    return pl.pallas_call(
        functools.partial(_route_kernel, tm=tm),
        grid=(t // tm,),
        in_specs=[pl.BlockSpec((tm, d), row),
                  pl.BlockSpec(wq.shape, lambda i: (0, 0)),
                  pl.BlockSpec(k2.shape, lambda i: (0, 0, 0))],
        out_specs=[pl.BlockSpec((tm, nk), row)] * 3,
        out_shape=[jax.ShapeDtypeStruct((t, nk), F32)] * 3,
        scratch_shapes=[pltpu.VMEM((PEER_HEADS, tm, LANES), BF16),
                        pltpu.VMEM((nk, tm), F32), pltpu.VMEM((nk, tm), F32), pltpu.VMEM((nk, tm), F32),
                        pltpu.VMEM((ncand, tm), F32)],
        compiler_params=_cparams("arbitrary"),
        name="route",
    )(h2, wq, k2)


def _gelu_exact(a):
    return 0.5 * a * (1.0 + lax.erf(a * (2.0 ** -0.5)))


def _expert_kernel(h_ref, i_ref, j_ref, g_ref, u_ref, v_ref, x1_ref, g2_ref, fg_ref, o_ref,
                   w3_ref, y_ref, acc_ref, *, tm, nsub, final_norm):
    c = pl.program_id(1)
    half = PEER_KEYS // 2
    per_half = half // nsub

    @pl.when(c == 0)
    def _():
        acc_ref[...] = jnp.zeros_like(acc_ref)

    @pl.when(c % per_half == 0)
    def _():
        i0 = (c // per_half) * half
        sub_i = (lax.broadcasted_iota(I32, (half, LANES), 0) + i0).astype(F32)
        sub_j = lax.broadcasted_iota(I32, (PEER_KEYS, LANES), 0).astype(F32)

        def group(tg, carry):
            r0 = pl.multiple_of(tg * GATE_GROUP, GATE_GROUP)
            ii = i_ref[pl.ds(r0, GATE_GROUP), :]
            jj = j_ref[pl.ds(r0, GATE_GROUP), :]
            gg = g_ref[pl.ds(r0, GATE_GROUP), :]
            for r in range(GATE_GROUP):
                a = (sub_i == ii[r:r + 1, :]).astype(BF16)
                rg = jnp.where(sub_j == jj[r:r + 1, :], gg[r:r + 1, :], 0.0).astype(BF16)
                w3_ref[pl.ds(pl.multiple_of((r0 + r) * W3_PITCH, 8), half), :] = _dot_nt(a, rg)
            return carry
        lax.fori_loop(0, tm // GATE_GROUP, group, 0)

    act = _dot_nt(h_ref[...], u_ref[...])
    for s in range(nsub):
        row_i = (c % per_half) * nsub + s
        w = w3_ref[pl.ds(row_i, tm, stride=W3_PITCH), :]
        sl = slice(s * LANES, (s + 1) * LANES)
        y_ref[:, sl] = (w * _gelu_exact(act[:, sl])).astype(BF16)
    acc_ref[...] += _dot(y_ref[...], v_ref[...])

    @pl.when(c == pl.num_programs(1) - 1)
    def _():
        x2 = x1_ref[...] + g2_ref[...] * acc_ref[...]
        if final_norm:
            x2 = x2 * lax.rsqrt(jnp.mean(x2 * x2, axis=-1, keepdims=True) + NORM_EPS) * fg_ref[...]
        o_ref[...] = x2


def _experts(h2, ii, jj, gg, u, v, x1, g2, fg, seq, final_norm):
    t, d = h2.shape
    tm = 512
    nsub = 8
    ne = nsub * PEER_KEYS
    nb = seq // tm
    row = lambda i, c: (i, 0)
    chunk = lambda i, c: (c, 0)
    return pl.pallas_call(
        functools.partial(_expert_kernel, tm=tm, nsub=nsub, final_norm=final_norm),
        grid=(t // tm, PEER_KEYS // nsub),
        in_specs=[pl.BlockSpec((tm, d), row),
                  pl.BlockSpec((tm, LANES), row), pl.BlockSpec((tm, LANES), row), pl.BlockSpec((tm, LANES), row),
                  pl.BlockSpec((ne, d), chunk), pl.BlockSpec((ne, d), chunk),
                  pl.BlockSpec((tm, d), row),
                  pl.BlockSpec((None, 1, d), lambda i, c: (i // nb, 0, 0)),
                  pl.BlockSpec((1, d), lambda i, c: (0, 0))],
        out_specs=pl.BlockSpec((tm, d), row),
        out_shape=jax.ShapeDtypeStruct((t, d), F32),
        scratch_shapes=[pltpu.VMEM((tm * W3_PITCH, LANES), F32),
                        pltpu.VMEM((tm, ne), BF16),
                        pltpu.VMEM((tm, d), F32)],
        compiler_params=_cparams("arbitrary", "arbitrary"),
        name="experts",
    )(h2, ii, jj, gg, u, v, x1, g2, fg)


def _dup(w):
    return jnp.concatenate([w, w], axis=1)


def _win_layout(w_in):
    nq, nkv = NSA_HEADS * HEAD_DIM, NSA_KV_HEADS * HEAD_DIM
    sizes = (nq, nkv, nkv, nkv, nkv, nkv, nkv, 3 * NSA_HEADS, DSA_HEADS * HEAD_DIM, DSA_KV_RANK,
             IDX_HEADS * IDX_DIM, IDX_DIM, IDX_HEADS, 2 * D_MODEL)
    offs = [0]
    for s in sizes:
        offs.append(offs[-1] + s)
    q_n, kc, vc, ks, vs, kw, vw, g_n, q_d, ckv, qi, ki, wi, g_m = [w_in[:, offs[k]:offs[k + 1]] for k in range(14)]
    dup2 = lambda w: jnp.concatenate([_dup(w[:, :HEAD_DIM]), _dup(w[:, HEAD_DIM:])], axis=1)
    misc = jnp.concatenate([g_n, wi, jnp.zeros((w_in.shape[0], LANES - 32), w_in.dtype)], axis=1)
    cols = [q_n, dup2(ks), dup2(kw), q_d, qi, _dup(ki), kc, dup2(vs), dup2(vw), vc, ckv, misc, g_m]
    return jnp.concatenate(cols, axis=1).astype(BF16)


def _rope_tables(seq):
    half = HEAD_DIM // 2
    pos = jnp.arange(seq, dtype=F32)
    inv = ROPE_THETA ** (-jnp.arange(half, dtype=F32) / half)
    ang = pos[:, None] * inv[None, :]
    cos, sin, zero = jnp.cos(ang), jnp.sin(ang), jnp.zeros((seq, half), F32)
    c = jnp.concatenate([cos, cos, cos, cos], axis=1)
    s1 = jnp.concatenate([-sin, zero, -sin, zero], axis=1)
    s2 = jnp.concatenate([zero, sin, zero, sin], axis=1)
    return c, s1, s2


def _cmp_weights(pos, w1, w2):
    g = NSA_KV_HEADS
    eye = jnp.eye(g, dtype=F32)
    big = jnp.einsum("lde,gh->lgdhe", w1, eye).reshape(CMP_BLOCK, g * HEAD_DIM, g * HEAD_DIM)
    wa = big[:CMP_STRIDE].reshape(CMP_STRIDE * g * HEAD_DIM, g * HEAD_DIM)
    wb = big[CMP_STRIDE:].reshape(CMP_STRIDE * g * HEAD_DIM, g * HEAD_DIM)
    z = jnp.zeros_like(w2)
    w2d = jnp.concatenate([jnp.concatenate([w2, w2, z, z], axis=1),
                           jnp.concatenate([z, z, w2, w2], axis=1)], axis=0)
    pos_t = jnp.broadcast_to(pos[:, None, :], (CMP_BLOCK, g, HEAD_DIM))
    pa = pos_t[:CMP_STRIDE].reshape(1, -1)
    pb = pos_t[CMP_STRIDE:].reshape(1, -1)
    return wa.astype(BF16), wb.astype(BF16), w2d.astype(BF16), pa, pb


def _selection_constants(seq):
    ncmp = seq // CMP_STRIDE
    nslc = seq // SLC_BLOCK
    per = SLC_BLOCK // CMP_STRIDE
    n = jnp.arange(ncmp)[:, None]
    j = jnp.arange(LANES)[None, :]
    mimp = ((n // per == j).astype(F32) + ((n + 1) // per == j).astype(F32)) * (j < nslc)
    mimp = mimp * (n < ncmp - 1)
    key = jnp.arange(seq)[None, :]
    eexp = (key // SLC_BLOCK == jnp.arange(LANES)[:, None]).astype(F32)
    return mimp.T.astype(BF16), eexp.astype(BF16)


def _subkey_blocks(subkeys):
    z = jnp.zeros_like(subkeys[:, 0])
    top = jnp.concatenate([subkeys[:, 0], z], axis=2)
    bot = jnp.concatenate([z, subkeys[:, 1]], axis=2)
    return jnp.concatenate([top, bot], axis=1).astype(BF16)


def kernel(x, c, ada_w, ada_b, norm1_g, w_in, cmp_pos, cmp_w1_k, cmp_w2_k, cmp_w1_v, cmp_w2_v,
           kv_norm_g, w_uk, w_uv, w_br_nsa, w_br_dsa, w_out, norm2_g,
           peer_wq, peer_subkeys, peer_u, peer_v, final_g):
    bsz, seq, d = x.shape
    depth = ada_w.shape[0]
    t = bsz * seq
    cos, s1, s2 = _rope_tables(seq)
    mimp, eexp = _selection_constants(seq)
    xt = x.reshape(t, d)
    out = xt
    for l in range(depth):
        mod = _ada(c, ada_w[l], ada_b[l]).reshape(bsz, 1, 6 * d)
        sh1, sc1, g1, sh2, sc2, g2 = [mod[:, :, k * d:(k + 1) * d] for k in range(6)]
        a, kc, vv, vc, mm, gm = _proj(xt, sc1, sh1, norm1_g[l].reshape(1, d), _win_layout(w_in[l]),
                                      cos, s1, s2, seq)
        a3 = a.reshape(bsz, seq, NA)
        m3 = mm.reshape(bsz, seq, NM)
        ncmp = seq // CMP_STRIDE
        wak, wbk, w2k, pa, pb = _cmp_weights(cmp_pos[l], cmp_w1_k[l], cmp_w2_k[l])
        wav, wbv, w2v, _, _ = _cmp_weights(cmp_pos[l], cmp_w1_v[l], cmp_w2_v[l])
        o_nsa = _nsa(a3, vv.reshape(bsz, seq, NV), kc.reshape(bsz, ncmp, CMP_STRIDE * 128),
                     vc.reshape(bsz, ncmp, CMP_STRIDE * 128), m3,
                     (wak, wbk, w2k, wav, wbv, w2v, pa, pb, mimp, eexp), seq)
        o_dsa = _dsa(a3, m3, kv_norm_g[l].reshape(1, DSA_KV_RANK), _dup(w_uk[l]).astype(BF16),
                     _dup(w_uv[l]).astype(BF16), cos, s1, s2, seq)
        x1, h2 = _merge(o_nsa.reshape(t, 512), o_dsa.reshape(t, 512), gm, xt, g1, sc2, sh2,
                        norm2_g[l].reshape(1, d), w_br_nsa[l].astype(BF16), w_br_dsa[l].astype(BF16),
                        w_out[l].astype(BF16), seq)
        ii, jj, gg = _route(h2, peer_wq[l].astype(BF16), _subkey_blocks(peer_subkeys[l]))
        out = _experts(h2, ii, jj, gg, peer_u[l].astype(BF16), peer_v[l].astype(BF16), x1, g2,
                       final_g.reshape(1, d), seq, final_norm=(l == depth - 1))
        xt = out
    return out.reshape(bsz, seq, d)
```

```python
import functools

import jax
import jax.numpy as jnp
from jax import lax
from jax.experimental import pallas as pl
from jax.experimental.pallas import tpu as pltpu

F32 = jnp.float32
BF16 = jnp.bfloat16
I32 = jnp.int32

D_MODEL = 1024
HEAD_DIM = 64
ROPE_THETA = 10000.0
NORM_EPS = 1e-6
NEG_INF = -1e30

NSA_HEADS = 8
NSA_KV_HEADS = 2
CMP_BLOCK = 32
CMP_STRIDE = 16
SLC_BLOCK = 64
SLC_TOPN = 16
WINDOW = 512
NSA_KEY_VARIANTS = 8

DSA_HEADS = 8
DSA_KV_RANK = 128
IDX_HEADS = 8
IDX_DIM = 64
DSA_TOPK = 256
DSA_KEY_VARIANTS = 8

PEER_HEADS = 8
PEER_KEYS = 128
PEER_HALF = 64
PEER_TOPK = 16

LANES = 128
W3_PITCH = PEER_KEYS // 2 + 8
GATE_GROUP = 32
VMEM_LIMIT = 56 * 1024 * 1024

NA = 2176
NV = 512
NM = 256
NG = 2 * D_MODEL
OFF_KC = NA
OFF_V = OFF_KC + 128
OFF_VC = OFF_V + NV
OFF_M = OFF_VC + 128
OFF_G = OFF_M + NM
NZ = OFF_G + NG
MISC_GATE = 0
MISC_WI = 24


def _dot(a, b):
    return jnp.dot(a, b, preferred_element_type=F32)


def _dot_nt(a, b):
    return lax.dot_general(a, b, (((1,), (1,)), ((), ())), preferred_element_type=F32)


def _cparams(*sem):
    return pltpu.CompilerParams(dimension_semantics=sem, vmem_limit_bytes=VMEM_LIMIT)


def _rope(z, c, s1, s2):
    return z * c + pltpu.roll(z, 96, 1) * s1 + pltpu.roll(z, 32, 1) * s2


def _col_reduce(x, op, reduce):
    n = x.shape[0]
    while n > 8 and (n // 8) % 2 == 0:
        n //= 2
        x = op(x[:n], x[n:])
    return reduce(x, axis=0, keepdims=True)


def _colsum(x):
    return _col_reduce(x, jnp.add, jnp.sum)


def _colmax(x):
    return _col_reduce(x, jnp.maximum, jnp.max)


def _half_mask(x, odd):
    lane = lax.broadcasted_iota(I32, x.shape, 1)
    keep = (lane >= HEAD_DIM) if odd else (lane < HEAD_DIM)
    return jnp.where(keep, x, jnp.zeros_like(x))


def _ada_kernel(c_ref, w_ref, b_ref, o_ref):
    c = c_ref[...]
    sc = c * jax.nn.sigmoid(c)
    o_ref[...] = _dot(sc.astype(BF16), w_ref[...].astype(BF16)) + b_ref[...]


def _ada(c, w, b):
    bsz, d = c.shape
    n = w.shape[1]
    tn = 1024
    return pl.pallas_call(
        _ada_kernel,
        grid=(n // tn,),
        in_specs=[pl.BlockSpec((bsz, d), lambda j: (0, 0)),
                  pl.BlockSpec((d, tn), lambda j: (0, j)),
                  pl.BlockSpec((1, tn), lambda j: (0, j))],
        out_specs=pl.BlockSpec((bsz, tn), lambda j: (0, j)),
        out_shape=jax.ShapeDtypeStruct((bsz, n), F32),
        compiler_params=_cparams("arbitrary"),
        name="ada",
    )(c, w, b.reshape(1, n))


def _proj_kernel(x_ref, sc_ref, sh_ref, g_ref, w_ref, cos_ref, s1_ref, s2_ref,
                 a_ref, kc_ref, v_ref, vc_ref, m_ref, gm_ref):
    x = x_ref[...]
    y = x * lax.rsqrt(jnp.mean(x * x, axis=-1, keepdims=True) + NORM_EPS) * g_ref[...]
    h = y * (1.0 + sc_ref[...]) + sh_ref[...]
    z = _dot(h.astype(BF16), w_ref[...])
    c, s1, s2 = cos_ref[...], s1_ref[...], s2_ref[...]
    for k in range(NA // LANES):
        sl = slice(k * LANES, (k + 1) * LANES)
        a_ref[:, sl] = _rope(z[:, sl], c, s1, s2).astype(BF16)
    kc_ref[...] = _rope(z[:, OFF_KC:OFF_KC + 128], c, s1, s2)
    v_ref[...] = z[:, OFF_V:OFF_V + NV].astype(BF16)
    vc_ref[...] = z[:, OFF_VC:OFF_VC + 128]
    m_ref[...] = z[:, OFF_M:OFF_M + NM]
    gm_ref[...] = z[:, OFF_G:OFF_G + NG]


def _proj(x2, sc, sh, g, w, cos, s1, s2, seq):
    t, d = x2.shape
    tm = 256
    nb = seq // tm
    row = lambda i: (i, 0)
    bat = lambda i: (i // nb, 0, 0)
    pos = lambda i: (i % nb, 0)
    full = lambda i: (0, 0)
    widths = (NA, 128, NV, 128, NM, NG)
    dtypes = (BF16, F32, BF16, F32, F32, F32)
    return pl.pallas_call(
        _proj_kernel,
        grid=(t // tm,),
        in_specs=[pl.BlockSpec((tm, d), row),
                  pl.BlockSpec((None, 1, d), bat),
                  pl.BlockSpec((None, 1, d), bat),
                  pl.BlockSpec((1, d), full),
                  pl.BlockSpec((d, NZ), full),
                  pl.BlockSpec((tm, LANES), pos),
                  pl.BlockSpec((tm, LANES), pos),
                  pl.BlockSpec((tm, LANES), pos)],
        out_specs=[pl.BlockSpec((tm, n), row) for n in widths],
        out_shape=[jax.ShapeDtypeStruct((t, n), dt) for n, dt in zip(widths, dtypes)],
        compiler_params=_cparams("arbitrary"),
        name="proj",
    )(x2, sc, sh, g, w, cos, s1, s2)


def _masked_attend(s, mask, v):
    s = jnp.where(mask, s, NEG_INF)
    m = jnp.max(s, axis=-1, keepdims=True)
    p = jnp.where(mask, jnp.exp(s - m), 0.0)
    den = jnp.maximum(jnp.sum(p, axis=-1, keepdims=True), 1e-30)
    return _dot(p.astype(BF16), v) / den, p, den


def _nsa_kernel(q_ref, ks_ref, kw_ref, vs_ref, vw_ref, kc_ref, vc_ref, misc_ref,
                wak_ref, wbk_ref, w2k_ref, wav_ref, wbv_ref, w2v_ref, posa_ref, posb_ref,
                mimp_ref, eexp_ref, o_ref, kcmp_ref, vcmp_ref, oslc_ref, *, tq, seq):
    i = pl.program_id(1)
    ncmp = seq // CMP_STRIDE
    scale = HEAD_DIM ** -0.5

    @pl.when(i == 0)
    def _():
        def compress(c_ref, wa_ref, wb_ref, w2_ref):
            c = c_ref[...]
            ha = _dot((c + posa_ref[...]).astype(BF16), wa_ref[...])
            hb = _dot((c + posb_ref[...]).astype(BF16), wb_ref[...])
            hid = ha + pltpu.roll(hb, ncmp - 1, 0)
            return _dot(jax.nn.gelu(hid).astype(BF16), w2_ref[...])
        kcmp_ref[...] = compress(kc_ref, wak_ref, wbk_ref, w2k_ref).astype(BF16)
        vcmp_ref[...] = compress(vc_ref, wav_ref, wbv_ref, w2v_ref).astype(BF16)

    t0 = i * tq
    trow = t0 + lax.broadcasted_iota(I32, (tq, LANES), 0)
    lane = lax.broadcasted_iota(I32, (tq, LANES), 1)
    sig = jax.nn.sigmoid(misc_ref[...])

    hpg = NSA_HEADS // NSA_KV_HEADS
    qs = [_half_mask(q_ref[:, (h // 2) * LANES:(h // 2 + 1) * LANES], h % 2) * scale for h in range(NSA_HEADS)]

    cmp_mask = (lane * CMP_STRIDE + (CMP_BLOCK - 1)) <= trow
    o_cmp, imp = [], []
    for h in range(NSA_HEADS):
        g = h // hpg
        s = _dot_nt(qs[h], kcmp_ref[:, g * LANES:(g + 1) * LANES])
        o, p, den = _masked_attend(s, cmp_mask, vcmp_ref[:, g * LANES:(g + 1) * LANES])
        o_cmp.append(o)
        pn = p / den
        if h % hpg == 0:
            imp.append(pn)
        else:
            imp[g] = imp[g] + pn

    chosen = []
    for g in range(NSA_KV_HEADS):
        pg = imp[g]
        p_hi = pg.astype(BF16)
        r1 = pg - p_hi.astype(F32)
        p_mid = r1.astype(BF16)
        p_lo = (r1 - p_mid.astype(F32)).astype(BF16)
        blk_t = (_dot_nt(mimp_ref[...], p_hi) + _dot_nt(mimp_ref[...], p_mid) + _dot_nt(mimp_ref[...], p_lo))
        nslc = seq // SLC_BLOCK
        jrow = lax.broadcasted_iota(I32, (nslc, tq), 0)
        tcol = t0 + lax.broadcasted_iota(I32, (nslc, tq), 1)
        cur = tcol // SLC_BLOCK
        forced = (jrow == 0) | (jrow == cur) | (jrow == cur - 1)
        admissible = jrow * SLC_BLOCK <= tcol
        val = jnp.where(forced, jnp.inf, jnp.where(admissible, blk_t[0:nslc, :], -jnp.inf))
        rank = jnp.zeros((nslc, tq), I32)
        for j in range(nslc):
            vj = val[j:j + 1, :]
            ahead = (vj > val) | ((vj == val) & (jrow > j))
            rank = rank + ahead.astype(I32)
        chosen_t = jnp.concatenate([(rank < SLC_TOPN).astype(F32), jnp.zeros((LANES - nslc, tq), F32)], axis=0)
        chosen.append(chosen_t.T.astype(BF16))

    def biased_attend(q, k, v, bias):
        s = _dot_nt(q, k) + bias
        p = jnp.exp(s - jnp.max(s, axis=-1, keepdims=True))
        den = jnp.maximum(jnp.sum(p, axis=-1, keepdims=True), 1e-30)
        return _dot(p.astype(BF16), v) / den

    def selected_branch(nk):
        causal = lax.broadcasted_iota(I32, (tq, nk), 1) <= t0 + lax.broadcasted_iota(I32, (tq, nk), 0)
        for g in range(NSA_KV_HEADS):
            picked = _dot(chosen[g], eexp_ref[:, 0:nk]) > 0.5
            bias = jnp.where(picked & causal, 0.0, NEG_INF)
            ksd = ks_ref[0:nk, g * LANES:(g + 1) * LANES]
            vsd = vs_ref[0:nk, g * LANES:(g + 1) * LANES]
            for h in range(g * hpg, (g + 1) * hpg):
                oslc_ref[h] = biased_attend(qs[h], ksd, vsd, bias)

    step = seq // NSA_KEY_VARIANTS
    per = step // tq
    for v in range(NSA_KEY_VARIANTS):
        pl.when(i // per == v)(functools.partial(selected_branch, (v + 1) * step))

    kspan = WINDOW + tq
    kstart = pl.multiple_of(jnp.maximum(t0 - WINDOW, 0), tq)
    diff_w = (t0 + lax.broadcasted_iota(I32, (tq, kspan), 0)) - (kstart + lax.broadcasted_iota(I32, (tq, kspan), 1))
    win_bias = jnp.where((diff_w >= 0) & (diff_w < WINDOW), 0.0, NEG_INF)
    pair = None
    for h in range(NSA_HEADS):
        g = h // hpg
        kwd = kw_ref[pl.ds(kstart, kspan), g * LANES:(g + 1) * LANES]
        vwd = vw_ref[pl.ds(kstart, kspan), g * LANES:(g + 1) * LANES]
        o_w = biased_attend(qs[h], kwd, vwd, win_bias)
        c0 = MISC_GATE + 3 * h
        o_h = sig[:, c0:c0 + 1] * o_cmp[h] + sig[:, c0 + 1:c0 + 2] * oslc_ref[h] + sig[:, c0 + 2:c0 + 3] * o_w
        if h % 2 == 0:
            pair = o_h
        else:
            o_ref[:, (h // 2) * LANES:(h // 2 + 1) * LANES] = jnp.where(lane < HEAD_DIM, pair, o_h).astype(BF16)


def _nsa(a3, v3, kcr, vcr, m3, wts, seq):
    bsz = a3.shape[0]
    tq = 128
    ncmp = seq // CMP_STRIDE
    per_b = lambda blk: (lambda b, i: (b, 0, blk))
    full2 = lambda b, i: (0, 0)
    w_specs = [pl.BlockSpec(w.shape, full2) for w in wts]
    return pl.pallas_call(
        functools.partial(_nsa_kernel, tq=tq, seq=seq),
        grid=(bsz, seq // tq),
        in_specs=[pl.BlockSpec((None, tq, 512), lambda b, i: (b, i, 0)),
                  pl.BlockSpec((None, seq, 256), per_b(2)),
                  pl.BlockSpec((None, seq, 256), per_b(3)),
                  pl.BlockSpec((None, seq, 256), per_b(0)),
                  pl.BlockSpec((None, seq, 256), per_b(1)),
                  pl.BlockSpec((None, ncmp, CMP_STRIDE * 128), per_b(0)),
                  pl.BlockSpec((None, ncmp, CMP_STRIDE * 128), per_b(0)),
                  pl.BlockSpec((None, tq, LANES), lambda b, i: (b, i, 1)),
                  ] + w_specs,
        out_specs=pl.BlockSpec((None, tq, 512), lambda b, i: (b, i, 0)),
        out_shape=jax.ShapeDtypeStruct((bsz, seq, 512), BF16),
        scratch_shapes=[pltpu.VMEM((ncmp, 256), BF16), pltpu.VMEM((ncmp, 256), BF16),
                        pltpu.VMEM((NSA_HEADS, tq, LANES), F32)],
        compiler_params=_cparams("arbitrary", "arbitrary"),
        name="nsa",
    )(a3, a3, a3, v3, v3, kcr, vcr, m3, *wts)


def _dsa_kernel(qd_ref, qi_ref, ki_ref, ckv_ref, misc_ref, gkv_ref, wuk_ref, wuv_ref,
                cos_ref, s1_ref, s2_ref, o_ref, kd_ref, vt_ref, *, tq, seq, topk):
    i = pl.program_id(1)

    @pl.when(i == 0)
    def _():
        ck = ckv_ref[...]
        cn = ck * lax.rsqrt(jnp.mean(ck * ck, axis=-1, keepdims=True) + NORM_EPS) * gkv_ref[...]
        cb = cn.astype(BF16)
        kd_ref[...] = _rope(_dot(cb, wuk_ref[...]), cos_ref[...], s1_ref[...], s2_ref[...]).astype(BF16)
        vt_ref[...] = _dot(cb, wuv_ref[...]).T.astype(BF16)

    t0 = i * tq
    misc_t = misc_ref[...].T
    int_min = jnp.int32(-2 ** 31)
    idx_bits = max(1, (seq - 1).bit_length()) + 1

    def attend(nk):
        kpos = lax.broadcasted_iota(I32, (nk, tq), 0)
        tpos = t0 + lax.broadcasted_iota(I32, (nk, tq), 1)
        causal = kpos <= tpos

        score = jnp.zeros((nk, tq), F32)
        for h in range(IDX_HEADS):
            qm = _half_mask(qi_ref[:, (h // 2) * LANES:(h // 2 + 1) * LANES], h % 2) * (IDX_DIM ** -0.5)
            logit = _dot_nt(ki_ref[0:nk, :], qm)
            w_h = misc_t[MISC_WI + h:MISC_WI + h + 1, :] * (IDX_HEADS ** -0.5)
            score = score + w_h * jnp.maximum(logit, 0.0)
        score = jnp.where(causal, score, -jnp.inf)

        def as_float(cu):
            ks = cu ^ int_min
            return lax.bitcast_convert_type(ks ^ ((ks >> 31) & jnp.int32(0x7FFFFFFF)), F32)

        def bit_step(it, tu):
            cu = tu | lax.shift_left(jnp.int32(1), 31 - it)
            cnt = _colsum((score >= as_float(cu)).astype(F32))
            below_neg_inf = (cu >= 0) & (cu < jnp.int32(0x007FFFFF))
            return jnp.where((cnt >= topk) | below_neg_inf, cu, tu)
        thr = as_float(lax.fori_loop(0, 32, bit_step, jnp.zeros((1, tq), I32)))
        above = score > thr
        need = topk - _colsum(above.astype(F32))
        tie_pos = jnp.where(score == thr, kpos, jnp.int32(2 * seq))

        def tie_step(it, lim):
            cl = lim | lax.shift_left(jnp.int32(1), idx_bits - 1 - it)
            cnt = _colsum((tie_pos < cl).astype(F32))
            return jnp.where(cnt <= need, cl, lim)
        lim = lax.fori_loop(0, idx_bits, tie_step, jnp.zeros((1, tq), I32))
        sel = causal & (above | (tie_pos < lim))
        bias = jnp.where(sel, 0.0, NEG_INF)

        row = lax.broadcasted_iota(I32, (LANES, tq), 0)
        pair = None
        for h in range(DSA_HEADS):
            qm = _half_mask(qd_ref[:, (h // 2) * LANES:(h // 2 + 1) * LANES], h % 2) * (HEAD_DIM ** -0.5)
            s = _dot_nt(kd_ref[0:nk, :], qm) + bias
            p = jnp.exp(s - _colmax(s))
            den = jnp.maximum(_colsum(p), 1e-30)
            o_t = _dot(vt_ref[:, 0:nk], p.astype(BF16)) / den
            if h % 2 == 0:
                pair = o_t
            else:
                o_ref[:, (h // 2) * LANES:(h // 2 + 1) * LANES] = (
                    jnp.where(row < HEAD_DIM, pair, o_t).T.astype(BF16))

    step = seq // DSA_KEY_VARIANTS
    per = step // tq
    for v in range(DSA_KEY_VARIANTS):
        pl.when(i // per == v)(functools.partial(attend, (v + 1) * step))


def _dsa(a3, m3, gkv, wuk, wuv, cos, s1, s2, seq):
    bsz = a3.shape[0]
    tq = 128
    topk = min(DSA_TOPK, seq // 4)
    full2 = lambda b, i: (0, 0)
    return pl.pallas_call(
        functools.partial(_dsa_kernel, tq=tq, seq=seq, topk=topk),
        grid=(bsz, seq // tq),
        in_specs=[pl.BlockSpec((None, tq, 512), lambda b, i: (b, i, 2)),
                  pl.BlockSpec((None, tq, 512), lambda b, i: (b, i, 3)),
                  pl.BlockSpec((None, seq, LANES), lambda b, i: (b, 0, 16)),
                  pl.BlockSpec((None, seq, LANES), lambda b, i: (b, 0, 0)),
                  pl.BlockSpec((None, tq, LANES), lambda b, i: (b, i, 1)),
                  pl.BlockSpec((1, LANES), full2),
                  pl.BlockSpec((LANES, LANES), full2),
                  pl.BlockSpec((LANES, LANES), full2),
                  pl.BlockSpec((seq, LANES), full2),
                  pl.BlockSpec((seq, LANES), full2),
                  pl.BlockSpec((seq, LANES), full2)],
        out_specs=pl.BlockSpec((None, tq, 512), lambda b, i: (b, i, 0)),
        out_shape=jax.ShapeDtypeStruct((bsz, seq, 512), BF16),
        scratch_shapes=[pltpu.VMEM((seq, LANES), BF16), pltpu.VMEM((LANES, seq), BF16)],
        compiler_params=_cparams("arbitrary", "arbitrary"),
        name="dsa",
    )(a3, a3, a3, m3, m3, gkv, wuk, wuv, cos, s1, s2)


def _merge_kernel(on_ref, od_ref, gm_ref, x_ref, g1_ref, sc_ref, sh_ref, n2_ref,
                  wn_ref, wd_ref, wo_ref, x1_ref, h2_ref):
    gm = gm_ref[...]
    merged = (jax.nn.sigmoid(gm[:, :D_MODEL]) * _dot(on_ref[...], wn_ref[...])
              + jax.nn.sigmoid(gm[:, D_MODEL:]) * _dot(od_ref[...], wd_ref[...]))
    x1 = x_ref[...] + g1_ref[...] * _dot(merged.astype(BF16), wo_ref[...])
    x1_ref[...] = x1
    y = x1 * lax.rsqrt(jnp.mean(x1 * x1, axis=-1, keepdims=True) + NORM_EPS) * n2_ref[...]
    h2_ref[...] = (y * (1.0 + sc_ref[...]) + sh_ref[...]).astype(BF16)


def _merge(o_nsa, o_dsa, gm, x2, g1, sc2, sh2, n2, wn, wd, wo, seq):
    t, d = x2.shape
    tm = 256
    nb = seq // tm
    row = lambda i: (i, 0)
    bat = lambda i: (i // nb, 0, 0)
    full = lambda i: (0, 0)
    return pl.pallas_call(
        _merge_kernel,
        grid=(t // tm,),
        in_specs=[pl.BlockSpec((tm, 512), row), pl.BlockSpec((tm, 512), row),
                  pl.BlockSpec((tm, NG), row), pl.BlockSpec((tm, d), row),
                  pl.BlockSpec((None, 1, d), bat), pl.BlockSpec((None, 1, d), bat),
                  pl.BlockSpec((None, 1, d), bat), pl.BlockSpec((1, d), full),
                  pl.BlockSpec(wn.shape, full), pl.BlockSpec(wd.shape, full), pl.BlockSpec(wo.shape, full)],
        out_specs=[pl.BlockSpec((tm, d), row), pl.BlockSpec((tm, d), row)],
        out_shape=[jax.ShapeDtypeStruct((t, d), F32), jax.ShapeDtypeStruct((t, d), BF16)],
        compiler_params=_cparams("arbitrary"),
        name="merge",
    )(o_nsa, o_dsa, gm, x2, g1, sc2, sh2, n2, wn, wd, wo)


def _topk_rows(vals, k):
    n, t = vals.shape
    ridx = lax.broadcasted_iota(I32, (n, t), 0).astype(F32)
    slot = lax.broadcasted_iota(I32, (k, t), 0)
    top_v = jnp.zeros((k, t), F32)
    top_i = jnp.zeros((k, t), F32)
    for j in range(k):
        m = jnp.max(vals, axis=0, keepdims=True)
        idx = jnp.min(jnp.where(vals == m, ridx, float(n)), axis=0, keepdims=True)
        top_v = jnp.where(slot == j, m, top_v)
        top_i = jnp.where(slot == j, idx, top_i)
        vals = jnp.where(ridx == idx, -jnp.inf, vals)
    return top_v, top_i


def _route_kernel(h_ref, wq_ref, k2_ref, i_ref, j_ref, g_ref, q3_ref, is_ref, js_ref, gs_ref, cand_ref, *, tm):
    q = _dot(h_ref[...], wq_ref[...]).astype(BF16)
    for h in range(PEER_HEADS):
        q3_ref[h] = q[:, h * LANES:(h + 1) * LANES]

    def head(h, carry):
        s = _dot_nt(k2_ref[h], q3_ref[h])
        s0, i0 = _topk_rows(s[:PEER_KEYS], PEER_TOPK)
        s1, i1 = _topk_rows(s[PEER_KEYS:], PEER_TOPK)
        r0 = 0
        for a in range(PEER_TOPK):
            nb = PEER_TOPK // (a + 1)
            cand_ref[r0:r0 + nb, :] = s0[a:a + 1, :] + s1[0:nb, :]
            r0 += nb
        cand_ref[r0:, :] = jnp.full((cand_ref.shape[0] - r0, tm), -jnp.inf, F32)
        top_s, top_r = _topk_rows(cand_ref[...], PEER_TOPK)
        pa = jnp.zeros((PEER_TOPK, tm), F32)
        pb = jnp.zeros((PEER_TOPK, tm), F32)
        r = 0
        for a in range(PEER_TOPK):
            for b in range(PEER_TOPK // (a + 1)):
                hit = top_r == r
                pa = jnp.where(hit, float(a), pa)
                pb = jnp.where(hit, float(b), pb)
                r += 1
        ii = jnp.zeros((PEER_TOPK, tm), F32)
        jj = jnp.zeros((PEER_TOPK, tm), F32)
        for a in range(PEER_TOPK):
            ii = jnp.where(pa == a, i0[a:a + 1, :], ii)
            jj = jnp.where(pb == a, i1[a:a + 1, :], jj)
        e = jnp.exp(top_s - jnp.max(top_s, axis=0, keepdims=True))
        gate = e / jnp.sum(e, axis=0, keepdims=True)
        rows = pl.ds(pl.multiple_of(h * PEER_TOPK, PEER_TOPK), PEER_TOPK)
        is_ref[rows, :] = ii
        js_ref[rows, :] = jj
        gs_ref[rows, :] = gate
        return carry
    lax.fori_loop(0, PEER_HEADS, head, 0)
    i_ref[...] = is_ref[...].T
    j_ref[...] = js_ref[...].T
    g_ref[...] = gs_ref[...].T


def _route(h2, wq, k2):
    t, d = h2.shape
    tm = 512
    row = lambda i: (i, 0)
    nk = PEER_HEADS * PEER_TOPK
    ncand = -(-sum(PEER_TOPK // (a + 1) for a in range(PEER_TOPK)) // 8) * 8
    return pl.pallas_call(
        functools.partial(_route_kernel, tm=tm),
        grid=(t // tm,),
        in_specs=[pl.BlockSpec((tm, d), row),
                  pl.BlockSpec(wq.shape, lambda i: (0, 0)),
                  pl.BlockSpec(k2.shape, lambda i: (0, 0, 0))],
        out_specs=[pl.BlockSpec((tm, nk), row)] * 3,
        out_shape=[jax.ShapeDtypeStruct((t, nk), F32)] * 3,
        scratch_shapes=[pltpu.VMEM((PEER_HEADS, tm, LANES), BF16),
                        pltpu.VMEM((nk, tm), F32), pltpu.VMEM((nk, tm), F32), pltpu.VMEM((nk, tm), F32),
                        pltpu.VMEM((ncand, tm), F32)],
        compiler_params=_cparams("arbitrary"),
        name="route",
    )(h2, wq, k2)


def _gelu_exact(a):
    return 0.5 * a * (1.0 + lax.erf(a * (2.0 ** -0.5)))


def _expert_kernel(h_ref, i_ref, j_ref, g_ref, u_ref, v_ref, x1_ref, g2_ref, fg_ref, o_ref,
                   w3_ref, y_ref, acc_ref, *, tm, nsub, final_norm):
    c = pl.program_id(1)
    half = PEER_KEYS // 2
    per_half = half // nsub

    @pl.when(c == 0)
    def _():
        acc_ref[...] = jnp.zeros_like(acc_ref)

    @pl.when(c % per_half == 0)
    def _():
        i0 = (c // per_half) * half
        sub_i = (lax.broadcasted_iota(I32, (half, LANES), 0) + i0).astype(F32)
        sub_j = lax.broadcasted_iota(I32, (PEER_KEYS, LANES), 0).astype(F32)

        def group(tg, carry):
            r0 = pl.multiple_of(tg * GATE_GROUP, GATE_GROUP)
            ii = i_ref[pl.ds(r0, GATE_GROUP), :]
            jj = j_ref[pl.ds(r0, GATE_GROUP), :]
            gg = g_ref[pl.ds(r0, GATE_GROUP), :]
            for r in range(GATE_GROUP):
                a = (sub_i == ii[r:r + 1, :]).astype(BF16)
                rg = jnp.where(sub_j == jj[r:r + 1, :], gg[r:r + 1, :], 0.0).astype(BF16)
                w3_ref[pl.ds(pl.multiple_of((r0 + r) * W3_PITCH, 8), half), :] = _dot_nt(a, rg)
            return carry
        lax.fori_loop(0, tm // GATE_GROUP, group, 0)

    act = _dot_nt(h_ref[...], u_ref[...])
    for s in range(nsub):
        row_i = (c % per_half) * nsub + s
        w = w3_ref[pl.ds(row_i, tm, stride=W3_PITCH), :]
        sl = slice(s * LANES, (s + 1) * LANES)
        y_ref[:, sl] = (w * _gelu_exact(act[:, sl])).astype(BF16)
    acc_ref[...] += _dot(y_ref[...], v_ref[...])

    @pl.when(c == pl.num_programs(1) - 1)
    def _():
        x2 = x1_ref[...] + g2_ref[...] * acc_ref[...]
        if final_norm:
            x2 = x2 * lax.rsqrt(jnp.mean(x2 * x2, axis=-1, keepdims=True) + NORM_EPS) * fg_ref[...]
        o_ref[...] = x2


def _experts(h2, ii, jj, gg, u, v, x1, g2, fg, seq, final_norm):
    t, d = h2.shape
    tm = 512
    nsub = 8
    ne = nsub * PEER_KEYS
    nb = seq // tm
    row = lambda i, c: (i, 0)
    chunk = lambda i, c: (c, 0)
    return pl.pallas_call(
        functools.partial(_expert_kernel, tm=tm, nsub=nsub, final_norm=final_norm),
        grid=(t // tm, PEER_KEYS // nsub),
        in_specs=[pl.BlockSpec((tm, d), row),
                  pl.BlockSpec((tm, LANES), row), pl.BlockSpec((tm, LANES), row), pl.BlockSpec((tm, LANES), row),
                  pl.BlockSpec((ne, d), chunk), pl.BlockSpec((ne, d), chunk),
                  pl.BlockSpec((tm, d), row),
                  pl.BlockSpec((None, 1, d), lambda i, c: (i // nb, 0, 0)),
                  pl.BlockSpec((1, d), lambda i, c: (0, 0))],
        out_specs=pl.BlockSpec((tm, d), row),
        out_shape=jax.ShapeDtypeStruct((t, d), F32),
        scratch_shapes=[pltpu.VMEM((tm * W3_PITCH, LANES), F32),
                        pltpu.VMEM((tm, ne), BF16),
                        pltpu.VMEM((tm, d), F32)],
        compiler_params=_cparams("arbitrary", "arbitrary"),
        name="experts",
    )(h2, ii, jj, gg, u, v, x1, g2, fg)


def _dup(w):
    return jnp.concatenate([w, w], axis=1)


def _win_layout(w_in):
    nq, nkv = NSA_HEADS * HEAD_DIM, NSA_KV_HEADS * HEAD_DIM
    sizes = (nq, nkv, nkv, nkv, nkv, nkv, nkv, 3 * NSA_HEADS, DSA_HEADS * HEAD_DIM, DSA_KV_RANK,
             IDX_HEADS * IDX_DIM, IDX_DIM, IDX_HEADS, 2 * D_MODEL)
    offs = [0]
    for s in sizes:
        offs.append(offs[-1] + s)
    q_n, kc, vc, ks, vs, kw, vw, g_n, q_d, ckv, qi, ki, wi, g_m = [w_in[:, offs[k]:offs[k + 1]] for k in range(14)]
    dup2 = lambda w: jnp.concatenate([_dup(w[:, :HEAD_DIM]), _dup(w[:, HEAD_DIM:])], axis=1)
    misc = jnp.concatenate([g_n, wi, jnp.zeros((w_in.shape[0], LANES - 32), w_in.dtype)], axis=1)
    cols = [q_n, dup2(ks), dup2(kw), q_d, qi, _dup(ki), kc, dup2(vs), dup2(vw), vc, ckv, misc, g_m]
    return jnp.concatenate(cols, axis=1).astype(BF16)


def _rope_tables(seq):
    half = HEAD_DIM // 2
    pos = jnp.arange(seq, dtype=F32)
    inv = ROPE_THETA ** (-jnp.arange(half, dtype=F32) / half)
    ang = pos[:, None] * inv[None, :]
    cos, sin, zero = jnp.cos(ang), jnp.sin(ang), jnp.zeros((seq, half), F32)
    c = jnp.concatenate([cos, cos, cos, cos], axis=1)
    s1 = jnp.concatenate([-sin, zero, -sin, zero], axis=1)
    s2 = jnp.concatenate([zero, sin, zero, sin], axis=1)
    return c, s1, s2


def _cmp_weights(pos, w1, w2):
    g = NSA_KV_HEADS
    eye = jnp.eye(g, dtype=F32)
    big = jnp.einsum("lde,gh->lgdhe", w1, eye).reshape(CMP_BLOCK, g * HEAD_DIM, g * HEAD_DIM)
    wa = big[:CMP_STRIDE].reshape(CMP_STRIDE * g * HEAD_DIM, g * HEAD_DIM)
    wb = big[CMP_STRIDE:].reshape(CMP_STRIDE * g * HEAD_DIM, g * HEAD_DIM)
    z = jnp.zeros_like(w2)
    w2d = jnp.concatenate([jnp.concatenate([w2, w2, z, z], axis=1),
                           jnp.concatenate([z, z, w2, w2], axis=1)], axis=0)
    pos_t = jnp.broadcast_to(pos[:, None, :], (CMP_BLOCK, g, HEAD_DIM))
    pa = pos_t[:CMP_STRIDE].reshape(1, -1)
    pb = pos_t[CMP_STRIDE:].reshape(1, -1)
    return wa.astype(BF16), wb.astype(BF16), w2d.astype(BF16), pa, pb


def _selection_constants(seq):
    ncmp = seq // CMP_STRIDE
    nslc = seq // SLC_BLOCK
    per = SLC_BLOCK // CMP_STRIDE
    n = jnp.arange(ncmp)[:, None]
    j = jnp.arange(LANES)[None, :]
    mimp = ((n // per == j).astype(F32) + ((n + 1) // per == j).astype(F32)) * (j < nslc)
    mimp = mimp * (n < ncmp - 1)
    key = jnp.arange(seq)[None, :]
    eexp = (key // SLC_BLOCK == jnp.arange(LANES)[:, None]).astype(F32)
    return mimp.T.astype(BF16), eexp.astype(BF16)


def _subkey_blocks(subkeys):
    z = jnp.zeros_like(subkeys[:, 0])
    top = jnp.concatenate([subkeys[:, 0], z], axis=2)
    bot = jnp.concatenate([z, subkeys[:, 1]], axis=2)
    return jnp.concatenate([top, bot], axis=1).astype(BF16)


def kernel(x, c, ada_w, ada_b, norm1_g, w_in, cmp_pos, cmp_w1_k, cmp_w2_k, cmp_w1_v, cmp_w2_v,
           kv_norm_g, w_uk, w_uv, w_br_nsa, w_br_dsa, w_out, norm2_g,
           peer_wq, peer_subkeys, peer_u, peer_v, final_g):
    bsz, seq, d = x.shape
    depth = ada_w.shape[0]
    t = bsz * seq
    cos, s1, s2 = _rope_tables(seq)
    mimp, eexp = _selection_constants(seq)
    xt = x.reshape(t, d)
    out = xt
    for l in range(depth):
        mod = _ada(c, ada_w[l], ada_b[l]).reshape(bsz, 1, 6 * d)
        sh1, sc1, g1, sh2, sc2, g2 = [mod[:, :, k * d:(k + 1) * d] for k in range(6)]
        a, kc, vv, vc, mm, gm = _proj(xt, sc1, sh1, norm1_g[l].reshape(1, d), _win_layout(w_in[l]),
                                      cos, s1, s2, seq)
        a3 = a.reshape(bsz, seq, NA)
        m3 = mm.reshape(bsz, seq, NM)
        ncmp = seq // CMP_STRIDE
        wak, wbk, w2k, pa, pb = _cmp_weights(cmp_pos[l], cmp_w1_k[l], cmp_w2_k[l])
        wav, wbv, w2v, _, _ = _cmp_weights(cmp_pos[l], cmp_w1_v[l], cmp_w2_v[l])
        o_nsa = _nsa(a3, vv.reshape(bsz, seq, NV), kc.reshape(bsz, ncmp, CMP_STRIDE * 128),
                     vc.reshape(bsz, ncmp, CMP_STRIDE * 128), m3,
                     (wak, wbk, w2k, wav, wbv, w2v, pa, pb, mimp, eexp), seq)
        o_dsa = _dsa(a3, m3, kv_norm_g[l].reshape(1, DSA_KV_RANK), _dup(w_uk[l]).astype(BF16),
                     _dup(w_uv[l]).astype(BF16), cos, s1, s2, seq)
        x1, h2 = _merge(o_nsa.reshape(t, 512), o_dsa.reshape(t, 512), gm, xt, g1, sc2, sh2,
                        norm2_g[l].reshape(1, d), w_br_nsa[l].astype(BF16), w_br_dsa[l].astype(BF16),
                        w_out[l].astype(BF16), seq)
        ii, jj, gg = _route(h2, peer_wq[l].astype(BF16), _subkey_blocks(peer_subkeys[l]))
        out = _experts(h2, ii, jj, gg, peer_u[l].astype(BF16), peer_v[l].astype(BF16), x1, g2,
                       final_g.reshape(1, d), seq, final_norm=(l == depth - 1))
        xt = out
    return out.reshape(bsz, seq, d)
```

```python
import functools

import jax
import jax.numpy as jnp
from jax import lax
from jax.experimental import pallas as pl
from jax.experimental.pallas import tpu as pltpu

F32 = jnp.float32
BF16 = jnp.bfloat16
I32 = jnp.int32

D_MODEL = 1024
HEAD_DIM = 64
ROPE_THETA = 10000.0
NORM_EPS = 1e-6
NEG_INF = -1e30

NSA_HEADS = 8
NSA_KV_HEADS = 2
CMP_BLOCK = 32
CMP_STRIDE = 16
SLC_BLOCK = 64
SLC_TOPN = 16
WINDOW = 512
NSA_KEY_VARIANTS = 8

DSA_HEADS = 8
DSA_KV_RANK = 128
IDX_HEADS = 8
IDX_DIM = 64
DSA_TOPK = 256
DSA_KEY_VARIANTS = 8

PEER_HEADS = 8
PEER_KEYS = 128
PEER_HALF = 64
PEER_TOPK = 16

LANES = 128
W3_PAD = 8
GATE_GROUP = 32
VMEM_LIMIT = 56 * 1024 * 1024

NA = 2176
NV = 512
NM = 256
NG = 2 * D_MODEL
OFF_KC = NA
OFF_V = OFF_KC + 128
OFF_VC = OFF_V + NV
OFF_M = OFF_VC + 128
OFF_G = OFF_M + NM
NZ = OFF_G + NG
MISC_GATE = 0
MISC_WI = 24


def _dot(a, b):
    return jnp.dot(a, b, preferred_element_type=F32)


def _dot_nt(a, b):
    return lax.dot_general(a, b, (((1,), (1,)), ((), ())), preferred_element_type=F32)


def _cparams(*sem):
    return pltpu.CompilerParams(dimension_semantics=sem, vmem_limit_bytes=VMEM_LIMIT)


def _rope(z, c, s1, s2):
    return z * c + pltpu.roll(z, 96, 1) * s1 + pltpu.roll(z, 32, 1) * s2


def _col_reduce(x, op, reduce):
    n = x.shape[0]
    while n > 8 and (n // 8) % 2 == 0:
        n //= 2
        x = op(x[:n], x[n:])
    return reduce(x, axis=0, keepdims=True)


def _colsum(x):
    return _col_reduce(x, jnp.add, jnp.sum)


def _colmax(x):
    return _col_reduce(x, jnp.maximum, jnp.max)


def _half_mask(x, odd):
    lane = lax.broadcasted_iota(I32, x.shape, 1)
    keep = (lane >= HEAD_DIM) if odd else (lane < HEAD_DIM)
    return jnp.where(keep, x, jnp.zeros_like(x))


def _ada_kernel(c_ref, w_ref, b_ref, o_ref):
    c = c_ref[...]
    sc = c * jax.nn.sigmoid(c)
    o_ref[...] = _dot(sc.astype(BF16), w_ref[...].astype(BF16)) + b_ref[...]


def _ada(c, w, b):
    bsz, d = c.shape
    n = w.shape[1]
    tn = 1024
    return pl.pallas_call(
        _ada_kernel,
        grid=(n // tn,),
        in_specs=[pl.BlockSpec((bsz, d), lambda j: (0, 0)),
                  pl.BlockSpec((d, tn), lambda j: (0, j)),
                  pl.BlockSpec((1, tn), lambda j: (0, j))],
        out_specs=pl.BlockSpec((bsz, tn), lambda j: (0, j)),
        out_shape=jax.ShapeDtypeStruct((bsz, n), F32),
        compiler_params=_cparams("arbitrary"),
        name="ada",
    )(c, w, b.reshape(1, n))


def _proj_kernel(x_ref, sc_ref, sh_ref, g_ref, w_ref, cos_ref, s1_ref, s2_ref,
                 a_ref, kc_ref, v_ref, vc_ref, m_ref, gm_ref):
    x = x_ref[...]
    y = x * lax.rsqrt(jnp.mean(x * x, axis=-1, keepdims=True) + NORM_EPS) * g_ref[...]
    h = y * (1.0 + sc_ref[...]) + sh_ref[...]
    z = _dot(h.astype(BF16), w_ref[...])
    c, s1, s2 = cos_ref[...], s1_ref[...], s2_ref[...]
    for k in range(NA // LANES):
        sl = slice(k * LANES, (k + 1) * LANES)
        a_ref[:, sl] = _rope(z[:, sl], c, s1, s2).astype(BF16)
    kc_ref[...] = _rope(z[:, OFF_KC:OFF_KC + 128], c, s1, s2)
    v_ref[...] = z[:, OFF_V:OFF_V + NV].astype(BF16)
    vc_ref[...] = z[:, OFF_VC:OFF_VC + 128]
    m_ref[...] = z[:, OFF_M:OFF_M + NM]
    gm_ref[...] = z[:, OFF_G:OFF_G + NG]


def _proj(x2, sc, sh, g, w, cos, s1, s2, seq):
    t, d = x2.shape
    tm = 256
    nb = seq // tm
    row = lambda i: (i, 0)
    bat = lambda i: (i // nb, 0, 0)
    pos = lambda i: (i % nb, 0)
    full = lambda i: (0, 0)
    widths = (NA, 128, NV, 128, NM, NG)
    dtypes = (BF16, F32, BF16, F32, F32, F32)
    return pl.pallas_call(
        _proj_kernel,
        grid=(t // tm,),
        in_specs=[pl.BlockSpec((tm, d), row),
                  pl.BlockSpec((None, 1, d), bat),
                  pl.BlockSpec((None, 1, d), bat),
                  pl.BlockSpec((1, d), full),
                  pl.BlockSpec((d, NZ), full),
                  pl.BlockSpec((tm, LANES), pos),
                  pl.BlockSpec((tm, LANES), pos),
                  pl.BlockSpec((tm, LANES), pos)],
        out_specs=[pl.BlockSpec((tm, n), row) for n in widths],
        out_shape=[jax.ShapeDtypeStruct((t, n), dt) for n, dt in zip(widths, dtypes)],
        compiler_params=_cparams("arbitrary"),
        name="proj",
    )(x2, sc, sh, g, w, cos, s1, s2)


def _masked_attend(s, mask, v):
    s = jnp.where(mask, s, NEG_INF)
    m = jnp.max(s, axis=-1, keepdims=True)
    p = jnp.where(mask, jnp.exp(s - m), 0.0)
    den = jnp.maximum(jnp.sum(p, axis=-1, keepdims=True), 1e-30)
    return _dot(p.astype(BF16), v) / den, p, den


def _nsa_kernel(q_ref, ks_ref, kw_ref, vs_ref, vw_ref, kc_ref, vc_ref, misc_ref,
                wak_ref, wbk_ref, w2k_ref, wav_ref, wbv_ref, w2v_ref, posa_ref, posb_ref,
                mimp_ref, eexp_ref, o_ref, kcmp_ref, vcmp_ref, oslc_ref, *, tq, seq):
    i = pl.program_id(1)
    ncmp = seq // CMP_STRIDE
    scale = HEAD_DIM ** -0.5

    @pl.when(i == 0)
    def _():
        def compress(c_ref, wa_ref, wb_ref, w2_ref):
            c = c_ref[...]
            ha = _dot((c + posa_ref[...]).astype(BF16), wa_ref[...])
            hb = _dot((c + posb_ref[...]).astype(BF16), wb_ref[...])
            hid = ha + pltpu.roll(hb, ncmp - 1, 0)
            return _dot(jax.nn.gelu(hid).astype(BF16), w2_ref[...])
        kcmp_ref[...] = compress(kc_ref, wak_ref, wbk_ref, w2k_ref).astype(BF16)
        vcmp_ref[...] = compress(vc_ref, wav_ref, wbv_ref, w2v_ref).astype(BF16)

    t0 = i * tq
    trow = t0 + lax.broadcasted_iota(I32, (tq, LANES), 0)
    lane = lax.broadcasted_iota(I32, (tq, LANES), 1)
    sig = jax.nn.sigmoid(misc_ref[...])

    hpg = NSA_HEADS // NSA_KV_HEADS
    qs = [_half_mask(q_ref[:, (h // 2) * LANES:(h // 2 + 1) * LANES], h % 2) * scale for h in range(NSA_HEADS)]

    cmp_mask = (lane * CMP_STRIDE + (CMP_BLOCK - 1)) <= trow
    o_cmp, imp = [], []
    for h in range(NSA_HEADS):
        g = h // hpg
        s = _dot_nt(qs[h], kcmp_ref[:, g * LANES:(g + 1) * LANES])
        o, p, den = _masked_attend(s, cmp_mask, vcmp_ref[:, g * LANES:(g + 1) * LANES])
        o_cmp.append(o)
        pn = p / den
        if h % hpg == 0:
            imp.append(pn)
        else:
            imp[g] = imp[g] + pn

    chosen = []
    for g in range(NSA_KV_HEADS):
        pg = imp[g]
        p_hi = pg.astype(BF16)
        r1 = pg - p_hi.astype(F32)
        p_mid = r1.astype(BF16)
        p_lo = (r1 - p_mid.astype(F32)).astype(BF16)
        blk_t = (_dot_nt(mimp_ref[...], p_hi) + _dot_nt(mimp_ref[...], p_mid) + _dot_nt(mimp_ref[...], p_lo))
        nslc = seq // SLC_BLOCK
        jrow = lax.broadcasted_iota(I32, (nslc, tq), 0)
        tcol = t0 + lax.broadcasted_iota(I32, (nslc, tq), 1)
        cur = tcol // SLC_BLOCK
        forced = (jrow == 0) | (jrow == cur) | (jrow == cur - 1)
        admissible = jrow * SLC_BLOCK <= tcol
        val = jnp.where(forced, jnp.inf, jnp.where(admissible, blk_t[0:nslc, :], -jnp.inf))
        rank = jnp.zeros((nslc, tq), I32)
        for j in range(nslc):
            vj = val[j:j + 1, :]
            ahead = (vj > val) | ((vj == val) & (jrow > j))
            rank = rank + ahead.astype(I32)
        chosen_t = jnp.concatenate([(rank < SLC_TOPN).astype(F32), jnp.zeros((LANES - nslc, tq), F32)], axis=0)
        chosen.append(chosen_t.T.astype(BF16))

    def biased_attend(q, k, v, bias):
        s = _dot_nt(q, k) + bias
        p = jnp.exp(s - jnp.max(s, axis=-1, keepdims=True))
        den = jnp.maximum(jnp.sum(p, axis=-1, keepdims=True), 1e-30)
        return _dot(p.astype(BF16), v) / den

    def selected_branch(nk):
        causal = lax.broadcasted_iota(I32, (tq, nk), 1) <= t0 + lax.broadcasted_iota(I32, (tq, nk), 0)
        for g in range(NSA_KV_HEADS):
            picked = _dot(chosen[g], eexp_ref[:, 0:nk]) > 0.5
            bias = jnp.where(picked & causal, 0.0, NEG_INF)
            ksd = ks_ref[0:nk, g * LANES:(g + 1) * LANES]
            vsd = vs_ref[0:nk, g * LANES:(g + 1) * LANES]
            for h in range(g * hpg, (g + 1) * hpg):
                oslc_ref[h] = biased_attend(qs[h], ksd, vsd, bias)

    step = seq // NSA_KEY_VARIANTS
    per = step // tq
    for v in range(NSA_KEY_VARIANTS):
        pl.when(i // per == v)(functools.partial(selected_branch, (v + 1) * step))

    kspan = WINDOW + tq
    kstart = pl.multiple_of(jnp.maximum(t0 - WINDOW, 0), tq)
    diff_w = (t0 + lax.broadcasted_iota(I32, (tq, kspan), 0)) - (kstart + lax.broadcasted_iota(I32, (tq, kspan), 1))
    win_bias = jnp.where((diff_w >= 0) & (diff_w < WINDOW), 0.0, NEG_INF)
    pair = None
    for h in range(NSA_HEADS):
        g = h // hpg
        kwd = kw_ref[pl.ds(kstart, kspan), g * LANES:(g + 1) * LANES]
        vwd = vw_ref[pl.ds(kstart, kspan), g * LANES:(g + 1) * LANES]
        o_w = biased_attend(qs[h], kwd, vwd, win_bias)
        c0 = MISC_GATE + 3 * h
        o_h = sig[:, c0:c0 + 1] * o_cmp[h] + sig[:, c0 + 1:c0 + 2] * oslc_ref[h] + sig[:, c0 + 2:c0 + 3] * o_w
        if h % 2 == 0:
            pair = o_h
        else:
            o_ref[:, (h // 2) * LANES:(h // 2 + 1) * LANES] = jnp.where(lane < HEAD_DIM, pair, o_h).astype(BF16)


def _nsa(a3, v3, kcr, vcr, m3, wts, seq):
    bsz = a3.shape[0]
    tq = 128
    ncmp = seq // CMP_STRIDE
    per_b = lambda blk: (lambda b, i: (b, 0, blk))
    full2 = lambda b, i: (0, 0)
    w_specs = [pl.BlockSpec(w.shape, full2) for w in wts]
    return pl.pallas_call(
        functools.partial(_nsa_kernel, tq=tq, seq=seq),
        grid=(bsz, seq // tq),
        in_specs=[pl.BlockSpec((None, tq, 512), lambda b, i: (b, i, 0)),
                  pl.BlockSpec((None, seq, 256), per_b(2)),
                  pl.BlockSpec((None, seq, 256), per_b(3)),
                  pl.BlockSpec((None, seq, 256), per_b(0)),
                  pl.BlockSpec((None, seq, 256), per_b(1)),
                  pl.BlockSpec((None, ncmp, CMP_STRIDE * 128), per_b(0)),
                  pl.BlockSpec((None, ncmp, CMP_STRIDE * 128), per_b(0)),
                  pl.BlockSpec((None, tq, LANES), lambda b, i: (b, i, 1)),
                  ] + w_specs,
        out_specs=pl.BlockSpec((None, tq, 512), lambda b, i: (b, i, 0)),
        out_shape=jax.ShapeDtypeStruct((bsz, seq, 512), BF16),
        scratch_shapes=[pltpu.VMEM((ncmp, 256), BF16), pltpu.VMEM((ncmp, 256), BF16),
                        pltpu.VMEM((NSA_HEADS, tq, LANES), F32)],
        compiler_params=_cparams("arbitrary", "arbitrary"),
        name="nsa",
    )(a3, a3, a3, v3, v3, kcr, vcr, m3, *wts)


def _dsa_kernel(qd_ref, qi_ref, ki_ref, ckv_ref, misc_ref, gkv_ref, wuk_ref, wuv_ref,
                cos_ref, s1_ref, s2_ref, o_ref, kd_ref, vt_ref, *, tq, seq, topk):
    i = pl.program_id(1)

    @pl.when(i == 0)
    def _():
        ck = ckv_ref[...]
        cn = ck * lax.rsqrt(jnp.mean(ck * ck, axis=-1, keepdims=True) + NORM_EPS) * gkv_ref[...]
        cb = cn.astype(BF16)
        kd_ref[...] = _rope(_dot(cb, wuk_ref[...]), cos_ref[...], s1_ref[...], s2_ref[...]).astype(BF16)
        vt_ref[...] = _dot(cb, wuv_ref[...]).T.astype(BF16)

    t0 = i * tq
    misc_t = misc_ref[...].T
    int_min = jnp.int32(-2 ** 31)
    idx_bits = max(1, (seq - 1).bit_length()) + 1

    def attend(nk):
        kpos = lax.broadcasted_iota(I32, (nk, tq), 0)
        tpos = t0 + lax.broadcasted_iota(I32, (nk, tq), 1)
        causal = kpos <= tpos

        score = jnp.zeros((nk, tq), F32)
        for h in range(IDX_HEADS):
            qm = _half_mask(qi_ref[:, (h // 2) * LANES:(h // 2 + 1) * LANES], h % 2) * (IDX_DIM ** -0.5)
            logit = _dot_nt(ki_ref[0:nk, :], qm)
            w_h = misc_t[MISC_WI + h:MISC_WI + h + 1, :] * (IDX_HEADS ** -0.5)
            score = score + w_h * jnp.maximum(logit, 0.0)
        score = jnp.where(causal, score, -jnp.inf)

        def as_float(cu):
            ks = cu ^ int_min
            return lax.bitcast_convert_type(ks ^ ((ks >> 31) & jnp.int32(0x7FFFFFFF)), F32)

        def bit_step(it, tu):
            cu = tu | lax.shift_left(jnp.int32(1), 31 - it)
            cnt = _colsum((score >= as_float(cu)).astype(F32))
            below_neg_inf = (cu >= 0) & (cu < jnp.int32(0x007FFFFF))
            return jnp.where((cnt >= topk) | below_neg_inf, cu, tu)
        thr = as_float(lax.fori_loop(0, 32, bit_step, jnp.zeros((1, tq), I32)))
        above = score > thr
        need = topk - _colsum(above.astype(F32))
        tie = score == thr
        tie_pos = jnp.where(tie, kpos, jnp.int32(2 * seq))
        surplus = jnp.max(_colsum(tie.astype(F32)) - need)

        def lowest_ties():
            def tie_step(it, lim):
                cl = lim | lax.shift_left(jnp.int32(1), idx_bits - 1 - it)
                cnt = _colsum((tie_pos < cl).astype(F32))
                return jnp.where(cnt <= need, cl, lim)
            return lax.fori_loop(0, idx_bits, tie_step, jnp.zeros((1, tq), I32))

        lim = lax.cond(surplus > 0.0, lowest_ties, lambda: jnp.full((1, tq), 2 * seq, I32))
        sel = causal & (above | (tie_pos < lim))
        bias = jnp.where(sel, 0.0, NEG_INF)

        row = lax.broadcasted_iota(I32, (LANES, tq), 0)
        pair = None
        for h in range(DSA_HEADS):
            qm = _half_mask(qd_ref[:, (h // 2) * LANES:(h // 2 + 1) * LANES], h % 2) * (HEAD_DIM ** -0.5)
            s = _dot_nt(kd_ref[0:nk, :], qm) + bias
            p = jnp.exp(s - _colmax(s))
            den = jnp.maximum(_colsum(p), 1e-30)
            o_t = _dot(vt_ref[:, 0:nk], p.astype(BF16)) / den
            if h % 2 == 0:
                pair = o_t
            else:
                o_ref[:, (h // 2) * LANES:(h // 2 + 1) * LANES] = (
                    jnp.where(row < HEAD_DIM, pair, o_t).T.astype(BF16))

    step = seq // DSA_KEY_VARIANTS
    per = step // tq
    for v in range(DSA_KEY_VARIANTS):
        pl.when(i // per == v)(functools.partial(attend, (v + 1) * step))


def _dsa(a3, m3, gkv, wuk, wuv, cos, s1, s2, seq):
    bsz = a3.shape[0]
    tq = 128
    topk = min(DSA_TOPK, seq // 4)
    full2 = lambda b, i: (0, 0)
    return pl.pallas_call(
        functools.partial(_dsa_kernel, tq=tq, seq=seq, topk=topk),
        grid=(bsz, seq // tq),
        in_specs=[pl.BlockSpec((None, tq, 512), lambda b, i: (b, i, 2)),
                  pl.BlockSpec((None, tq, 512), lambda b, i: (b, i, 3)),
                  pl.BlockSpec((None, seq, LANES), lambda b, i: (b, 0, 16)),
                  pl.BlockSpec((None, seq, LANES), lambda b, i: (b, 0, 0)),
                  pl.BlockSpec((None, tq, LANES), lambda b, i: (b, i, 1)),
                  pl.BlockSpec((1, LANES), full2),
                  pl.BlockSpec((LANES, LANES), full2),
                  pl.BlockSpec((LANES, LANES), full2),
                  pl.BlockSpec((seq, LANES), full2),
                  pl.BlockSpec((seq, LANES), full2),
                  pl.BlockSpec((seq, LANES), full2)],
        out_specs=pl.BlockSpec((None, tq, 512), lambda b, i: (b, i, 0)),
        out_shape=jax.ShapeDtypeStruct((bsz, seq, 512), BF16),
        scratch_shapes=[pltpu.VMEM((seq, LANES), BF16), pltpu.VMEM((LANES, seq), BF16)],
        compiler_params=_cparams("arbitrary", "arbitrary"),
        name="dsa",
    )(a3, a3, a3, m3, m3, gkv, wuk, wuv, cos, s1, s2)


def _merge_kernel(on_ref, od_ref, gm_ref, x_ref, g1_ref, sc_ref, sh_ref, n2_ref,
                  wn_ref, wd_ref, wo_ref, x1_ref, h2_ref):
    gm = gm_ref[...]
    merged = (jax.nn.sigmoid(gm[:, :D_MODEL]) * _dot(on_ref[...], wn_ref[...])
              + jax.nn.sigmoid(gm[:, D_MODEL:]) * _dot(od_ref[...], wd_ref[...]))
    x1 = x_ref[...] + g1_ref[...] * _dot(merged.astype(BF16), wo_ref[...])
    x1_ref[...] = x1
    y = x1 * lax.rsqrt(jnp.mean(x1 * x1, axis=-1, keepdims=True) + NORM_EPS) * n2_ref[...]
    h2_ref[...] = (y * (1.0 + sc_ref[...]) + sh_ref[...]).astype(BF16)


def _merge(o_nsa, o_dsa, gm, x2, g1, sc2, sh2, n2, wn, wd, wo, seq):
    t, d = x2.shape
    tm = 256
    nb = seq // tm
    row = lambda i: (i, 0)
    bat = lambda i: (i // nb, 0, 0)
    full = lambda i: (0, 0)
    return pl.pallas_call(
        _merge_kernel,
        grid=(t // tm,),
        in_specs=[pl.BlockSpec((tm, 512), row), pl.BlockSpec((tm, 512), row),
                  pl.BlockSpec((tm, NG), row), pl.BlockSpec((tm, d), row),
                  pl.BlockSpec((None, 1, d), bat), pl.BlockSpec((None, 1, d), bat),
                  pl.BlockSpec((None, 1, d), bat), pl.BlockSpec((1, d), full),
                  pl.BlockSpec(wn.shape, full), pl.BlockSpec(wd.shape, full), pl.BlockSpec(wo.shape, full)],
        out_specs=[pl.BlockSpec((tm, d), row), pl.BlockSpec((tm, d), row)],
        out_shape=[jax.ShapeDtypeStruct((t, d), F32), jax.ShapeDtypeStruct((t, d), BF16)],
        compiler_params=_cparams("arbitrary"),
        name="merge",
    )(o_nsa, o_dsa, gm, x2, g1, sc2, sh2, n2, wn, wd, wo)


def _topk_rows(vals, k):
    n, t = vals.shape
    ridx = lax.broadcasted_iota(I32, (n, t), 0).astype(F32)
    slot = lax.broadcasted_iota(I32, (k, t), 0)
    top_v = jnp.zeros((k, t), F32)
    top_i = jnp.zeros((k, t), F32)
    for j in range(k):
        m = jnp.max(vals, axis=0, keepdims=True)
        idx = jnp.min(jnp.where(vals == m, ridx, float(n)), axis=0, keepdims=True)
        top_v = jnp.where(slot == j, m, top_v)
        top_i = jnp.where(slot == j, idx, top_i)
        vals = jnp.where(ridx == idx, -jnp.inf, vals)
    return top_v, top_i


def _route_kernel(h_ref, wq_ref, k2_ref, i_ref, j_ref, g_ref, q3_ref, is_ref, js_ref, gs_ref, cand_ref, *, tm):
    q = _dot(h_ref[...], wq_ref[...]).astype(BF16)
    for h in range(PEER_HEADS):
        q3_ref[h] = q[:, h * LANES:(h + 1) * LANES]

    def head(h, carry):
        s = _dot_nt(k2_ref[h], q3_ref[h])
        s0, i0 = _topk_rows(s[:PEER_KEYS], PEER_TOPK)
        s1, i1 = _topk_rows(s[PEER_KEYS:], PEER_TOPK)
        r0 = 0
        for a in range(PEER_TOPK):
            nb = PEER_TOPK // (a + 1)
            cand_ref[r0:r0 + nb, :] = s0[a:a + 1, :] + s1[0:nb, :]
            r0 += nb
        cand_ref[r0:, :] = jnp.full((cand_ref.shape[0] - r0, tm), -jnp.inf, F32)
        top_s, top_r = _topk_rows(cand_ref[...], PEER_TOPK)
        pa = jnp.zeros((PEER_TOPK, tm), F32)
        pb = jnp.zeros((PEER_TOPK, tm), F32)
        r = 0
        for a in range(PEER_TOPK):
            for b in range(PEER_TOPK // (a + 1)):
                hit = top_r == r
                pa = jnp.where(hit, float(a), pa)
                pb = jnp.where(hit, float(b), pb)
                r += 1
        ii = jnp.zeros((PEER_TOPK, tm), F32)
        jj = jnp.zeros((PEER_TOPK, tm), F32)
        for a in range(PEER_TOPK):
            ii = jnp.where(pa == a, i0[a:a + 1, :], ii)
            jj = jnp.where(pb == a, i1[a:a + 1, :], jj)
        e = jnp.exp(top_s - jnp.max(top_s, axis=0, keepdims=True))
        gate = e / jnp.sum(e, axis=0, keepdims=True)
        rows = pl.ds(pl.multiple_of(h * PEER_TOPK, PEER_TOPK), PEER_TOPK)
        is_ref[rows, :] = ii
        js_ref[rows, :] = jj
        gs_ref[rows, :] = gate
        return carry
    lax.fori_loop(0, PEER_HEADS, head, 0)
    i_ref[...] = is_ref[...].T
    j_ref[...] = js_ref[...].T
    g_ref[...] = gs_ref[...].T


def _route(h2, wq, k2):
    t, d = h2.shape
    tm = 512
    row = lambda i: (i, 0)
    nk = PEER_HEADS * PEER_TOPK
    ncand = -(-sum(PEER_TOPK // (a + 1) for a in range(PEER_TOPK)) // 8) * 8
    return pl.pallas_call(
        functools.partial(_route_kernel, tm=tm),
        grid=(t // tm,),
        in_specs=[pl.BlockSpec((tm, d), row),
                  pl.BlockSpec(wq.shape, lambda i: (0, 0)),
                  pl.BlockSpec(k2.shape, lambda i: (0, 0, 0))],
        out_specs=[pl.BlockSpec((tm, nk), row)] * 3,
        out_shape=[jax.ShapeDtypeStruct((t, nk), F32)] * 3,
        scratch_shapes=[pltpu.VMEM((PEER_HEADS, tm, LANES), BF16),
                        pltpu.VMEM((nk, tm), F32), pltpu.VMEM((nk, tm), F32), pltpu.VMEM((nk, tm), F32),
                        pltpu.VMEM((ncand, tm), F32)],
        compiler_params=_cparams("arbitrary"),
        name="route",
    )(h2, wq, k2)


def _gelu_exact(a):
    return 0.5 * a * (1.0 + lax.erf(a * (2.0 ** -0.5)))


def _expert_kernel(h_ref, i_ref, j_ref, g_ref, u_ref, v_ref, x1_ref, g2_ref, fg_ref, o_ref,
                   w3_ref, y_ref, acc_ref, *, tm, nsub, final_norm):
    c = pl.program_id(1)
    half = PEER_KEYS // 2
    per_half = half // nsub
    pitch = tm + W3_PAD

    @pl.when(c == 0)
    def _():
        acc_ref[...] = jnp.zeros_like(acc_ref)

    @pl.when(c % per_half == 0)
    def _():
        i0 = (c // per_half) * half
        sub_i = (lax.broadcasted_iota(I32, (half, LANES), 0) + i0).astype(F32)
        sub_j = lax.broadcasted_iota(I32, (PEER_KEYS, LANES), 0).astype(F32)

        def group(tg, carry):
            r0 = pl.multiple_of(tg * GATE_GROUP, GATE_GROUP)
            ii = i_ref[pl.ds(r0, GATE_GROUP), :]
            jj = j_ref[pl.ds(r0, GATE_GROUP), :]
            gg = g_ref[pl.ds(r0, GATE_GROUP), :]
            for r in range(GATE_GROUP):
                a = (sub_i == ii[r:r + 1, :]).astype(BF16)
                rg = jnp.where(sub_j == jj[r:r + 1, :], gg[r:r + 1, :], 0.0).astype(BF16)
                w3_ref[pl.ds(r0 + r, half, stride=pitch), :] = _dot_nt(a, rg)
            return carry
        lax.fori_loop(0, tm // GATE_GROUP, group, 0)

    act = _dot_nt(h_ref[...], u_ref[...])
    for s in range(nsub):
        row_i = (c % per_half) * nsub + s
        w = w3_ref[pl.ds(pl.multiple_of(row_i * pitch, 8), tm), :]
        sl = slice(s * LANES, (s + 1) * LANES)
        y_ref[:, sl] = (w * _gelu_exact(act[:, sl])).astype(BF16)
    acc_ref[...] += _dot(y_ref[...], v_ref[...])

    @pl.when(c == pl.num_programs(1) - 1)
    def _():
        x2 = x1_ref[...] + g2_ref[...] * acc_ref[...]
        if final_norm:
            x2 = x2 * lax.rsqrt(jnp.mean(x2 * x2, axis=-1, keepdims=True) + NORM_EPS) * fg_ref[...]
        o_ref[...] = x2


def _experts(h2, ii, jj, gg, u, v, x1, g2, fg, seq, final_norm):
    t, d = h2.shape
    tm = 512
    nsub = 8
    ne = nsub * PEER_KEYS
    nb = seq // tm
    row = lambda i, c: (i, 0)
    chunk = lambda i, c: (c, 0)
    return pl.pallas_call(
        functools.partial(_expert_kernel, tm=tm, nsub=nsub, final_norm=final_norm),
        grid=(t // tm, PEER_KEYS // nsub),
        in_specs=[pl.BlockSpec((tm, d), row),
                  pl.BlockSpec((tm, LANES), row), pl.BlockSpec((tm, LANES), row), pl.BlockSpec((tm, LANES), row),
                  pl.BlockSpec((ne, d), chunk), pl.BlockSpec((ne, d), chunk),
                  pl.BlockSpec((tm, d), row),
                  pl.BlockSpec((None, 1, d), lambda i, c: (i // nb, 0, 0)),
                  pl.BlockSpec((1, d), lambda i, c: (0, 0))],
        out_specs=pl.BlockSpec((tm, d), row),
        out_shape=jax.ShapeDtypeStruct((t, d), F32),
        scratch_shapes=[pltpu.VMEM((PEER_KEYS // 2 * (tm + W3_PAD), LANES), F32),
                        pltpu.VMEM((tm, ne), BF16),
                        pltpu.VMEM((tm, d), F32)],
        compiler_params=_cparams("arbitrary", "arbitrary"),
        name="experts",
    )(h2, ii, jj, gg, u, v, x1, g2, fg)


def _dup(w):
    return jnp.concatenate([w, w], axis=1)


def _win_layout(w_in):
    nq, nkv = NSA_HEADS * HEAD_DIM, NSA_KV_HEADS * HEAD_DIM
    sizes = (nq, nkv, nkv, nkv, nkv, nkv, nkv, 3 * NSA_HEADS, DSA_HEADS * HEAD_DIM, DSA_KV_RANK,
             IDX_HEADS * IDX_DIM, IDX_DIM, IDX_HEADS, 2 * D_MODEL)
    offs = [0]
    for s in sizes:
        offs.append(offs[-1] + s)
    q_n, kc, vc, ks, vs, kw, vw, g_n, q_d, ckv, qi, ki, wi, g_m = [w_in[:, offs[k]:offs[k + 1]] for k in range(14)]
    dup2 = lambda w: jnp.concatenate([_dup(w[:, :HEAD_DIM]), _dup(w[:, HEAD_DIM:])], axis=1)
    misc = jnp.concatenate([g_n, wi, jnp.zeros((w_in.shape[0], LANES - 32), w_in.dtype)], axis=1)
    cols = [q_n, dup2(ks), dup2(kw), q_d, qi, _dup(ki), kc, dup2(vs), dup2(vw), vc, ckv, misc, g_m]
    return jnp.concatenate(cols, axis=1).astype(BF16)


def _rope_tables(seq):
    half = HEAD_DIM // 2
    pos = jnp.arange(seq, dtype=F32)
    inv = ROPE_THETA ** (-jnp.arange(half, dtype=F32) / half)
    ang = pos[:, None] * inv[None, :]
    cos, sin, zero = jnp.cos(ang), jnp.sin(ang), jnp.zeros((seq, half), F32)
    c = jnp.concatenate([cos, cos, cos, cos], axis=1)
    s1 = jnp.concatenate([-sin, zero, -sin, zero], axis=1)
    s2 = jnp.concatenate([zero, sin, zero, sin], axis=1)
    return c, s1, s2


def _cmp_weights(pos, w1, w2):
    g = NSA_KV_HEADS
    eye = jnp.eye(g, dtype=F32)
    big = jnp.einsum("lde,gh->lgdhe", w1, eye).reshape(CMP_BLOCK, g * HEAD_DIM, g * HEAD_DIM)
    wa = big[:CMP_STRIDE].reshape(CMP_STRIDE * g * HEAD_DIM, g * HEAD_DIM)
    wb = big[CMP_STRIDE:].reshape(CMP_STRIDE * g * HEAD_DIM, g * HEAD_DIM)
    z = jnp.zeros_like(w2)
    w2d = jnp.concatenate([jnp.concatenate([w2, w2, z, z], axis=1),
                           jnp.concatenate([z, z, w2, w2], axis=1)], axis=0)
    pos_t = jnp.broadcast_to(pos[:, None, :], (CMP_BLOCK, g, HEAD_DIM))
    pa = pos_t[:CMP_STRIDE].reshape(1, -1)
    pb = pos_t[CMP_STRIDE:].reshape(1, -1)
    return wa.astype(BF16), wb.astype(BF16), w2d.astype(BF16), pa, pb


def _selection_constants(seq):
    ncmp = seq // CMP_STRIDE
    nslc = seq // SLC_BLOCK
    per = SLC_BLOCK // CMP_STRIDE
    n = jnp.arange(ncmp)[:, None]
    j = jnp.arange(LANES)[None, :]
    mimp = ((n // per == j).astype(F32) + ((n + 1) // per == j).astype(F32)) * (j < nslc)
    mimp = mimp * (n < ncmp - 1)
    key = jnp.arange(seq)[None, :]
    eexp = (key // SLC_BLOCK == jnp.arange(LANES)[:, None]).astype(F32)
    return mimp.T.astype(BF16), eexp.astype(BF16)


def _subkey_blocks(subkeys):
    z = jnp.zeros_like(subkeys[:, 0])
    top = jnp.concatenate([subkeys[:, 0], z], axis=2)
    bot = jnp.concatenate([z, subkeys[:, 1]], axis=2)
    return jnp.concatenate([top, bot], axis=1).astype(BF16)


def kernel(x, c, ada_w, ada_b, norm1_g, w_in, cmp_pos, cmp_w1_k, cmp_w2_k, cmp_w1_v, cmp_w2_v,
           kv_norm_g, w_uk, w_uv, w_br_nsa, w_br_dsa, w_out, norm2_g,
           peer_wq, peer_subkeys, peer_u, peer_v, final_g):
    bsz, seq, d = x.shape
    depth = ada_w.shape[0]
    t = bsz * seq
    cos, s1, s2 = _rope_tables(seq)
    mimp, eexp = _selection_constants(seq)
    xt = x.reshape(t, d)
    out = xt
    for l in range(depth):
        mod = _ada(c, ada_w[l], ada_b[l]).reshape(bsz, 1, 6 * d)
        sh1, sc1, g1, sh2, sc2, g2 = [mod[:, :, k * d:(k + 1) * d] for k in range(6)]
        a, kc, vv, vc, mm, gm = _proj(xt, sc1, sh1, norm1_g[l].reshape(1, d), _win_layout(w_in[l]),
                                      cos, s1, s2, seq)
        a3 = a.reshape(bsz, seq, NA)
        m3 = mm.reshape(bsz, seq, NM)
        ncmp = seq // CMP_STRIDE
        wak, wbk, w2k, pa, pb = _cmp_weights(cmp_pos[l], cmp_w1_k[l], cmp_w2_k[l])
        wav, wbv, w2v, _, _ = _cmp_weights(cmp_pos[l], cmp_w1_v[l], cmp_w2_v[l])
        o_nsa = _nsa(a3, vv.reshape(bsz, seq, NV), kc.reshape(bsz, ncmp, CMP_STRIDE * 128),
                     vc.reshape(bsz, ncmp, CMP_STRIDE * 128), m3,
                     (wak, wbk, w2k, wav, wbv, w2v, pa, pb, mimp, eexp), seq)
        o_dsa = _dsa(a3, m3, kv_norm_g[l].reshape(1, DSA_KV_RANK), _dup(w_uk[l]).astype(BF16),
                     _dup(w_uv[l]).astype(BF16), cos, s1, s2, seq)
        x1, h2 = _merge(o_nsa.reshape(t, 512), o_dsa.reshape(t, 512), gm, xt, g1, sc2, sh2,
                        norm2_g[l].reshape(1, d), w_br_nsa[l].astype(BF16), w_br_dsa[l].astype(BF16),
                        w_out[l].astype(BF16), seq)
        ii, jj, gg = _route(h2, peer_wq[l].astype(BF16), _subkey_blocks(peer_subkeys[l]))
        out = _experts(h2, ii, jj, gg, peer_u[l].astype(BF16), peer_v[l].astype(BF16), x1, g2,
                       final_g.reshape(1, d), seq, final_norm=(l == depth - 1))
        xt = out
    return out.reshape(bsz, seq, d)
```

```python
import functools

import jax
import jax.numpy as jnp
from jax import lax
from jax.experimental import pallas as pl
from jax.experimental.pallas import tpu as pltpu

F32 = jnp.float32
BF16 = jnp.bfloat16
I32 = jnp.int32

D_MODEL = 1024
HEAD_DIM = 64
ROPE_THETA = 10000.0
NORM_EPS = 1e-6
NEG_INF = -1e30

NSA_HEADS = 8
NSA_KV_HEADS = 2
CMP_BLOCK = 32
CMP_STRIDE = 16
SLC_BLOCK = 64
SLC_TOPN = 16
WINDOW = 512
NSA_KEY_VARIANTS = 8

DSA_HEADS = 8
DSA_KV_RANK = 128
IDX_HEADS = 8
IDX_DIM = 64
DSA_TOPK = 256
DSA_KEY_VARIANTS = 8

PEER_HEADS = 8
PEER_KEYS = 128
PEER_HALF = 64
PEER_TOPK = 16

LANES = 128
W3_PAD = 8
GATE_GROUP = 64
VMEM_LIMIT = 56 * 1024 * 1024

NA = 2176
NV = 512
NM = 256
NG = 2 * D_MODEL
OFF_KC = NA
OFF_V = OFF_KC + 128
OFF_VC = OFF_V + NV
OFF_M = OFF_VC + 128
OFF_G = OFF_M + NM
NZ = OFF_G + NG
MISC_GATE = 0
MISC_WI = 24


def _dot(a, b):
    return jnp.dot(a, b, preferred_element_type=F32)


def _dot_nt(a, b):
    return lax.dot_general(a, b, (((1,), (1,)), ((), ())), preferred_element_type=F32)


def _cparams(*sem):
    return pltpu.CompilerParams(dimension_semantics=sem, vmem_limit_bytes=VMEM_LIMIT)


def _rope(z, c, s1, s2):
    return z * c + pltpu.roll(z, 96, 1) * s1 + pltpu.roll(z, 32, 1) * s2


def _col_reduce(x, op, reduce):
    n = x.shape[0]
    while n > 8 and (n // 8) % 2 == 0:
        n //= 2
        x = op(x[:n], x[n:])
    return reduce(x, axis=0, keepdims=True)


def _colsum(x):
    return _col_reduce(x, jnp.add, jnp.sum)


def _colmax(x):
    return _col_reduce(x, jnp.maximum, jnp.max)


def _half_mask(x, odd):
    lane = lax.broadcasted_iota(I32, x.shape, 1)
    keep = (lane >= HEAD_DIM) if odd else (lane < HEAD_DIM)
    return jnp.where(keep, x, jnp.zeros_like(x))


def _ada_kernel(c_ref, w_ref, b_ref, o_ref):
    c = c_ref[...]
    sc = c * jax.nn.sigmoid(c)
    o_ref[...] = _dot(sc.astype(BF16), w_ref[...].astype(BF16)) + b_ref[...]


def _ada(c, w, b):
    bsz, d = c.shape
    n = w.shape[1]
    tn = 1024
    return pl.pallas_call(
        _ada_kernel,
        grid=(n // tn,),
        in_specs=[pl.BlockSpec((bsz, d), lambda j: (0, 0)),
                  pl.BlockSpec((d, tn), lambda j: (0, j)),
                  pl.BlockSpec((1, tn), lambda j: (0, j))],
        out_specs=pl.BlockSpec((bsz, tn), lambda j: (0, j)),
        out_shape=jax.ShapeDtypeStruct((bsz, n), F32),
        compiler_params=_cparams("arbitrary"),
        name="ada",
    )(c, w, b.reshape(1, n))


def _proj_kernel(x_ref, sc_ref, sh_ref, g_ref, w_ref, cos_ref, s1_ref, s2_ref,
                 a_ref, kc_ref, v_ref, vc_ref, m_ref, gm_ref):
    x = x_ref[...]
    y = x * lax.rsqrt(jnp.mean(x * x, axis=-1, keepdims=True) + NORM_EPS) * g_ref[...]
    h = y * (1.0 + sc_ref[...]) + sh_ref[...]
    z = _dot(h.astype(BF16), w_ref[...])
    c, s1, s2 = cos_ref[...], s1_ref[...], s2_ref[...]
    for k in range(NA // LANES):
        sl = slice(k * LANES, (k + 1) * LANES)
        a_ref[:, sl] = _rope(z[:, sl], c, s1, s2).astype(BF16)
    kc_ref[...] = _rope(z[:, OFF_KC:OFF_KC + 128], c, s1, s2)
    v_ref[...] = z[:, OFF_V:OFF_V + NV].astype(BF16)
    vc_ref[...] = z[:, OFF_VC:OFF_VC + 128]
    m_ref[...] = z[:, OFF_M:OFF_M + NM]
    gm_ref[...] = z[:, OFF_G:OFF_G + NG]


def _proj(x2, sc, sh, g, w, cos, s1, s2, seq):
    t, d = x2.shape
    tm = 256
    nb = seq // tm
    row = lambda i: (i, 0)
    bat = lambda i: (i // nb, 0, 0)
    pos = lambda i: (i % nb, 0)
    full = lambda i: (0, 0)
    widths = (NA, 128, NV, 128, NM, NG)
    dtypes = (BF16, F32, BF16, F32, F32, F32)
    return pl.pallas_call(
        _proj_kernel,
        grid=(t // tm,),
        in_specs=[pl.BlockSpec((tm, d), row),
                  pl.BlockSpec((None, 1, d), bat),
                  pl.BlockSpec((None, 1, d), bat),
                  pl.BlockSpec((1, d), full),
                  pl.BlockSpec((d, NZ), full),
                  pl.BlockSpec((tm, LANES), pos),
                  pl.BlockSpec((tm, LANES), pos),
                  pl.BlockSpec((tm, LANES), pos)],
        out_specs=[pl.BlockSpec((tm, n), row) for n in widths],
        out_shape=[jax.ShapeDtypeStruct((t, n), dt) for n, dt in zip(widths, dtypes)],
        compiler_params=_cparams("arbitrary"),
        name="proj",
    )(x2, sc, sh, g, w, cos, s1, s2)


def _masked_attend(s, mask, v):
    s = jnp.where(mask, s, NEG_INF)
    m = jnp.max(s, axis=-1, keepdims=True)
    p = jnp.where(mask, jnp.exp(s - m), 0.0)
    den = jnp.maximum(jnp.sum(p, axis=-1, keepdims=True), 1e-30)
    return _dot(p.astype(BF16), v) / den, p, den


def _nsa_kernel(q_ref, ks_ref, kw_ref, vs_ref, vw_ref, kc_ref, vc_ref, misc_ref,
                wak_ref, wbk_ref, w2k_ref, wav_ref, wbv_ref, w2v_ref, posa_ref, posb_ref,
                mimp_ref, eexp_ref, o_ref, kcmp_ref, vcmp_ref, part_ref, *, tq, seq):
    i = pl.program_id(1)
    ncmp = seq // CMP_STRIDE
    scale = HEAD_DIM ** -0.5

    @pl.when(i == 0)
    def _():
        def compress(c_ref, wa_ref, wb_ref, w2_ref):
            c = c_ref[...]
            ha = _dot((c + posa_ref[...]).astype(BF16), wa_ref[...])
            hb = _dot((c + posb_ref[...]).astype(BF16), wb_ref[...])
            hid = ha + pltpu.roll(hb, ncmp - 1, 0)
            return _dot(jax.nn.gelu(hid).astype(BF16), w2_ref[...])
        kcmp_ref[...] = compress(kc_ref, wak_ref, wbk_ref, w2k_ref).astype(BF16)
        vcmp_ref[...] = compress(vc_ref, wav_ref, wbv_ref, w2v_ref).astype(BF16)

    t0 = i * tq
    trow = t0 + lax.broadcasted_iota(I32, (tq, LANES), 0)
    lane = lax.broadcasted_iota(I32, (tq, LANES), 1)
    sig = jax.nn.sigmoid(misc_ref[...])

    hpg = NSA_HEADS // NSA_KV_HEADS
    qs = [_half_mask(q_ref[:, (h // 2) * LANES:(h // 2 + 1) * LANES], h % 2) * scale for h in range(NSA_HEADS)]

    cmp_mask = (lane * CMP_STRIDE + (CMP_BLOCK - 1)) <= trow
    o_cmp, imp = [], []
    for h in range(NSA_HEADS):
        g = h // hpg
        s = _dot_nt(qs[h], kcmp_ref[:, g * LANES:(g + 1) * LANES])
        o, p, den = _masked_attend(s, cmp_mask, vcmp_ref[:, g * LANES:(g + 1) * LANES])
        o_cmp.append(o)
        pn = p / den
        if h % hpg == 0:
            imp.append(pn)
        else:
            imp[g] = imp[g] + pn

    chosen = []
    for g in range(NSA_KV_HEADS):
        pg = imp[g]
        p_hi = pg.astype(BF16)
        r1 = pg - p_hi.astype(F32)
        p_mid = r1.astype(BF16)
        p_lo = (r1 - p_mid.astype(F32)).astype(BF16)
        blk_t = (_dot_nt(mimp_ref[...], p_hi) + _dot_nt(mimp_ref[...], p_mid) + _dot_nt(mimp_ref[...], p_lo))
        nslc = seq // SLC_BLOCK
        jrow = lax.broadcasted_iota(I32, (nslc, tq), 0)
        tcol = t0 + lax.broadcasted_iota(I32, (nslc, tq), 1)
        cur = tcol // SLC_BLOCK
        forced = (jrow == 0) | (jrow == cur) | (jrow == cur - 1)
        admissible = jrow * SLC_BLOCK <= tcol
        val = jnp.where(forced, jnp.inf, jnp.where(admissible, blk_t[0:nslc, :], -jnp.inf))
        rank = jnp.zeros((nslc, tq), I32)
        for j in range(nslc):
            vj = val[j:j + 1, :]
            ahead = (vj > val) | ((vj == val) & (jrow > j))
            rank = rank + ahead.astype(I32)
        chosen_t = jnp.concatenate([(rank < SLC_TOPN).astype(F32), jnp.zeros((LANES - nslc, tq), F32)], axis=0)
        chosen.append(chosen_t.T.astype(BF16))

    def biased_attend(q, k, v, bias):
        s = _dot_nt(q, k) + bias
        p = jnp.exp(s - jnp.max(s, axis=-1, keepdims=True))
        den = jnp.maximum(jnp.sum(p, axis=-1, keepdims=True), 1e-30)
        return _dot(p.astype(BF16), v) / den

    kspan = WINDOW + tq
    kstart = pl.multiple_of(jnp.maximum(t0 - WINDOW, 0), tq)
    diff_w = (t0 + lax.broadcasted_iota(I32, (tq, kspan), 0)) - (kstart + lax.broadcasted_iota(I32, (tq, kspan), 1))
    win_bias = jnp.where((diff_w >= 0) & (diff_w < WINDOW), 0.0, NEG_INF)
    for h in range(NSA_HEADS):
        g = h // hpg
        kwd = kw_ref[pl.ds(kstart, kspan), g * LANES:(g + 1) * LANES]
        vwd = vw_ref[pl.ds(kstart, kspan), g * LANES:(g + 1) * LANES]
        o_w = biased_attend(qs[h], kwd, vwd, win_bias)
        c0 = MISC_GATE + 3 * h
        part_ref[h] = sig[:, c0:c0 + 1] * o_cmp[h] + sig[:, c0 + 2:c0 + 3] * o_w

    def selected_branch(nk):
        causal = lax.broadcasted_iota(I32, (tq, nk), 1) <= t0 + lax.broadcasted_iota(I32, (tq, nk), 0)
        pair = None
        for g in range(NSA_KV_HEADS):
            picked = _dot(chosen[g], eexp_ref[:, 0:nk]) > 0.5
            bias = jnp.where(picked & causal, 0.0, NEG_INF)
            ksd = ks_ref[0:nk, g * LANES:(g + 1) * LANES]
            vsd = vs_ref[0:nk, g * LANES:(g + 1) * LANES]
            for h in range(g * hpg, (g + 1) * hpg):
                c1 = MISC_GATE + 3 * h + 1
                o_h = part_ref[h] + sig[:, c1:c1 + 1] * biased_attend(qs[h], ksd, vsd, bias)
                if h % 2 == 0:
                    pair = o_h
                else:
                    o_ref[:, (h // 2) * LANES:(h // 2 + 1) * LANES] = (
                        jnp.where(lane < HEAD_DIM, pair, o_h).astype(BF16))

    step = seq // NSA_KEY_VARIANTS
    per = step // tq
    for v in range(NSA_KEY_VARIANTS):
        pl.when(i // per == v)(functools.partial(selected_branch, (v + 1) * step))


def _nsa(a3, v3, kcr, vcr, m3, wts, seq):
    bsz = a3.shape[0]
    tq = 128
    ncmp = seq // CMP_STRIDE
    per_b = lambda blk: (lambda b, i: (b, 0, blk))
    full2 = lambda b, i: (0, 0)
    w_specs = [pl.BlockSpec(w.shape, full2) for w in wts]
    return pl.pallas_call(
        functools.partial(_nsa_kernel, tq=tq, seq=seq),
        grid=(bsz, seq // tq),
        in_specs=[pl.BlockSpec((None, tq, 512), lambda b, i: (b, i, 0)),
                  pl.BlockSpec((None, seq, 256), per_b(2)),
                  pl.BlockSpec((None, seq, 256), per_b(3)),
                  pl.BlockSpec((None, seq, 256), per_b(0)),
                  pl.BlockSpec((None, seq, 256), per_b(1)),
                  pl.BlockSpec((None, ncmp, CMP_STRIDE * 128), per_b(0)),
                  pl.BlockSpec((None, ncmp, CMP_STRIDE * 128), per_b(0)),
                  pl.BlockSpec((None, tq, LANES), lambda b, i: (b, i, 1)),
                  ] + w_specs,
        out_specs=pl.BlockSpec((None, tq, 512), lambda b, i: (b, i, 0)),
        out_shape=jax.ShapeDtypeStruct((bsz, seq, 512), BF16),
        scratch_shapes=[pltpu.VMEM((ncmp, 256), BF16), pltpu.VMEM((ncmp, 256), BF16),
                        pltpu.VMEM((NSA_HEADS, tq, LANES), F32)],
        compiler_params=_cparams("arbitrary", "arbitrary"),
        name="nsa",
    )(a3, a3, a3, v3, v3, kcr, vcr, m3, *wts)


def _dsa_kernel(qd_ref, qi_ref, ki_ref, ckv_ref, misc_ref, gkv_ref, wuk_ref, wuv_ref,
                cos_ref, s1_ref, s2_ref, o_ref, kd_ref, vt_ref, *, tq, seq, topk):
    i = pl.program_id(1)

    @pl.when(i == 0)
    def _():
        ck = ckv_ref[...]
        cn = ck * lax.rsqrt(jnp.mean(ck * ck, axis=-1, keepdims=True) + NORM_EPS) * gkv_ref[...]
        cb = cn.astype(BF16)
        kd_ref[...] = _rope(_dot(cb, wuk_ref[...]), cos_ref[...], s1_ref[...], s2_ref[...]).astype(BF16)
        vt_ref[...] = _dot(cb, wuv_ref[...]).T.astype(BF16)

    t0 = i * tq
    misc_t = misc_ref[...].T
    int_min = jnp.int32(-2 ** 31)
    idx_bits = max(1, (seq - 1).bit_length()) + 1

    def attend(nk):
        kpos = lax.broadcasted_iota(I32, (nk, tq), 0)
        tpos = t0 + lax.broadcasted_iota(I32, (nk, tq), 1)
        causal = kpos <= tpos

        score = jnp.zeros((nk, tq), F32)
        for h in range(IDX_HEADS):
            qm = _half_mask(qi_ref[:, (h // 2) * LANES:(h // 2 + 1) * LANES], h % 2) * (IDX_DIM ** -0.5)
            logit = _dot_nt(ki_ref[0:nk, :], qm)
            w_h = misc_t[MISC_WI + h:MISC_WI + h + 1, :] * (IDX_HEADS ** -0.5)
            score = score + w_h * jnp.maximum(logit, 0.0)
        score = jnp.where(causal, score, -jnp.inf)

        def as_float(cu):
            ks = cu ^ int_min
            return lax.bitcast_convert_type(ks ^ ((ks >> 31) & jnp.int32(0x7FFFFFFF)), F32)

        def bit_step(it, tu):
            cu = tu | lax.shift_left(jnp.int32(1), 31 - it)
            cnt = _colsum((score >= as_float(cu)).astype(F32))
            below_neg_inf = (cu >= 0) & (cu < jnp.int32(0x007FFFFF))
            return jnp.where((cnt >= topk) | below_neg_inf, cu, tu)
        thr = as_float(lax.fori_loop(0, 32, bit_step, jnp.zeros((1, tq), I32)))
        above = score > thr
        need = topk - _colsum(above.astype(F32))
        tie = score == thr
        tie_pos = jnp.where(tie, kpos, jnp.int32(2 * seq))
        surplus = jnp.max(_colsum(tie.astype(F32)) - need)

        def lowest_ties():
            def tie_step(it, lim):
                cl = lim | lax.shift_left(jnp.int32(1), idx_bits - 1 - it)
                cnt = _colsum((tie_pos < cl).astype(F32))
                return jnp.where(cnt <= need, cl, lim)
            return lax.fori_loop(0, idx_bits, tie_step, jnp.zeros((1, tq), I32))

        lim = lax.cond(surplus > 0.0, lowest_ties, lambda: jnp.full((1, tq), 2 * seq, I32))
        sel = causal & (above | (tie_pos < lim))
        bias = jnp.where(sel, 0.0, NEG_INF)

        row = lax.broadcasted_iota(I32, (LANES, tq), 0)
        pair = None
        for h in range(DSA_HEADS):
            qm = _half_mask(qd_ref[:, (h // 2) * LANES:(h // 2 + 1) * LANES], h % 2) * (HEAD_DIM ** -0.5)
            s = _dot_nt(kd_ref[0:nk, :], qm) + bias
            p = jnp.exp(s - _colmax(s))
            den = jnp.maximum(_colsum(p), 1e-30)
            o_t = _dot(vt_ref[:, 0:nk], p.astype(BF16)) / den
            if h % 2 == 0:
                pair = o_t
            else:
                o_ref[:, (h // 2) * LANES:(h // 2 + 1) * LANES] = (
                    jnp.where(row < HEAD_DIM, pair, o_t).T.astype(BF16))

    step = seq // DSA_KEY_VARIANTS
    per = step // tq
    for v in range(DSA_KEY_VARIANTS):
        pl.when(i // per == v)(functools.partial(attend, (v + 1) * step))


def _dsa(a3, m3, gkv, wuk, wuv, cos, s1, s2, seq):
    bsz = a3.shape[0]
    tq = 128
    topk = min(DSA_TOPK, seq // 4)
    full2 = lambda b, i: (0, 0)
    return pl.pallas_call(
        functools.partial(_dsa_kernel, tq=tq, seq=seq, topk=topk),
        grid=(bsz, seq // tq),
        in_specs=[pl.BlockSpec((None, tq, 512), lambda b, i: (b, i, 2)),
                  pl.BlockSpec((None, tq, 512), lambda b, i: (b, i, 3)),
                  pl.BlockSpec((None, seq, LANES), lambda b, i: (b, 0, 16)),
                  pl.BlockSpec((None, seq, LANES), lambda b, i: (b, 0, 0)),
                  pl.BlockSpec((None, tq, LANES), lambda b, i: (b, i, 1)),
                  pl.BlockSpec((1, LANES), full2),
                  pl.BlockSpec((LANES, LANES), full2),
                  pl.BlockSpec((LANES, LANES), full2),
                  pl.BlockSpec((seq, LANES), full2),
                  pl.BlockSpec((seq, LANES), full2),
                  pl.BlockSpec((seq, LANES), full2)],
        out_specs=pl.BlockSpec((None, tq, 512), lambda b, i: (b, i, 0)),
        out_shape=jax.ShapeDtypeStruct((bsz, seq, 512), BF16),
        scratch_shapes=[pltpu.VMEM((seq, LANES), BF16), pltpu.VMEM((LANES, seq), BF16)],
        compiler_params=_cparams("arbitrary", "arbitrary"),
        name="dsa",
    )(a3, a3, a3, m3, m3, gkv, wuk, wuv, cos, s1, s2)


def _merge_kernel(on_ref, od_ref, gm_ref, x_ref, g1_ref, sc_ref, sh_ref, n2_ref,
                  wn_ref, wd_ref, wo_ref, x1_ref, h2_ref):
    gm = gm_ref[...]
    merged = (jax.nn.sigmoid(gm[:, :D_MODEL]) * _dot(on_ref[...], wn_ref[...])
              + jax.nn.sigmoid(gm[:, D_MODEL:]) * _dot(od_ref[...], wd_ref[...]))
    x1 = x_ref[...] + g1_ref[...] * _dot(merged.astype(BF16), wo_ref[...])
    x1_ref[...] = x1
    y = x1 * lax.rsqrt(jnp.mean(x1 * x1, axis=-1, keepdims=True) + NORM_EPS) * n2_ref[...]
    h2_ref[...] = (y * (1.0 + sc_ref[...]) + sh_ref[...]).astype(BF16)


def _merge(o_nsa, o_dsa, gm, x2, g1, sc2, sh2, n2, wn, wd, wo, seq):
    t, d = x2.shape
    tm = 256
    nb = seq // tm
    row = lambda i: (i, 0)
    bat = lambda i: (i // nb, 0, 0)
    full = lambda i: (0, 0)
    return pl.pallas_call(
        _merge_kernel,
        grid=(t // tm,),
        in_specs=[pl.BlockSpec((tm, 512), row), pl.BlockSpec((tm, 512), row),
                  pl.BlockSpec((tm, NG), row), pl.BlockSpec((tm, d), row),
                  pl.BlockSpec((None, 1, d), bat), pl.BlockSpec((None, 1, d), bat),
                  pl.BlockSpec((None, 1, d), bat), pl.BlockSpec((1, d), full),
                  pl.BlockSpec(wn.shape, full), pl.BlockSpec(wd.shape, full), pl.BlockSpec(wo.shape, full)],
        out_specs=[pl.BlockSpec((tm, d), row), pl.BlockSpec((tm, d), row)],
        out_shape=[jax.ShapeDtypeStruct((t, d), F32), jax.ShapeDtypeStruct((t, d), BF16)],
        compiler_params=_cparams("arbitrary"),
        name="merge",
    )(o_nsa, o_dsa, gm, x2, g1, sc2, sh2, n2, wn, wd, wo)


def _topk_rows(vals, k):
    n, t = vals.shape
    ridx = lax.broadcasted_iota(I32, (n, t), 0).astype(F32)
    slot = lax.broadcasted_iota(I32, (k, t), 0)
    top_v = jnp.zeros((k, t), F32)
    top_i = jnp.zeros((k, t), F32)
    for j in range(k):
        m = jnp.max(vals, axis=0, keepdims=True)
        idx = jnp.min(jnp.where(vals == m, ridx, float(n)), axis=0, keepdims=True)
        top_v = jnp.where(slot == j, m, top_v)
        top_i = jnp.where(slot == j, idx, top_i)
        vals = jnp.where(ridx == idx, -jnp.inf, vals)
    return top_v, top_i


def _route_kernel(h_ref, wq_ref, k2_ref, i_ref, j_ref, g_ref, q3_ref, is_ref, js_ref, gs_ref, cand_ref, *, tm):
    q = _dot(h_ref[...], wq_ref[...]).astype(BF16)
    for h in range(PEER_HEADS):
        q3_ref[h] = q[:, h * LANES:(h + 1) * LANES]

    def head(h, carry):
        s = _dot_nt(k2_ref[h], q3_ref[h])
        s0, i0 = _topk_rows(s[:PEER_KEYS], PEER_TOPK)
        s1, i1 = _topk_rows(s[PEER_KEYS:], PEER_TOPK)
        r0 = 0
        for a in range(PEER_TOPK):
            nb = PEER_TOPK // (a + 1)
            cand_ref[r0:r0 + nb, :] = s0[a:a + 1, :] + s1[0:nb, :]
            r0 += nb
        cand_ref[r0:, :] = jnp.full((cand_ref.shape[0] - r0, tm), -jnp.inf, F32)
        top_s, top_r = _topk_rows(cand_ref[...], PEER_TOPK)
        pa = jnp.zeros((PEER_TOPK, tm), F32)
        pb = jnp.zeros((PEER_TOPK, tm), F32)
        r = 0
        for a in range(PEER_TOPK):
            for b in range(PEER_TOPK // (a + 1)):
                hit = top_r == r
                pa = jnp.where(hit, float(a), pa)
                pb = jnp.where(hit, float(b), pb)
                r += 1
        ii = jnp.zeros((PEER_TOPK, tm), F32)
        jj = jnp.zeros((PEER_TOPK, tm), F32)
        for a in range(PEER_TOPK):
            ii = jnp.where(pa == a, i0[a:a + 1, :], ii)
            jj = jnp.where(pb == a, i1[a:a + 1, :], jj)
        e = jnp.exp(top_s - jnp.max(top_s, axis=0, keepdims=True))
        gate = e / jnp.sum(e, axis=0, keepdims=True)
        rows = pl.ds(pl.multiple_of(h * PEER_TOPK, PEER_TOPK), PEER_TOPK)
        is_ref[rows, :] = ii
        js_ref[rows, :] = jj
        gs_ref[rows, :] = gate
        return carry
    lax.fori_loop(0, PEER_HEADS, head, 0)
    i_ref[...] = is_ref[...].T
    j_ref[...] = js_ref[...].T
    g_ref[...] = gs_ref[...].T


def _route(h2, wq, k2):
    t, d = h2.shape
    tm = 512
    row = lambda i: (i, 0)
    nk = PEER_HEADS * PEER_TOPK
    ncand = -(-sum(PEER_TOPK // (a + 1) for a in range(PEER_TOPK)) // 8) * 8
    return pl.pallas_call(
        functools.partial(_route_kernel, tm=tm),
        grid=(t // tm,),
        in_specs=[pl.BlockSpec((tm, d), row),
                  pl.BlockSpec(wq.shape, lambda i: (0, 0)),
                  pl.BlockSpec(k2.shape, lambda i: (0, 0, 0))],
        out_specs=[pl.BlockSpec((tm, nk), row)] * 3,
        out_shape=[jax.ShapeDtypeStruct((t, nk), F32)] * 3,
        scratch_shapes=[pltpu.VMEM((PEER_HEADS, tm, LANES), BF16),
                        pltpu.VMEM((nk, tm), F32), pltpu.VMEM((nk, tm), F32), pltpu.VMEM((nk, tm), F32),
                        pltpu.VMEM((ncand, tm), F32)],
        compiler_params=_cparams("arbitrary"),
        name="route",
    )(h2, wq, k2)


def _gelu_exact(a):
    return 0.5 * a * (1.0 + lax.erf(a * (2.0 ** -0.5)))


def _expert_kernel(h_ref, i_ref, j_ref, g_ref, u_ref, v_ref, x1_ref, g2_ref, fg_ref, o_ref,
                   w3_ref, y_ref, acc_ref, *, tm, nsub, final_norm):
    c = pl.program_id(1)
    half = PEER_KEYS // 2
    per_half = half // nsub
    pitch = tm + W3_PAD

    @pl.when(c == 0)
    def _():
        acc_ref[...] = jnp.zeros_like(acc_ref)

    @pl.when(c % per_half == 0)
    def _():
        i0 = (c // per_half) * half
        sub_i = (lax.broadcasted_iota(I32, (half, LANES), 0) + i0).astype(F32)
        sub_j = lax.broadcasted_iota(I32, (PEER_KEYS, LANES), 0).astype(F32)

        def group(tg, carry):
            r0 = pl.multiple_of(tg * GATE_GROUP, GATE_GROUP)
            ii = i_ref[pl.ds(r0, GATE_GROUP), :]
            jj = j_ref[pl.ds(r0, GATE_GROUP), :]
            gg = g_ref[pl.ds(r0, GATE_GROUP), :]
            for r in range(GATE_GROUP):
                a = (sub_i == ii[r:r + 1, :]).astype(BF16)
                rg = jnp.where(sub_j == jj[r:r + 1, :], gg[r:r + 1, :], 0.0).astype(BF16)
                w3_ref[pl.ds(r0 + r, half, stride=pitch), :] = _dot_nt(a, rg)
            return carry
        lax.fori_loop(0, tm // GATE_GROUP, group, 0)

    act = _dot_nt(h_ref[...], u_ref[...])
    for s in range(nsub):
        row_i = (c % per_half) * nsub + s
        w = w3_ref[pl.ds(pl.multiple_of(row_i * pitch, 8), tm), :]
        sl = slice(s * LANES, (s + 1) * LANES)
        y_ref[:, sl] = (w * _gelu_exact(act[:, sl])).astype(BF16)
    acc_ref[...] += _dot(y_ref[...], v_ref[...])

    @pl.when(c == pl.num_programs(1) - 1)
    def _():
        x2 = x1_ref[...] + g2_ref[...] * acc_ref[...]
        if final_norm:
            x2 = x2 * lax.rsqrt(jnp.mean(x2 * x2, axis=-1, keepdims=True) + NORM_EPS) * fg_ref[...]
        o_ref[...] = x2


def _experts(h2, ii, jj, gg, u, v, x1, g2, fg, seq, final_norm):
    t, d = h2.shape
    tm = 512
    nsub = 8
    ne = nsub * PEER_KEYS
    nb = seq // tm
    row = lambda i, c: (i, 0)
    chunk = lambda i, c: (c, 0)
    return pl.pallas_call(
        functools.partial(_expert_kernel, tm=tm, nsub=nsub, final_norm=final_norm),
        grid=(t // tm, PEER_KEYS // nsub),
        in_specs=[pl.BlockSpec((tm, d), row),
                  pl.BlockSpec((tm, LANES), row), pl.BlockSpec((tm, LANES), row), pl.BlockSpec((tm, LANES), row),
                  pl.BlockSpec((ne, d), chunk), pl.BlockSpec((ne, d), chunk),
                  pl.BlockSpec((tm, d), row),
                  pl.BlockSpec((None, 1, d), lambda i, c: (i // nb, 0, 0)),
                  pl.BlockSpec((1, d), lambda i, c: (0, 0))],
        out_specs=pl.BlockSpec((tm, d), row),
        out_shape=jax.ShapeDtypeStruct((t, d), F32),
        scratch_shapes=[pltpu.VMEM((PEER_KEYS // 2 * (tm + W3_PAD), LANES), F32),
                        pltpu.VMEM((tm, ne), BF16),
                        pltpu.VMEM((tm, d), F32)],
        compiler_params=_cparams("arbitrary", "arbitrary"),
        name="experts",
    )(h2, ii, jj, gg, u, v, x1, g2, fg)


def _dup(w):
    return jnp.concatenate([w, w], axis=1)


def _win_layout(w_in):
    nq, nkv = NSA_HEADS * HEAD_DIM, NSA_KV_HEADS * HEAD_DIM
    sizes = (nq, nkv, nkv, nkv, nkv, nkv, nkv, 3 * NSA_HEADS, DSA_HEADS * HEAD_DIM, DSA_KV_RANK,
             IDX_HEADS * IDX_DIM, IDX_DIM, IDX_HEADS, 2 * D_MODEL)
    offs = [0]
    for s in sizes:
        offs.append(offs[-1] + s)
    q_n, kc, vc, ks, vs, kw, vw, g_n, q_d, ckv, qi, ki, wi, g_m = [w_in[:, offs[k]:offs[k + 1]] for k in range(14)]
    dup2 = lambda w: jnp.concatenate([_dup(w[:, :HEAD_DIM]), _dup(w[:, HEAD_DIM:])], axis=1)
    misc = jnp.concatenate([g_n, wi, jnp.zeros((w_in.shape[0], LANES - 32), w_in.dtype)], axis=1)
    cols = [q_n, dup2(ks), dup2(kw), q_d, qi, _dup(ki), kc, dup2(vs), dup2(vw), vc, ckv, misc, g_m]
    return jnp.concatenate(cols, axis=1).astype(BF16)


def _rope_tables(seq):
    half = HEAD_DIM // 2
    pos = jnp.arange(seq, dtype=F32)
    inv = ROPE_THETA ** (-jnp.arange(half, dtype=F32) / half)
    ang = pos[:, None] * inv[None, :]
    cos, sin, zero = jnp.cos(ang), jnp.sin(ang), jnp.zeros((seq, half), F32)
    c = jnp.concatenate([cos, cos, cos, cos], axis=1)
    s1 = jnp.concatenate([-sin, zero, -sin, zero], axis=1)
    s2 = jnp.concatenate([zero, sin, zero, sin], axis=1)
    return c, s1, s2


def _cmp_weights(pos, w1, w2):
    g = NSA_KV_HEADS
    eye = jnp.eye(g, dtype=F32)
    big = jnp.einsum("lde,gh->lgdhe", w1, eye).reshape(CMP_BLOCK, g * HEAD_DIM, g * HEAD_DIM)
    wa = big[:CMP_STRIDE].reshape(CMP_STRIDE * g * HEAD_DIM, g * HEAD_DIM)
    wb = big[CMP_STRIDE:].reshape(CMP_STRIDE * g * HEAD_DIM, g * HEAD_DIM)
    z = jnp.zeros_like(w2)
    w2d = jnp.concatenate([jnp.concatenate([w2, w2, z, z], axis=1),
                           jnp.concatenate([z, z, w2, w2], axis=1)], axis=0)
    pos_t = jnp.broadcast_to(pos[:, None, :], (CMP_BLOCK, g, HEAD_DIM))
    pa = pos_t[:CMP_STRIDE].reshape(1, -1)
    pb = pos_t[CMP_STRIDE:].reshape(1, -1)
    return wa.astype(BF16), wb.astype(BF16), w2d.astype(BF16), pa, pb


def _selection_constants(seq):
    ncmp = seq // CMP_STRIDE
    nslc = seq // SLC_BLOCK
    per = SLC_BLOCK // CMP_STRIDE
    n = jnp.arange(ncmp)[:, None]
    j = jnp.arange(LANES)[None, :]
    mimp = ((n // per == j).astype(F32) + ((n + 1) // per == j).astype(F32)) * (j < nslc)
    mimp = mimp * (n < ncmp - 1)
    key = jnp.arange(seq)[None, :]
    eexp = (key // SLC_BLOCK == jnp.arange(LANES)[:, None]).astype(F32)
    return mimp.T.astype(BF16), eexp.astype(BF16)


def _subkey_blocks(subkeys):
    z = jnp.zeros_like(subkeys[:, 0])
    top = jnp.concatenate([subkeys[:, 0], z], axis=2)
    bot = jnp.concatenate([z, subkeys[:, 1]], axis=2)
    return jnp.concatenate([top, bot], axis=1).astype(BF16)


def kernel(x, c, ada_w, ada_b, norm1_g, w_in, cmp_pos, cmp_w1_k, cmp_w2_k, cmp_w1_v, cmp_w2_v,
           kv_norm_g, w_uk, w_uv, w_br_nsa, w_br_dsa, w_out, norm2_g,
           peer_wq, peer_subkeys, peer_u, peer_v, final_g):
    bsz, seq, d = x.shape
    depth = ada_w.shape[0]
    t = bsz * seq
    cos, s1, s2 = _rope_tables(seq)
    mimp, eexp = _selection_constants(seq)
    xt = x.reshape(t, d)
    out = xt
    for l in range(depth):
        mod = _ada(c, ada_w[l], ada_b[l]).reshape(bsz, 1, 6 * d)
        sh1, sc1, g1, sh2, sc2, g2 = [mod[:, :, k * d:(k + 1) * d] for k in range(6)]
        a, kc, vv, vc, mm, gm = _proj(xt, sc1, sh1, norm1_g[l].reshape(1, d), _win_layout(w_in[l]),
                                      cos, s1, s2, seq)
        a3 = a.reshape(bsz, seq, NA)
        m3 = mm.reshape(bsz, seq, NM)
        ncmp = seq // CMP_STRIDE
        wak, wbk, w2k, pa, pb = _cmp_weights(cmp_pos[l], cmp_w1_k[l], cmp_w2_k[l])
        wav, wbv, w2v, _, _ = _cmp_weights(cmp_pos[l], cmp_w1_v[l], cmp_w2_v[l])
        o_nsa = _nsa(a3, vv.reshape(bsz, seq, NV), kc.reshape(bsz, ncmp, CMP_STRIDE * 128),
                     vc.reshape(bsz, ncmp, CMP_STRIDE * 128), m3,
                     (wak, wbk, w2k, wav, wbv, w2v, pa, pb, mimp, eexp), seq)
        o_dsa = _dsa(a3, m3, kv_norm_g[l].reshape(1, DSA_KV_RANK), _dup(w_uk[l]).astype(BF16),
                     _dup(w_uv[l]).astype(BF16), cos, s1, s2, seq)
        x1, h2 = _merge(o_nsa.reshape(t, 512), o_dsa.reshape(t, 512), gm, xt, g1, sc2, sh2,
                        norm2_g[l].reshape(1, d), w_br_nsa[l].astype(BF16), w_br_dsa[l].astype(BF16),
                        w_out[l].astype(BF16), seq)
        ii, jj, gg = _route(h2, peer_wq[l].astype(BF16), _subkey_blocks(peer_subkeys[l]))
        out = _experts(h2, ii, jj, gg, peer_u[l].astype(BF16), peer_v[l].astype(BF16), x1, g2,
                       final_g.reshape(1, d), seq, final_norm=(l == depth - 1))
        xt = out
    return out.reshape(bsz, seq, d)
```

```python
import functools

import jax
import jax.numpy as jnp
from jax import lax
from jax.experimental import pallas as pl
from jax.experimental.pallas import tpu as pltpu

F32 = jnp.float32
BF16 = jnp.bfloat16
I32 = jnp.int32

D_MODEL = 1024
HEAD_DIM = 64
ROPE_THETA = 10000.0
NORM_EPS = 1e-6
NEG_INF = -1e30

NSA_HEADS = 8
NSA_KV_HEADS = 2
CMP_BLOCK = 32
CMP_STRIDE = 16
SLC_BLOCK = 64
SLC_TOPN = 16
WINDOW = 512
NSA_KEY_VARIANTS = 8

DSA_HEADS = 8
DSA_KV_RANK = 128
IDX_HEADS = 8
IDX_DIM = 64
DSA_TOPK = 256
DSA_KEY_VARIANTS = 8

PEER_HEADS = 8
PEER_KEYS = 128
PEER_HALF = 64
PEER_TOPK = 16

LANES = 128
W3_PAD = 8
GATE_GROUP = 64
VMEM_LIMIT = 56 * 1024 * 1024

NA = 2176
NV = 512
NM = 256
NG = 2 * D_MODEL
OFF_KC = NA
OFF_V = OFF_KC + 128
OFF_VC = OFF_V + NV
OFF_M = OFF_VC + 128
OFF_G = OFF_M + NM
NZ = OFF_G + NG
MISC_GATE = 0
MISC_WI = 24


def _dot(a, b):
    return jnp.dot(a, b, preferred_element_type=F32)


def _dot_nt(a, b):
    return lax.dot_general(a, b, (((1,), (1,)), ((), ())), preferred_element_type=F32)


def _cparams(*sem):
    return pltpu.CompilerParams(dimension_semantics=sem, vmem_limit_bytes=VMEM_LIMIT)


def _rope(z, c, s1, s2):
    return z * c + pltpu.roll(z, 96, 1) * s1 + pltpu.roll(z, 32, 1) * s2


def _col_reduce(x, op, reduce):
    n = x.shape[0]
    while n > 8 and (n // 8) % 2 == 0:
        n //= 2
        x = op(x[:n], x[n:])
    return reduce(x, axis=0, keepdims=True)


def _colsum(x):
    return _col_reduce(x, jnp.add, jnp.sum)


def _colmax(x):
    return _col_reduce(x, jnp.maximum, jnp.max)


def _half_mask(x, odd):
    lane = lax.broadcasted_iota(I32, x.shape, 1)
    keep = (lane >= HEAD_DIM) if odd else (lane < HEAD_DIM)
    return jnp.where(keep, x, jnp.zeros_like(x))


def _ada_kernel(c_ref, w_ref, b_ref, o_ref):
    c = c_ref[...]
    sc = c * jax.nn.sigmoid(c)
    o_ref[...] = _dot(sc.astype(BF16), w_ref[...].astype(BF16)) + b_ref[...]


def _ada(c, w, b):
    bsz, d = c.shape
    n = w.shape[1]
    tn = 1024
    return pl.pallas_call(
        _ada_kernel,
        grid=(n // tn,),
        in_specs=[pl.BlockSpec((bsz, d), lambda j: (0, 0)),
                  pl.BlockSpec((d, tn), lambda j: (0, j)),
                  pl.BlockSpec((1, tn), lambda j: (0, j))],
        out_specs=pl.BlockSpec((bsz, tn), lambda j: (0, j)),
        out_shape=jax.ShapeDtypeStruct((bsz, n), F32),
        compiler_params=_cparams("arbitrary"),
        name="ada",
    )(c, w, b.reshape(1, n))


def _proj_kernel(x_ref, sc_ref, sh_ref, g_ref, w_ref, cos_ref, s1_ref, s2_ref,
                 a_ref, kc_ref, v_ref, vc_ref, m_ref, gm_ref):
    x = x_ref[...]
    y = x * lax.rsqrt(jnp.mean(x * x, axis=-1, keepdims=True) + NORM_EPS) * g_ref[...]
    h = y * (1.0 + sc_ref[...]) + sh_ref[...]
    z = _dot(h.astype(BF16), w_ref[...])
    c, s1, s2 = cos_ref[...], s1_ref[...], s2_ref[...]
    for k in range(NA // LANES):
        sl = slice(k * LANES, (k + 1) * LANES)
        a_ref[:, sl] = _rope(z[:, sl], c, s1, s2).astype(BF16)
    kc_ref[...] = _rope(z[:, OFF_KC:OFF_KC + 128], c, s1, s2)
    v_ref[...] = z[:, OFF_V:OFF_V + NV].astype(BF16)
    vc_ref[...] = z[:, OFF_VC:OFF_VC + 128]
    m_ref[...] = z[:, OFF_M:OFF_M + NM]
    gm_ref[...] = z[:, OFF_G:OFF_G + NG]


def _proj(x2, sc, sh, g, w, cos, s1, s2, seq):
    t, d = x2.shape
    tm = 256
    nb = seq // tm
    row = lambda i: (i, 0)
    bat = lambda i: (i // nb, 0, 0)
    pos = lambda i: (i % nb, 0)
    full = lambda i: (0, 0)
    widths = (NA, 128, NV, 128, NM, NG)
    dtypes = (BF16, F32, BF16, F32, F32, F32)
    return pl.pallas_call(
        _proj_kernel,
        grid=(t // tm,),
        in_specs=[pl.BlockSpec((tm, d), row),
                  pl.BlockSpec((None, 1, d), bat),
                  pl.BlockSpec((None, 1, d), bat),
                  pl.BlockSpec((1, d), full),
                  pl.BlockSpec((d, NZ), full),
                  pl.BlockSpec((tm, LANES), pos),
                  pl.BlockSpec((tm, LANES), pos),
                  pl.BlockSpec((tm, LANES), pos)],
        out_specs=[pl.BlockSpec((tm, n), row) for n in widths],
        out_shape=[jax.ShapeDtypeStruct((t, n), dt) for n, dt in zip(widths, dtypes)],
        compiler_params=_cparams("arbitrary"),
        name="proj",
    )(x2, sc, sh, g, w, cos, s1, s2)


def _masked_attend(s, mask, v):
    s = jnp.where(mask, s, NEG_INF)
    m = jnp.max(s, axis=-1, keepdims=True)
    p = jnp.where(mask, jnp.exp(s - m), 0.0)
    den = jnp.maximum(jnp.sum(p, axis=-1, keepdims=True), 1e-30)
    return _dot(p.astype(BF16), v) / den, p, den


def _nsa_kernel(q_ref, ks_ref, kw_ref, vs_ref, vw_ref, kc_ref, vc_ref, misc_ref,
                wak_ref, wbk_ref, w2k_ref, wav_ref, wbv_ref, w2v_ref, posa_ref, posb_ref,
                mimp_ref, eexp_ref, o_ref, kcmp_ref, vcmp_ref, part_ref, *, tq, seq):
    i = pl.program_id(1)
    ncmp = seq // CMP_STRIDE
    scale = HEAD_DIM ** -0.5

    @pl.when(i == 0)
    def _():
        def compress(c_ref, wa_ref, wb_ref, w2_ref):
            c = c_ref[...]
            ha = _dot((c + posa_ref[...]).astype(BF16), wa_ref[...])
            hb = _dot((c + posb_ref[...]).astype(BF16), wb_ref[...])
            hid = ha + pltpu.roll(hb, ncmp - 1, 0)
            return _dot(jax.nn.gelu(hid).astype(BF16), w2_ref[...])
        kcmp_ref[...] = compress(kc_ref, wak_ref, wbk_ref, w2k_ref).astype(BF16)
        vcmp_ref[...] = compress(vc_ref, wav_ref, wbv_ref, w2v_ref).astype(BF16)

    t0 = i * tq
    trow = t0 + lax.broadcasted_iota(I32, (tq, LANES), 0)
    lane = lax.broadcasted_iota(I32, (tq, LANES), 1)
    sig = jax.nn.sigmoid(misc_ref[...])

    hpg = NSA_HEADS // NSA_KV_HEADS
    qs = [_half_mask(q_ref[:, (h // 2) * LANES:(h // 2 + 1) * LANES], h % 2) * scale for h in range(NSA_HEADS)]

    cmp_mask = (lane * CMP_STRIDE + (CMP_BLOCK - 1)) <= trow
    o_cmp, imp = [], []
    for h in range(NSA_HEADS):
        g = h // hpg
        s = _dot_nt(qs[h], kcmp_ref[:, g * LANES:(g + 1) * LANES])
        o, p, den = _masked_attend(s, cmp_mask, vcmp_ref[:, g * LANES:(g + 1) * LANES])
        o_cmp.append(o)
        pn = p / den
        if h % hpg == 0:
            imp.append(pn)
        else:
            imp[g] = imp[g] + pn

    chosen = []
    for g in range(NSA_KV_HEADS):
        pg = imp[g]
        p_hi = pg.astype(BF16)
        r1 = pg - p_hi.astype(F32)
        p_mid = r1.astype(BF16)
        p_lo = (r1 - p_mid.astype(F32)).astype(BF16)
        blk_t = (_dot_nt(mimp_ref[...], p_hi) + _dot_nt(mimp_ref[...], p_mid) + _dot_nt(mimp_ref[...], p_lo))
        nslc = seq // SLC_BLOCK
        jrow = lax.broadcasted_iota(I32, (nslc, tq), 0)
        tcol = t0 + lax.broadcasted_iota(I32, (nslc, tq), 1)
        cur = tcol // SLC_BLOCK
        forced = (jrow == 0) | (jrow == cur) | (jrow == cur - 1)
        admissible = jrow * SLC_BLOCK <= tcol
        val = jnp.where(forced, jnp.inf, jnp.where(admissible, blk_t[0:nslc, :], -jnp.inf))
        rank = jnp.zeros((nslc, tq), I32)
        for j in range(nslc):
            vj = val[j:j + 1, :]
            ahead = (vj > val) | ((vj == val) & (jrow > j))
            rank = rank + ahead.astype(I32)
        chosen_t = jnp.concatenate([(rank < SLC_TOPN).astype(F32), jnp.zeros((LANES - nslc, tq), F32)], axis=0)
        chosen.append(chosen_t.T.astype(BF16))

    def biased_attend(q, k, v, bias):
        s = _dot_nt(q, k) + bias
        p = jnp.exp(s - jnp.max(s, axis=-1, keepdims=True))
        den = jnp.maximum(jnp.sum(p, axis=-1, keepdims=True), 1e-30)
        return _dot(p.astype(BF16), v) / den

    kspan = WINDOW + tq
    kstart = pl.multiple_of(jnp.maximum(t0 - WINDOW, 0), tq)
    diff_w = (t0 + lax.broadcasted_iota(I32, (tq, kspan), 0)) - (kstart + lax.broadcasted_iota(I32, (tq, kspan), 1))
    win_bias = jnp.where((diff_w >= 0) & (diff_w < WINDOW), 0.0, NEG_INF)
    for h in range(NSA_HEADS):
        g = h // hpg
        kwd = kw_ref[pl.ds(kstart, kspan), g * LANES:(g + 1) * LANES]
        vwd = vw_ref[pl.ds(kstart, kspan), g * LANES:(g + 1) * LANES]
        o_w = biased_attend(qs[h], kwd, vwd, win_bias)
        c0 = MISC_GATE + 3 * h
        part_ref[h] = sig[:, c0:c0 + 1] * o_cmp[h] + sig[:, c0 + 2:c0 + 3] * o_w

    def selected_branch(nk):
        causal = lax.broadcasted_iota(I32, (tq, nk), 1) <= t0 + lax.broadcasted_iota(I32, (tq, nk), 0)
        pair = None
        for g in range(NSA_KV_HEADS):
            picked = _dot(chosen[g], eexp_ref[:, 0:nk]) > 0.5
            bias = jnp.where(picked & causal, 0.0, NEG_INF)
            ksd = ks_ref[0:nk, g * LANES:(g + 1) * LANES]
            vsd = vs_ref[0:nk, g * LANES:(g + 1) * LANES]
            for h in range(g * hpg, (g + 1) * hpg):
                c1 = MISC_GATE + 3 * h + 1
                o_h = part_ref[h] + sig[:, c1:c1 + 1] * biased_attend(qs[h], ksd, vsd, bias)
                if h % 2 == 0:
                    pair = o_h
                else:
                    o_ref[:, (h // 2) * LANES:(h // 2 + 1) * LANES] = (
                        jnp.where(lane < HEAD_DIM, pair, o_h).astype(BF16))

    step = seq // NSA_KEY_VARIANTS
    per = step // tq
    for v in range(NSA_KEY_VARIANTS):
        pl.when(i // per == v)(functools.partial(selected_branch, (v + 1) * step))


def _nsa(a3, v3, kcr, vcr, m3, wts, seq):
    bsz = a3.shape[0]
    tq = 128
    ncmp = seq // CMP_STRIDE
    per_b = lambda blk: (lambda b, i: (b, 0, blk))
    full2 = lambda b, i: (0, 0)
    w_specs = [pl.BlockSpec(w.shape, full2) for w in wts]
    return pl.pallas_call(
        functools.partial(_nsa_kernel, tq=tq, seq=seq),
        grid=(bsz, seq // tq),
        in_specs=[pl.BlockSpec((None, tq, 512), lambda b, i: (b, i, 0)),
                  pl.BlockSpec((None, seq, 256), per_b(2)),
                  pl.BlockSpec((None, seq, 256), per_b(3)),
                  pl.BlockSpec((None, seq, 256), per_b(0)),
                  pl.BlockSpec((None, seq, 256), per_b(1)),
                  pl.BlockSpec((None, ncmp, CMP_STRIDE * 128), per_b(0)),
                  pl.BlockSpec((None, ncmp, CMP_STRIDE * 128), per_b(0)),
                  pl.BlockSpec((None, tq, LANES), lambda b, i: (b, i, 1)),
                  ] + w_specs,
        out_specs=pl.BlockSpec((None, tq, 512), lambda b, i: (b, i, 0)),
        out_shape=jax.ShapeDtypeStruct((bsz, seq, 512), BF16),
        scratch_shapes=[pltpu.VMEM((ncmp, 256), BF16), pltpu.VMEM((ncmp, 256), BF16),
                        pltpu.VMEM((NSA_HEADS, tq, LANES), F32)],
        compiler_params=_cparams("arbitrary", "arbitrary"),
        name="nsa",
    )(a3, a3, a3, v3, v3, kcr, vcr, m3, *wts)


def _dsa_kernel(qd_ref, qi_ref, ki_ref, ckv_ref, misc_ref, gkv_ref, wuk_ref, wuv_ref,
                cos_ref, s1_ref, s2_ref, o_ref, kd_ref, vt_ref, *, tq, seq, topk):
    i = pl.program_id(1)

    @pl.when(i == 0)
    def _():
        ck = ckv_ref[...]
        cn = ck * lax.rsqrt(jnp.mean(ck * ck, axis=-1, keepdims=True) + NORM_EPS) * gkv_ref[...]
        cb = cn.astype(BF16)
        kd_ref[...] = _rope(_dot(cb, wuk_ref[...]), cos_ref[...], s1_ref[...], s2_ref[...]).astype(BF16)
        vt_ref[...] = _dot(cb, wuv_ref[...]).T.astype(BF16)

    t0 = i * tq
    misc_t = misc_ref[...].T
    int_min = jnp.int32(-2 ** 31)
    idx_bits = max(1, (seq - 1).bit_length()) + 1

    def attend(nk):
        kpos = lax.broadcasted_iota(I32, (nk, tq), 0)
        tpos = t0 + lax.broadcasted_iota(I32, (nk, tq), 1)
        causal = kpos <= tpos

        score = jnp.zeros((nk, tq), F32)
        for h in range(IDX_HEADS):
            qm = _half_mask(qi_ref[:, (h // 2) * LANES:(h // 2 + 1) * LANES], h % 2) * (IDX_DIM ** -0.5)
            logit = _dot_nt(ki_ref[0:nk, :], qm)
            w_h = misc_t[MISC_WI + h:MISC_WI + h + 1, :] * (IDX_HEADS ** -0.5)
            score = score + w_h * jnp.maximum(logit, 0.0)
        score = jnp.where(causal, score, -jnp.inf)

        def as_float(cu):
            ks = cu ^ int_min
            return lax.bitcast_convert_type(ks ^ ((ks >> 31) & jnp.int32(0x7FFFFFFF)), F32)

        def bit_step(it, tu):
            cu = tu | lax.shift_left(jnp.int32(1), 31 - it)
            cnt = _colsum((score >= as_float(cu)).astype(F32))
            below_neg_inf = (cu >= 0) & (cu < jnp.int32(0x007FFFFF))
            return jnp.where((cnt >= topk) | below_neg_inf, cu, tu)
        thr = as_float(lax.fori_loop(0, 32, bit_step, jnp.zeros((1, tq), I32)))
        above = score > thr
        need = topk - _colsum(above.astype(F32))
        tie = score == thr
        tie_pos = jnp.where(tie, kpos, jnp.int32(2 * seq))
        surplus = jnp.max(_colsum(tie.astype(F32)) - need)

        def lowest_ties():
            def tie_step(it, lim):
                cl = lim | lax.shift_left(jnp.int32(1), idx_bits - 1 - it)
                cnt = _colsum((tie_pos < cl).astype(F32))
                return jnp.where(cnt <= need, cl, lim)
            return lax.fori_loop(0, idx_bits, tie_step, jnp.zeros((1, tq), I32))

        lim = lax.cond(surplus > 0.0, lowest_ties, lambda: jnp.full((1, tq), 2 * seq, I32))
        sel = causal & (above | (tie_pos < lim))
        bias = jnp.where(sel, 0.0, NEG_INF)

        row = lax.broadcasted_iota(I32, (LANES, tq), 0)
        pair = None
        for h in range(DSA_HEADS):
            qm = _half_mask(qd_ref[:, (h // 2) * LANES:(h // 2 + 1) * LANES], h % 2) * (HEAD_DIM ** -0.5)
            s = _dot_nt(kd_ref[0:nk, :], qm) + bias
            p = jnp.exp(s - _colmax(s))
            den = jnp.maximum(_colsum(p), 1e-30)
            o_t = _dot(vt_ref[:, 0:nk], p.astype(BF16)) / den
            if h % 2 == 0:
                pair = o_t
            else:
                o_ref[:, (h // 2) * LANES:(h // 2 + 1) * LANES] = (
                    jnp.where(row < HEAD_DIM, pair, o_t).T.astype(BF16))

    step = seq // DSA_KEY_VARIANTS
    per = step // tq
    for v in range(DSA_KEY_VARIANTS):
        pl.when(i // per == v)(functools.partial(attend, (v + 1) * step))


def _dsa(a3, m3, gkv, wuk, wuv, cos, s1, s2, seq):
    bsz = a3.shape[0]
    tq = 128
    topk = min(DSA_TOPK, seq // 4)
    full2 = lambda b, i: (0, 0)
    return pl.pallas_call(
        functools.partial(_dsa_kernel, tq=tq, seq=seq, topk=topk),
        grid=(bsz, seq // tq),
        in_specs=[pl.BlockSpec((None, tq, 512), lambda b, i: (b, i, 2)),
                  pl.BlockSpec((None, tq, 512), lambda b, i: (b, i, 3)),
                  pl.BlockSpec((None, seq, LANES), lambda b, i: (b, 0, 16)),
                  pl.BlockSpec((None, seq, LANES), lambda b, i: (b, 0, 0)),
                  pl.BlockSpec((None, tq, LANES), lambda b, i: (b, i, 1)),
                  pl.BlockSpec((1, LANES), full2),
                  pl.BlockSpec((LANES, LANES), full2),
                  pl.BlockSpec((LANES, LANES), full2),
                  pl.BlockSpec((seq, LANES), full2),
                  pl.BlockSpec((seq, LANES), full2),
                  pl.BlockSpec((seq, LANES), full2)],
        out_specs=pl.BlockSpec((None, tq, 512), lambda b, i: (b, i, 0)),
        out_shape=jax.ShapeDtypeStruct((bsz, seq, 512), BF16),
        scratch_shapes=[pltpu.VMEM((seq, LANES), BF16), pltpu.VMEM((LANES, seq), BF16)],
        compiler_params=_cparams("arbitrary", "arbitrary"),
        name="dsa",
    )(a3, a3, a3, m3, m3, gkv, wuk, wuv, cos, s1, s2)


def _merge_kernel(on_ref, od_ref, gm_ref, x_ref, g1_ref, sc_ref, sh_ref, n2_ref,
                  wn_ref, wd_ref, wo_ref, x1_ref, h2_ref):
    gm = gm_ref[...]
    merged = (jax.nn.sigmoid(gm[:, :D_MODEL]) * _dot(on_ref[...], wn_ref[...])
              + jax.nn.sigmoid(gm[:, D_MODEL:]) * _dot(od_ref[...], wd_ref[...]))
    x1 = x_ref[...] + g1_ref[...] * _dot(merged.astype(BF16), wo_ref[...])
    x1_ref[...] = x1
    y = x1 * lax.rsqrt(jnp.mean(x1 * x1, axis=-1, keepdims=True) + NORM_EPS) * n2_ref[...]
    h2_ref[...] = (y * (1.0 + sc_ref[...]) + sh_ref[...]).astype(BF16)


def _merge(o_nsa, o_dsa, gm, x2, g1, sc2, sh2, n2, wn, wd, wo, seq):
    t, d = x2.shape
    tm = 256
    nb = seq // tm
    row = lambda i: (i, 0)
    bat = lambda i: (i // nb, 0, 0)
    full = lambda i: (0, 0)
    return pl.pallas_call(
        _merge_kernel,
        grid=(t // tm,),
        in_specs=[pl.BlockSpec((tm, 512), row), pl.BlockSpec((tm, 512), row),
                  pl.BlockSpec((tm, NG), row), pl.BlockSpec((tm, d), row),
                  pl.BlockSpec((None, 1, d), bat), pl.BlockSpec((None, 1, d), bat),
                  pl.BlockSpec((None, 1, d), bat), pl.BlockSpec((1, d), full),
                  pl.BlockSpec(wn.shape, full), pl.BlockSpec(wd.shape, full), pl.BlockSpec(wo.shape, full)],
        out_specs=[pl.BlockSpec((tm, d), row), pl.BlockSpec((tm, d), row)],
        out_shape=[jax.ShapeDtypeStruct((t, d), F32), jax.ShapeDtypeStruct((t, d), BF16)],
        compiler_params=_cparams("arbitrary"),
        name="merge",
    )(o_nsa, o_dsa, gm, x2, g1, sc2, sh2, n2, wn, wd, wo)


def _topk_rows(vals, k):
    n, t = vals.shape
    ridx = lax.broadcasted_iota(I32, (n, t), 0).astype(F32)
    slot = lax.broadcasted_iota(I32, (k, t), 0)
    top_v = jnp.zeros((k, t), F32)
    top_i = jnp.zeros((k, t), F32)
    for j in range(k):
        m = jnp.max(vals, axis=0, keepdims=True)
        idx = jnp.min(jnp.where(vals == m, ridx, float(n)), axis=0, keepdims=True)
        top_v = jnp.where(slot == j, m, top_v)
        top_i = jnp.where(slot == j, idx, top_i)
        vals = jnp.where(ridx == idx, -jnp.inf, vals)
    return top_v, top_i


N_CAND = -(-sum(PEER_TOPK // (a + 1) for a in range(PEER_TOPK)) // 8) * 8


def _route_head(s, cand_ref):
    t = s.shape[1]
    s0, i0 = _topk_rows(s[:PEER_KEYS], PEER_TOPK)
    s1, i1 = _topk_rows(s[PEER_KEYS:], PEER_TOPK)
    r0 = 0
    for a in range(PEER_TOPK):
        nb = PEER_TOPK // (a + 1)
        cand_ref[r0:r0 + nb, :] = s0[a:a + 1, :] + s1[0:nb, :]
        r0 += nb
    cand_ref[r0:, :] = jnp.full((N_CAND - r0, t), -jnp.inf, F32)
    top_s, top_r = _topk_rows(cand_ref[...], PEER_TOPK)
    pa = jnp.zeros((PEER_TOPK, t), F32)
    pb = jnp.zeros((PEER_TOPK, t), F32)
    r = 0
    for a in range(PEER_TOPK):
        for b in range(PEER_TOPK // (a + 1)):
            hit = top_r == r
            pa = jnp.where(hit, float(a), pa)
            pb = jnp.where(hit, float(b), pb)
            r += 1
    ii = jnp.zeros((PEER_TOPK, t), F32)
    jj = jnp.zeros((PEER_TOPK, t), F32)
    for a in range(PEER_TOPK):
        ii = jnp.where(pa == a, i0[a:a + 1, :], ii)
        jj = jnp.where(pb == a, i1[a:a + 1, :], jj)
    e = jnp.exp(top_s - jnp.max(top_s, axis=0, keepdims=True))
    return ii, jj, e / jnp.sum(e, axis=0, keepdims=True)


def _gelu_exact(a):
    return 0.5 * a * (1.0 + lax.erf(a * (2.0 ** -0.5)))


def _peer_kernel(h_ref, hn_ref, wq_ref, k2_ref, u_ref, v_ref, x1_ref, g2_ref, fg_ref, o_ref,
                 w3_ref, y_ref, acc_ref, q3_ref, rt_ref, cur_ref, cand_ref, *, tm, nsub, final_norm):
    i = pl.program_id(0)
    c = pl.program_id(1)
    half = PEER_KEYS // 2
    per_half = half // nsub
    pitch = tm + W3_PAD
    th = tm // 2
    groups = tm // GATE_GROUP

    def project_queries(x_ref):
        q = _dot(x_ref[...], wq_ref[...]).astype(BF16)
        for h in range(PEER_HEADS):
            q3_ref[h] = q[:, h * LANES:(h + 1) * LANES]

    def route_unit(unit):
        h = unit // 2
        part = unit % 2
        qh = q3_ref[h, pl.ds(pl.multiple_of(part * th, th), th), :]
        ii, jj, gate = _route_head(_dot_nt(k2_ref[h], qh), cand_ref)
        rows = pl.ds(pl.multiple_of(h * PEER_TOPK, PEER_TOPK), PEER_TOPK)
        rt_ref[0, part, rows, :] = ii
        rt_ref[1, part, rows, :] = jj
        rt_ref[2, part, rows, :] = gate

    @pl.when(c == 0)
    def _():
        acc_ref[...] = jnp.zeros_like(acc_ref)

        @pl.when(i == 0)
        def _():
            project_queries(h_ref)
            lax.fori_loop(0, 2 * PEER_HEADS, lambda unit, carry: (route_unit(unit), carry)[1], 0)

        for arr in range(3):
            for part in range(2):
                cur_ref[arr, part * th:(part + 1) * th, :] = rt_ref[arr, part].T
        project_queries(hn_ref)

    @pl.when(c % per_half == 0)
    def _():
        i0 = (c // per_half) * half
        sub_i = (lax.broadcasted_iota(I32, (half, LANES), 0) + i0).astype(F32)
        sub_j = lax.broadcasted_iota(I32, (PEER_KEYS, LANES), 0).astype(F32)

        def group(tg, carry):
            route_unit((c // per_half) * groups + tg)
            r0 = pl.multiple_of(tg * GATE_GROUP, GATE_GROUP)
            ii = cur_ref[0, pl.ds(r0, GATE_GROUP), :]
            jj = cur_ref[1, pl.ds(r0, GATE_GROUP), :]
            gg = cur_ref[2, pl.ds(r0, GATE_GROUP), :]
            for r in range(GATE_GROUP):
                a = (sub_i == ii[r:r + 1, :]).astype(BF16)
                rg = jnp.where(sub_j == jj[r:r + 1, :], gg[r:r + 1, :], 0.0).astype(BF16)
                w3_ref[pl.ds(r0 + r, half, stride=pitch), :] = _dot_nt(a, rg)
            return carry
        lax.fori_loop(0, tm // GATE_GROUP, group, 0)

    act = _dot_nt(h_ref[...], u_ref[...])
    for s in range(nsub):
        row_i = (c % per_half) * nsub + s
        w = w3_ref[pl.ds(pl.multiple_of(row_i * pitch, 8), tm), :]
        sl = slice(s * LANES, (s + 1) * LANES)
        y_ref[:, sl] = (w * _gelu_exact(act[:, sl])).astype(BF16)
    acc_ref[...] += _dot(y_ref[...], v_ref[...])

    @pl.when(c == pl.num_programs(1) - 1)
    def _():
        x2 = x1_ref[...] + g2_ref[...] * acc_ref[...]
        if final_norm:
            x2 = x2 * lax.rsqrt(jnp.mean(x2 * x2, axis=-1, keepdims=True) + NORM_EPS) * fg_ref[...]
        o_ref[...] = x2


def _peer(h2, wq, k2, u, v, x1, g2, fg, seq, final_norm):
    t, d = h2.shape
    tm = 512
    nsub = 8
    ne = nsub * PEER_KEYS
    nb = seq // tm
    ntile = t // tm
    builds = PEER_KEYS // (PEER_KEYS // 2)
    assert builds * (tm // GATE_GROUP) == 2 * PEER_HEADS, "one routing unit per gate-build iteration"
    row = lambda i, c: (i, 0)
    nxt = lambda i, c: (jnp.minimum(i + 1, ntile - 1), 0)
    chunk = lambda i, c: (c, 0)
    nk = PEER_HEADS * PEER_TOPK
    return pl.pallas_call(
        functools.partial(_peer_kernel, tm=tm, nsub=nsub, final_norm=final_norm),
        grid=(ntile, PEER_KEYS // nsub),
        in_specs=[pl.BlockSpec((tm, d), row), pl.BlockSpec((tm, d), nxt),
                  pl.BlockSpec(wq.shape, lambda i, c: (0, 0)),
                  pl.BlockSpec(k2.shape, lambda i, c: (0, 0, 0)),
                  pl.BlockSpec((ne, d), chunk), pl.BlockSpec((ne, d), chunk),
                  pl.BlockSpec((tm, d), row),
                  pl.BlockSpec((None, 1, d), lambda i, c: (i // nb, 0, 0)),
                  pl.BlockSpec((1, d), lambda i, c: (0, 0))],
        out_specs=pl.BlockSpec((tm, d), row),
        out_shape=jax.ShapeDtypeStruct((t, d), F32),
        scratch_shapes=[pltpu.VMEM((PEER_KEYS // 2 * (tm + W3_PAD), LANES), F32),
                        pltpu.VMEM((tm, ne), BF16),
                        pltpu.VMEM((tm, d), F32),
                        pltpu.VMEM((PEER_HEADS, tm, LANES), BF16),
                        pltpu.VMEM((3, 2, nk, tm // 2), F32),
                        pltpu.VMEM((3, tm, nk), F32),
                        pltpu.VMEM((N_CAND, tm // 2), F32)],
        compiler_params=_cparams("arbitrary", "arbitrary"),
        name="peer",
    )(h2, h2, wq, k2, u, v, x1, g2, fg)


def _dup(w):
    return jnp.concatenate([w, w], axis=1)


def _win_layout(w_in):
    nq, nkv = NSA_HEADS * HEAD_DIM, NSA_KV_HEADS * HEAD_DIM
    sizes = (nq, nkv, nkv, nkv, nkv, nkv, nkv, 3 * NSA_HEADS, DSA_HEADS * HEAD_DIM, DSA_KV_RANK,
             IDX_HEADS * IDX_DIM, IDX_DIM, IDX_HEADS, 2 * D_MODEL)
    offs = [0]
    for s in sizes:
        offs.append(offs[-1] + s)
    q_n, kc, vc, ks, vs, kw, vw, g_n, q_d, ckv, qi, ki, wi, g_m = [w_in[:, offs[k]:offs[k + 1]] for k in range(14)]
    dup2 = lambda w: jnp.concatenate([_dup(w[:, :HEAD_DIM]), _dup(w[:, HEAD_DIM:])], axis=1)
    misc = jnp.concatenate([g_n, wi, jnp.zeros((w_in.shape[0], LANES - 32), w_in.dtype)], axis=1)
    cols = [q_n, dup2(ks), dup2(kw), q_d, qi, _dup(ki), kc, dup2(vs), dup2(vw), vc, ckv, misc, g_m]
    return jnp.concatenate(cols, axis=1).astype(BF16)


def _rope_tables(seq):
    half = HEAD_DIM // 2
    pos = jnp.arange(seq, dtype=F32)
    inv = ROPE_THETA ** (-jnp.arange(half, dtype=F32) / half)
    ang = pos[:, None] * inv[None, :]
    cos, sin, zero = jnp.cos(ang), jnp.sin(ang), jnp.zeros((seq, half), F32)
    c = jnp.concatenate([cos, cos, cos, cos], axis=1)
    s1 = jnp.concatenate([-sin, zero, -sin, zero], axis=1)
    s2 = jnp.concatenate([zero, sin, zero, sin], axis=1)
    return c, s1, s2


def _cmp_weights(pos, w1, w2):
    g = NSA_KV_HEADS
    eye = jnp.eye(g, dtype=F32)
    big = jnp.einsum("lde,gh->lgdhe", w1, eye).reshape(CMP_BLOCK, g * HEAD_DIM, g * HEAD_DIM)
    wa = big[:CMP_STRIDE].reshape(CMP_STRIDE * g * HEAD_DIM, g * HEAD_DIM)
    wb = big[CMP_STRIDE:].reshape(CMP_STRIDE * g * HEAD_DIM, g * HEAD_DIM)
    z = jnp.zeros_like(w2)
    w2d = jnp.concatenate([jnp.concatenate([w2, w2, z, z], axis=1),
                           jnp.concatenate([z, z, w2, w2], axis=1)], axis=0)
    pos_t = jnp.broadcast_to(pos[:, None, :], (CMP_BLOCK, g, HEAD_DIM))
    pa = pos_t[:CMP_STRIDE].reshape(1, -1)
    pb = pos_t[CMP_STRIDE:].reshape(1, -1)
    return wa.astype(BF16), wb.astype(BF16), w2d.astype(BF16), pa, pb


def _selection_constants(seq):
    ncmp = seq // CMP_STRIDE
    nslc = seq // SLC_BLOCK
    per = SLC_BLOCK // CMP_STRIDE
    n = jnp.arange(ncmp)[:, None]
    j = jnp.arange(LANES)[None, :]
    mimp = ((n // per == j).astype(F32) + ((n + 1) // per == j).astype(F32)) * (j < nslc)
    mimp = mimp * (n < ncmp - 1)
    key = jnp.arange(seq)[None, :]
    eexp = (key // SLC_BLOCK == jnp.arange(LANES)[:, None]).astype(F32)
    return mimp.T.astype(BF16), eexp.astype(BF16)


def _subkey_blocks(subkeys):
    z = jnp.zeros_like(subkeys[:, 0])
    top = jnp.concatenate([subkeys[:, 0], z], axis=2)
    bot = jnp.concatenate([z, subkeys[:, 1]], axis=2)
    return jnp.concatenate([top, bot], axis=1).astype(BF16)


def kernel(x, c, ada_w, ada_b, norm1_g, w_in, cmp_pos, cmp_w1_k, cmp_w2_k, cmp_w1_v, cmp_w2_v,
           kv_norm_g, w_uk, w_uv, w_br_nsa, w_br_dsa, w_out, norm2_g,
           peer_wq, peer_subkeys, peer_u, peer_v, final_g):
    bsz, seq, d = x.shape
    depth = ada_w.shape[0]
    t = bsz * seq
    cos, s1, s2 = _rope_tables(seq)
    mimp, eexp = _selection_constants(seq)
    xt = x.reshape(t, d)
    out = xt
    for l in range(depth):
        mod = _ada(c, ada_w[l], ada_b[l]).reshape(bsz, 1, 6 * d)
        sh1, sc1, g1, sh2, sc2, g2 = [mod[:, :, k * d:(k + 1) * d] for k in range(6)]
        a, kc, vv, vc, mm, gm = _proj(xt, sc1, sh1, norm1_g[l].reshape(1, d), _win_layout(w_in[l]),
                                      cos, s1, s2, seq)
        a3 = a.reshape(bsz, seq, NA)
        m3 = mm.reshape(bsz, seq, NM)
        ncmp = seq // CMP_STRIDE
        wak, wbk, w2k, pa, pb = _cmp_weights(cmp_pos[l], cmp_w1_k[l], cmp_w2_k[l])
        wav, wbv, w2v, _, _ = _cmp_weights(cmp_pos[l], cmp_w1_v[l], cmp_w2_v[l])
        o_nsa = _nsa(a3, vv.reshape(bsz, seq, NV), kc.reshape(bsz, ncmp, CMP_STRIDE * 128),
                     vc.reshape(bsz, ncmp, CMP_STRIDE * 128), m3,
                     (wak, wbk, w2k, wav, wbv, w2v, pa, pb, mimp, eexp), seq)
        o_dsa = _dsa(a3, m3, kv_norm_g[l].reshape(1, DSA_KV_RANK), _dup(w_uk[l]).astype(BF16),
                     _dup(w_uv[l]).astype(BF16), cos, s1, s2, seq)
        x1, h2 = _merge(o_nsa.reshape(t, 512), o_dsa.reshape(t, 512), gm, xt, g1, sc2, sh2,
                        norm2_g[l].reshape(1, d), w_br_nsa[l].astype(BF16), w_br_dsa[l].astype(BF16),
                        w_out[l].astype(BF16), seq)
        out = _peer(h2, peer_wq[l].astype(BF16), _subkey_blocks(peer_subkeys[l]),
                    peer_u[l].astype(BF16), peer_v[l].astype(BF16), x1, g2,
                    final_g.reshape(1, d), seq, final_norm=(l == depth - 1))
        xt = out
    return out.reshape(bsz, seq, d)
```

```python
import functools

import jax
import jax.numpy as jnp
from jax import lax
from jax.experimental import pallas as pl
from jax.experimental.pallas import tpu as pltpu

F32 = jnp.float32
BF16 = jnp.bfloat16
I32 = jnp.int32

D_MODEL = 1024
HEAD_DIM = 64
ROPE_THETA = 10000.0
NORM_EPS = 1e-6
NEG_INF = -1e30

NSA_HEADS = 8
NSA_KV_HEADS = 2
CMP_BLOCK = 32
CMP_STRIDE = 16
SLC_BLOCK = 64
SLC_TOPN = 16
WINDOW = 512
NSA_KEY_VARIANTS = 16

DSA_HEADS = 8
DSA_KV_RANK = 128
IDX_HEADS = 8
IDX_DIM = 64
DSA_TOPK = 256
DSA_KEY_VARIANTS = 8

PEER_HEADS = 8
PEER_KEYS = 128
PEER_TOPK = 16

LANES = 128
W3_PAD = 8
ADA_TN = 1024
PROJ_TM = 256
ATTN_TQ = 128
MERGE_TM = 512
PEER_TM = 512
PEER_NSUB = 8
GATE_GROUP = 64
VMEM_LIMIT = 56 * 1024 * 1024

NA = 2176
NV = 512
NM = 256
NG = 2 * D_MODEL
OFF_KC = NA
OFF_V = OFF_KC + 128
OFF_VC = OFF_V + NV
OFF_M = OFF_VC + 128
OFF_G = OFF_M + NM
NZ = OFF_G + NG
MISC_GATE = 0
MISC_WI = 24


def _dot(a, b):
    return jnp.dot(a, b, preferred_element_type=F32)


def _dot_nt(a, b):
    return lax.dot_general(a, b, (((1,), (1,)), ((), ())), preferred_element_type=F32)


def _cparams(*sem):
    return pltpu.CompilerParams(dimension_semantics=sem, vmem_limit_bytes=VMEM_LIMIT)


def _rope(z, c, s1, s2):
    return z * c + pltpu.roll(z, 96, 1) * s1 + pltpu.roll(z, 32, 1) * s2


def _col_reduce(x, op, reduce):
    n = x.shape[0]
    while n > 8 and (n // 8) % 2 == 0:
        n //= 2
        x = op(x[:n], x[n:])
    return reduce(x, axis=0, keepdims=True)


def _colsum(x):
    return _col_reduce(x, jnp.add, jnp.sum)


def _colmax(x):
    return _col_reduce(x, jnp.maximum, jnp.max)


def _half_mask(x, odd):
    lane = lax.broadcasted_iota(I32, x.shape, 1)
    keep = (lane >= HEAD_DIM) if odd else (lane < HEAD_DIM)
    return jnp.where(keep, x, jnp.zeros_like(x))


def _ada_kernel(c_ref, w_ref, b_ref, o_ref):
    c = c_ref[...]
    sc = c * jax.nn.sigmoid(c)
    o_ref[...] = _dot(sc.astype(BF16), w_ref[...].astype(BF16)) + b_ref[...]


def _ada(c, w, b):
    bsz, d = c.shape
    n = w.shape[1]
    tn = ADA_TN
    return pl.pallas_call(
        _ada_kernel,
        grid=(n // tn,),
        in_specs=[pl.BlockSpec((bsz, d), lambda j: (0, 0)),
                  pl.BlockSpec((d, tn), lambda j: (0, j)),
                  pl.BlockSpec((1, tn), lambda j: (0, j))],
        out_specs=pl.BlockSpec((bsz, tn), lambda j: (0, j)),
        out_shape=jax.ShapeDtypeStruct((bsz, n), F32),
        compiler_params=_cparams("arbitrary"),
        name="ada",
    )(c, w, b.reshape(1, n))


def _proj_kernel(x_ref, sc_ref, sh_ref, g_ref, w_ref, cos_ref, s1_ref, s2_ref,
                 a_ref, kc_ref, v_ref, vc_ref, m_ref, gm_ref):
    x = x_ref[...]
    y = x * lax.rsqrt(jnp.mean(x * x, axis=-1, keepdims=True) + NORM_EPS) * g_ref[...]
    h = y * (1.0 + sc_ref[...]) + sh_ref[...]
    z = _dot(h.astype(BF16), w_ref[...])
    c, s1, s2 = cos_ref[...], s1_ref[...], s2_ref[...]
    for k in range(NA // LANES):
        sl = slice(k * LANES, (k + 1) * LANES)
        a_ref[:, sl] = _rope(z[:, sl], c, s1, s2).astype(BF16)
    kc_ref[...] = _rope(z[:, OFF_KC:OFF_KC + 128], c, s1, s2)
    v_ref[...] = z[:, OFF_V:OFF_V + NV].astype(BF16)
    vc_ref[...] = z[:, OFF_VC:OFF_VC + 128]
    m_ref[...] = z[:, OFF_M:OFF_M + NM]
    gm_ref[...] = z[:, OFF_G:OFF_G + NG]


def _proj(x2, sc, sh, g, w, cos, s1, s2, seq):
    t, d = x2.shape
    tm = PROJ_TM
    nb = seq // tm
    row = lambda i: (i, 0)
    bat = lambda i: (i // nb, 0, 0)
    pos = lambda i: (i % nb, 0)
    full = lambda i: (0, 0)
    widths = (NA, 128, NV, 128, NM, NG)
    dtypes = (BF16, F32, BF16, F32, F32, F32)
    return pl.pallas_call(
        _proj_kernel,
        grid=(t // tm,),
        in_specs=[pl.BlockSpec((tm, d), row),
                  pl.BlockSpec((None, 1, d), bat),
                  pl.BlockSpec((None, 1, d), bat),
                  pl.BlockSpec((1, d), full),
                  pl.BlockSpec((d, NZ), full),
                  pl.BlockSpec((tm, LANES), pos),
                  pl.BlockSpec((tm, LANES), pos),
                  pl.BlockSpec((tm, LANES), pos)],
        out_specs=[pl.BlockSpec((tm, n), row) for n in widths],
        out_shape=[jax.ShapeDtypeStruct((t, n), dt) for n, dt in zip(widths, dtypes)],
        compiler_params=_cparams("arbitrary"),
        name="proj",
    )(x2, sc, sh, g, w, cos, s1, s2)


def _masked_attend(s, mask, v):
    s = jnp.where(mask, s, NEG_INF)
    m = jnp.max(s, axis=-1, keepdims=True)
    p = jnp.where(mask, jnp.exp(s - m), 0.0)
    den = jnp.maximum(jnp.sum(p, axis=-1, keepdims=True), 1e-30)
    return _dot(p.astype(BF16), v) / den, p, den


def _nsa_kernel(q_ref, ks_ref, kw_ref, vs_ref, vw_ref, kc_ref, vc_ref, misc_ref,
                wak_ref, wbk_ref, w2k_ref, wav_ref, wbv_ref, w2v_ref, posa_ref, posb_ref,
                mimp_ref, eexp_ref, o_ref, kcmp_ref, vcmp_ref, part_ref, *, tq, seq):
    i = pl.program_id(1)
    ncmp = seq // CMP_STRIDE
    scale = HEAD_DIM ** -0.5

    @pl.when(i == 0)
    def _():
        def compress(c_ref, wa_ref, wb_ref, w2_ref):
            c = c_ref[...]
            ha = _dot((c + posa_ref[...]).astype(BF16), wa_ref[...])
            hb = _dot((c + posb_ref[...]).astype(BF16), wb_ref[...])
            hid = ha + pltpu.roll(hb, ncmp - 1, 0)
            return _dot(jax.nn.gelu(hid).astype(BF16), w2_ref[...])
        kcmp_ref[...] = compress(kc_ref, wak_ref, wbk_ref, w2k_ref).astype(BF16)
        vcmp_ref[...] = compress(vc_ref, wav_ref, wbv_ref, w2v_ref).astype(BF16)

    t0 = i * tq
    trow = t0 + lax.broadcasted_iota(I32, (tq, LANES), 0)
    lane = lax.broadcasted_iota(I32, (tq, LANES), 1)
    sig = jax.nn.sigmoid(misc_ref[...])

    hpg = NSA_HEADS // NSA_KV_HEADS
    qs = [_half_mask(q_ref[:, (h // 2) * LANES:(h // 2 + 1) * LANES], h % 2) * scale for h in range(NSA_HEADS)]

    cmp_mask = (lane * CMP_STRIDE + (CMP_BLOCK - 1)) <= trow
    o_cmp, imp = [], []
    for h in range(NSA_HEADS):
        g = h // hpg
        s = _dot_nt(qs[h], kcmp_ref[:, g * LANES:(g + 1) * LANES])
        o, p, den = _masked_attend(s, cmp_mask, vcmp_ref[:, g * LANES:(g + 1) * LANES])
        o_cmp.append(o)
        pn = p / den
        if h % hpg == 0:
            imp.append(pn)
        else:
            imp[g] = imp[g] + pn

    chosen = []
    for g in range(NSA_KV_HEADS):
        pg = imp[g]
        p_hi = pg.astype(BF16)
        r1 = pg - p_hi.astype(F32)
        p_mid = r1.astype(BF16)
        p_lo = (r1 - p_mid.astype(F32)).astype(BF16)
        blk_t = (_dot_nt(mimp_ref[...], p_hi) + _dot_nt(mimp_ref[...], p_mid) + _dot_nt(mimp_ref[...], p_lo))
        nslc = seq // SLC_BLOCK
        jrow = lax.broadcasted_iota(I32, (nslc, tq), 0)
        tcol = t0 + lax.broadcasted_iota(I32, (nslc, tq), 1)
        cur = tcol // SLC_BLOCK
        forced = (jrow == 0) | (jrow == cur) | (jrow == cur - 1)
        admissible = jrow * SLC_BLOCK <= tcol
        val = jnp.where(forced, jnp.inf, jnp.where(admissible, blk_t[0:nslc, :], -jnp.inf))
        rank = jnp.zeros((nslc, tq), I32)
        for j in range(nslc):
            vj = val[j:j + 1, :]
            ahead = (vj > val) | ((vj == val) & (jrow > j))
            rank = rank + ahead.astype(I32)
        chosen_t = jnp.concatenate([(rank < SLC_TOPN).astype(F32), jnp.zeros((LANES - nslc, tq), F32)], axis=0)
        chosen.append(chosen_t.T.astype(BF16))

    def biased_attend(q, k, v, bias):
        s = _dot_nt(q, k) + bias
        p = jnp.exp(s - jnp.max(s, axis=-1, keepdims=True))
        den = jnp.maximum(jnp.sum(p, axis=-1, keepdims=True), 1e-30)
        return _dot(p.astype(BF16), v) / den

    kspan = WINDOW + tq
    kstart = pl.multiple_of(jnp.maximum(t0 - WINDOW, 0), tq)
    diff_w = (t0 + lax.broadcasted_iota(I32, (tq, kspan), 0)) - (kstart + lax.broadcasted_iota(I32, (tq, kspan), 1))
    win_bias = jnp.where((diff_w >= 0) & (diff_w < WINDOW), 0.0, NEG_INF)
    for h in range(NSA_HEADS):
        g = h // hpg
        kwd = kw_ref[pl.ds(kstart, kspan), g * LANES:(g + 1) * LANES]
        vwd = vw_ref[pl.ds(kstart, kspan), g * LANES:(g + 1) * LANES]
        o_w = biased_attend(qs[h], kwd, vwd, win_bias)
        c0 = MISC_GATE + 3 * h
        part_ref[h] = sig[:, c0:c0 + 1] * o_cmp[h] + sig[:, c0 + 2:c0 + 3] * o_w

    def selected_branch(nk):
        causal = lax.broadcasted_iota(I32, (tq, nk), 1) <= t0 + lax.broadcasted_iota(I32, (tq, nk), 0)
        pair = None
        for g in range(NSA_KV_HEADS):
            picked = _dot(chosen[g], eexp_ref[:, 0:nk]) > 0.5
            bias = jnp.where(picked & causal, 0.0, NEG_INF)
            ksd = ks_ref[0:nk, g * LANES:(g + 1) * LANES]
            vsd = vs_ref[0:nk, g * LANES:(g + 1) * LANES]
            for h in range(g * hpg, (g + 1) * hpg):
                c1 = MISC_GATE + 3 * h + 1
                o_h = part_ref[h] + sig[:, c1:c1 + 1] * biased_attend(qs[h], ksd, vsd, bias)
                if h % 2 == 0:
                    pair = o_h
                else:
                    o_ref[:, (h // 2) * LANES:(h // 2 + 1) * LANES] = (
                        jnp.where(lane < HEAD_DIM, pair, o_h).astype(BF16))

    step = seq // NSA_KEY_VARIANTS
    per = step // tq
    for v in range(NSA_KEY_VARIANTS):
        pl.when(i // per == v)(functools.partial(selected_branch, (v + 1) * step))


def _nsa(a3, v3, kcr, vcr, m3, wts, seq):
    bsz = a3.shape[0]
    tq = ATTN_TQ
    ncmp = seq // CMP_STRIDE
    per_b = lambda blk: (lambda b, i: (b, 0, blk))
    full2 = lambda b, i: (0, 0)
    w_specs = [pl.BlockSpec(w.shape, full2) for w in wts]
    return pl.pallas_call(
        functools.partial(_nsa_kernel, tq=tq, seq=seq),
        grid=(bsz, seq // tq),
        in_specs=[pl.BlockSpec((None, tq, 512), lambda b, i: (b, i, 0)),
                  pl.BlockSpec((None, seq, 256), per_b(2)),
                  pl.BlockSpec((None, seq, 256), per_b(3)),
                  pl.BlockSpec((None, seq, 256), per_b(0)),
                  pl.BlockSpec((None, seq, 256), per_b(1)),
                  pl.BlockSpec((None, ncmp, CMP_STRIDE * 128), per_b(0)),
                  pl.BlockSpec((None, ncmp, CMP_STRIDE * 128), per_b(0)),
                  pl.BlockSpec((None, tq, LANES), lambda b, i: (b, i, 1)),
                  ] + w_specs,
        out_specs=pl.BlockSpec((None, tq, 512), lambda b, i: (b, i, 0)),
        out_shape=jax.ShapeDtypeStruct((bsz, seq, 512), BF16),
        scratch_shapes=[pltpu.VMEM((ncmp, 256), BF16), pltpu.VMEM((ncmp, 256), BF16),
                        pltpu.VMEM((NSA_HEADS, tq, LANES), F32)],
        compiler_params=_cparams("arbitrary", "arbitrary"),
        name="nsa",
    )(a3, a3, a3, v3, v3, kcr, vcr, m3, *wts)


def _dsa_kernel(qd_ref, qi_ref, ki_ref, ckv_ref, misc_ref, gkv_ref, wuk_ref, wuv_ref,
                cos_ref, s1_ref, s2_ref, o_ref, kd_ref, vt_ref, *, tq, seq, topk):
    i = pl.program_id(1)

    @pl.when(i == 0)
    def _():
        ck = ckv_ref[...]
        cn = ck * lax.rsqrt(jnp.mean(ck * ck, axis=-1, keepdims=True) + NORM_EPS) * gkv_ref[...]
        cb = cn.astype(BF16)
        kd_ref[...] = _rope(_dot(cb, wuk_ref[...]), cos_ref[...], s1_ref[...], s2_ref[...]).astype(BF16)
        vt_ref[...] = _dot(cb, wuv_ref[...]).T.astype(BF16)

    t0 = i * tq
    misc_t = misc_ref[...].T
    int_min = jnp.int32(-2 ** 31)
    idx_bits = max(1, (seq - 1).bit_length()) + 1

    def attend(nk):
        kpos = lax.broadcasted_iota(I32, (nk, tq), 0)
        tpos = t0 + lax.broadcasted_iota(I32, (nk, tq), 1)
        causal = kpos <= tpos

        score = jnp.zeros((nk, tq), F32)
        for h in range(IDX_HEADS):
            qm = _half_mask(qi_ref[:, (h // 2) * LANES:(h // 2 + 1) * LANES], h % 2) * (IDX_DIM ** -0.5)
            logit = _dot_nt(ki_ref[0:nk, :], qm)
            w_h = misc_t[MISC_WI + h:MISC_WI + h + 1, :] * (IDX_HEADS ** -0.5)
            score = score + w_h * jnp.maximum(logit, 0.0)
        score = jnp.where(causal, score, -jnp.inf)

        def as_float(cu):
            ks = cu ^ int_min
            return lax.bitcast_convert_type(ks ^ ((ks >> 31) & jnp.int32(0x7FFFFFFF)), F32)

        def bit_step(it, tu):
            cu = tu | lax.shift_left(jnp.int32(1), 31 - it)
            cnt = _colsum((score >= as_float(cu)).astype(F32))
            below_neg_inf = (cu >= 0) & (cu < jnp.int32(0x007FFFFF))
            return jnp.where((cnt >= topk) | below_neg_inf, cu, tu)
        thr = as_float(lax.fori_loop(0, 32, bit_step, jnp.zeros((1, tq), I32)))
        above = score > thr
        need = topk - _colsum(above.astype(F32))
        tie = score == thr
        tie_pos = jnp.where(tie, kpos, jnp.int32(2 * seq))
        surplus = jnp.max(_colsum(tie.astype(F32)) - need)

        def lowest_ties():
            def tie_step(it, lim):
                cl = lim | lax.shift_left(jnp.int32(1), idx_bits - 1 - it)
                cnt = _colsum((tie_pos < cl).astype(F32))
                return jnp.where(cnt <= need, cl, lim)
            return lax.fori_loop(0, idx_bits, tie_step, jnp.zeros((1, tq), I32))

        lim = lax.cond(surplus > 0.0, lowest_ties, lambda: jnp.full((1, tq), 2 * seq, I32))
        sel = causal & (above | (tie_pos < lim))
        bias = jnp.where(sel, 0.0, NEG_INF)

        row = lax.broadcasted_iota(I32, (LANES, tq), 0)
        pair = None
        for h in range(DSA_HEADS):
            qm = _half_mask(qd_ref[:, (h // 2) * LANES:(h // 2 + 1) * LANES], h % 2) * (HEAD_DIM ** -0.5)
            s = _dot_nt(kd_ref[0:nk, :], qm) + bias
            p = jnp.exp(s - _colmax(s))
            den = jnp.maximum(_colsum(p), 1e-30)
            o_t = _dot(vt_ref[:, 0:nk], p.astype(BF16)) / den
            if h % 2 == 0:
                pair = o_t
            else:
                o_ref[:, (h // 2) * LANES:(h // 2 + 1) * LANES] = (
                    jnp.where(row < HEAD_DIM, pair, o_t).T.astype(BF16))

    step = seq // DSA_KEY_VARIANTS
    per = step // tq
    for v in range(DSA_KEY_VARIANTS):
        pl.when(i // per == v)(functools.partial(attend, (v + 1) * step))


def _dsa(a3, m3, gkv, wuk, wuv, cos, s1, s2, seq):
    bsz = a3.shape[0]
    tq = ATTN_TQ
    topk = min(DSA_TOPK, seq // 4)
    assert seq // DSA_KEY_VARIANTS >= topk, "the radix select needs at least top-k keys in every variant"
    full2 = lambda b, i: (0, 0)
    return pl.pallas_call(
        functools.partial(_dsa_kernel, tq=tq, seq=seq, topk=topk),
        grid=(bsz, seq // tq),
        in_specs=[pl.BlockSpec((None, tq, 512), lambda b, i: (b, i, 2)),
                  pl.BlockSpec((None, tq, 512), lambda b, i: (b, i, 3)),
                  pl.BlockSpec((None, seq, LANES), lambda b, i: (b, 0, 16)),
                  pl.BlockSpec((None, seq, LANES), lambda b, i: (b, 0, 0)),
                  pl.BlockSpec((None, tq, LANES), lambda b, i: (b, i, 1)),
                  pl.BlockSpec((1, LANES), full2),
                  pl.BlockSpec((LANES, LANES), full2),
                  pl.BlockSpec((LANES, LANES), full2),
                  pl.BlockSpec((seq, LANES), full2),
                  pl.BlockSpec((seq, LANES), full2),
                  pl.BlockSpec((seq, LANES), full2)],
        out_specs=pl.BlockSpec((None, tq, 512), lambda b, i: (b, i, 0)),
        out_shape=jax.ShapeDtypeStruct((bsz, seq, 512), BF16),
        scratch_shapes=[pltpu.VMEM((seq, LANES), BF16), pltpu.VMEM((LANES, seq), BF16)],
        compiler_params=_cparams("arbitrary", "arbitrary"),
        name="dsa",
    )(a3, a3, a3, m3, m3, gkv, wuk, wuv, cos, s1, s2)


def _merge_kernel(on_ref, od_ref, gm_ref, x_ref, g1_ref, sc_ref, sh_ref, n2_ref,
                  wn_ref, wd_ref, wo_ref, x1_ref, h2_ref):
    gm = gm_ref[...]
    merged = (jax.nn.sigmoid(gm[:, :D_MODEL]) * _dot(on_ref[...], wn_ref[...])
              + jax.nn.sigmoid(gm[:, D_MODEL:]) * _dot(od_ref[...], wd_ref[...]))
    x1 = x_ref[...] + g1_ref[...] * _dot(merged.astype(BF16), wo_ref[...])
    x1_ref[...] = x1
    y = x1 * lax.rsqrt(jnp.mean(x1 * x1, axis=-1, keepdims=True) + NORM_EPS) * n2_ref[...]
    h2_ref[...] = (y * (1.0 + sc_ref[...]) + sh_ref[...]).astype(BF16)


def _merge(o_nsa, o_dsa, gm, x2, g1, sc2, sh2, n2, wn, wd, wo, seq):
    t, d = x2.shape
    tm = MERGE_TM
    nb = seq // tm
    row = lambda i: (i, 0)
    bat = lambda i: (i // nb, 0, 0)
    full = lambda i: (0, 0)
    return pl.pallas_call(
        _merge_kernel,
        grid=(t // tm,),
        in_specs=[pl.BlockSpec((tm, 512), row), pl.BlockSpec((tm, 512), row),
                  pl.BlockSpec((tm, NG), row), pl.BlockSpec((tm, d), row),
                  pl.BlockSpec((None, 1, d), bat), pl.BlockSpec((None, 1, d), bat),
                  pl.BlockSpec((None, 1, d), bat), pl.BlockSpec((1, d), full),
                  pl.BlockSpec(wn.shape, full), pl.BlockSpec(wd.shape, full), pl.BlockSpec(wo.shape, full)],
        out_specs=[pl.BlockSpec((tm, d), row), pl.BlockSpec((tm, d), row)],
        out_shape=[jax.ShapeDtypeStruct((t, d), F32), jax.ShapeDtypeStruct((t, d), BF16)],
        compiler_params=_cparams("arbitrary"),
        name="merge",
    )(o_nsa, o_dsa, gm, x2, g1, sc2, sh2, n2, wn, wd, wo)


def _topk_rows(vals, k):
    n, t = vals.shape
    ridx = lax.broadcasted_iota(I32, (n, t), 0).astype(F32)
    slot = lax.broadcasted_iota(I32, (k, t), 0)
    top_v = jnp.zeros((k, t), F32)
    top_i = jnp.zeros((k, t), F32)
    for j in range(k):
        m = jnp.max(vals, axis=0, keepdims=True)
        idx = jnp.min(jnp.where(vals == m, ridx, float(n)), axis=0, keepdims=True)
        top_v = jnp.where(slot == j, m, top_v)
        top_i = jnp.where(slot == j, idx, top_i)
        vals = jnp.where(ridx == idx, -jnp.inf, vals)
    return top_v, top_i


N_CAND = -(-sum(PEER_TOPK // (a + 1) for a in range(PEER_TOPK)) // 8) * 8


def _route_head(s, cand_ref):
    t = s.shape[1]
    s0, i0 = _topk_rows(s[:PEER_KEYS], PEER_TOPK)
    s1, i1 = _topk_rows(s[PEER_KEYS:], PEER_TOPK)
    r0 = 0
    for a in range(PEER_TOPK):
        nb = PEER_TOPK // (a + 1)
        cand_ref[r0:r0 + nb, :] = s0[a:a + 1, :] + s1[0:nb, :]
        r0 += nb
    cand_ref[r0:, :] = jnp.full((N_CAND - r0, t), -jnp.inf, F32)
    top_s, top_r = _topk_rows(cand_ref[...], PEER_TOPK)
    pa = jnp.zeros((PEER_TOPK, t), F32)
    pb = jnp.zeros((PEER_TOPK, t), F32)
    r = 0
    for a in range(PEER_TOPK):
        for b in range(PEER_TOPK // (a + 1)):
            hit = top_r == r
            pa = jnp.where(hit, float(a), pa)
            pb = jnp.where(hit, float(b), pb)
            r += 1
    ii = jnp.zeros((PEER_TOPK, t), F32)
    jj = jnp.zeros((PEER_TOPK, t), F32)
    for a in range(PEER_TOPK):
        ii = jnp.where(pa == a, i0[a:a + 1, :], ii)
        jj = jnp.where(pb == a, i1[a:a + 1, :], jj)
    e = jnp.exp(top_s - jnp.max(top_s, axis=0, keepdims=True))
    return ii, jj, e / jnp.sum(e, axis=0, keepdims=True)


def _gelu_exact(a):
    return 0.5 * a * (1.0 + lax.erf(a * (2.0 ** -0.5)))


def _peer_kernel(h_ref, hn_ref, wq_ref, k2_ref, u_ref, v_ref, x1_ref, g2_ref, fg_ref, o_ref,
                 w3_ref, y_ref, acc_ref, q3_ref, rt_ref, cur_ref, cand_ref, *, tm, nsub, final_norm):
    i = pl.program_id(0)
    c = pl.program_id(1)
    half = PEER_KEYS // 2
    per_half = half // nsub
    pitch = tm + W3_PAD
    th = tm // 2
    groups = tm // GATE_GROUP

    def project_queries(x_ref):
        q = _dot(x_ref[...], wq_ref[...]).astype(BF16)
        for h in range(PEER_HEADS):
            q3_ref[h] = q[:, h * LANES:(h + 1) * LANES]

    def route_unit(unit):
        h = unit // 2
        part = unit % 2
        qh = q3_ref[h, pl.ds(pl.multiple_of(part * th, th), th), :]
        ii, jj, gate = _route_head(_dot_nt(k2_ref[h], qh), cand_ref)
        rows = pl.ds(pl.multiple_of(h * PEER_TOPK, PEER_TOPK), PEER_TOPK)
        rt_ref[0, part, rows, :] = ii
        rt_ref[1, part, rows, :] = jj
        rt_ref[2, part, rows, :] = gate

    @pl.when(c == 0)
    def _():
        acc_ref[...] = jnp.zeros_like(acc_ref)

        @pl.when(i == 0)
        def _():
            project_queries(h_ref)
            lax.fori_loop(0, 2 * PEER_HEADS, lambda unit, carry: (route_unit(unit), carry)[1], 0)

        for arr in range(3):
            for part in range(2):
                cur_ref[arr, part * th:(part + 1) * th, :] = rt_ref[arr, part].T
        project_queries(hn_ref)

    @pl.when(c % per_half == 0)
    def _():
        i0 = (c // per_half) * half
        sub_i = (lax.broadcasted_iota(I32, (half, LANES), 0) + i0).astype(F32)
        sub_j = lax.broadcasted_iota(I32, (PEER_KEYS, LANES), 0).astype(F32)

        def group(tg, carry):
            route_unit((c // per_half) * groups + tg)
            r0 = pl.multiple_of(tg * GATE_GROUP, GATE_GROUP)
            ii = cur_ref[0, pl.ds(r0, GATE_GROUP), :]
            jj = cur_ref[1, pl.ds(r0, GATE_GROUP), :]
            gg = cur_ref[2, pl.ds(r0, GATE_GROUP), :]
            for r in range(GATE_GROUP):
                a = (sub_i == ii[r:r + 1, :]).astype(BF16)
                rg = jnp.where(sub_j == jj[r:r + 1, :], gg[r:r + 1, :], 0.0).astype(BF16)
                w3_ref[pl.ds(r0 + r, half, stride=pitch), :] = _dot_nt(a, rg)
            return carry
        lax.fori_loop(0, tm // GATE_GROUP, group, 0)

    act = _dot_nt(h_ref[...], u_ref[...])
    for s in range(nsub):
        row_i = (c % per_half) * nsub + s
        w = w3_ref[pl.ds(pl.multiple_of(row_i * pitch, 8), tm), :]
        sl = slice(s * LANES, (s + 1) * LANES)
        y_ref[:, sl] = (w * _gelu_exact(act[:, sl])).astype(BF16)
    acc_ref[...] += _dot(y_ref[...], v_ref[...])

    @pl.when(c == pl.num_programs(1) - 1)
    def _():
        x2 = x1_ref[...] + g2_ref[...] * acc_ref[...]
        if final_norm:
            x2 = x2 * lax.rsqrt(jnp.mean(x2 * x2, axis=-1, keepdims=True) + NORM_EPS) * fg_ref[...]
        o_ref[...] = x2


def _peer(h2, wq, k2, u, v, x1, g2, fg, seq, final_norm):
    t, d = h2.shape
    tm = PEER_TM
    nsub = PEER_NSUB
    ne = nsub * PEER_KEYS
    nb = seq // tm
    ntile = t // tm
    builds = PEER_KEYS // (PEER_KEYS // 2)
    assert builds * (tm // GATE_GROUP) == 2 * PEER_HEADS, "one routing unit per gate-build iteration"
    row = lambda i, c: (i, 0)
    nxt = lambda i, c: (jnp.minimum(i + 1, ntile - 1), 0)
    chunk = lambda i, c: (c, 0)
    nk = PEER_HEADS * PEER_TOPK
    return pl.pallas_call(
        functools.partial(_peer_kernel, tm=tm, nsub=nsub, final_norm=final_norm),
        grid=(ntile, PEER_KEYS // nsub),
        in_specs=[pl.BlockSpec((tm, d), row), pl.BlockSpec((tm, d), nxt),
                  pl.BlockSpec(wq.shape, lambda i, c: (0, 0)),
                  pl.BlockSpec(k2.shape, lambda i, c: (0, 0, 0)),
                  pl.BlockSpec((ne, d), chunk), pl.BlockSpec((ne, d), chunk),
                  pl.BlockSpec((tm, d), row),
                  pl.BlockSpec((None, 1, d), lambda i, c: (i // nb, 0, 0)),
                  pl.BlockSpec((1, d), lambda i, c: (0, 0))],
        out_specs=pl.BlockSpec((tm, d), row),
        out_shape=jax.ShapeDtypeStruct((t, d), F32),
        scratch_shapes=[pltpu.VMEM((PEER_KEYS // 2 * (tm + W3_PAD), LANES), F32),
                        pltpu.VMEM((tm, ne), BF16),
                        pltpu.VMEM((tm, d), F32),
                        pltpu.VMEM((PEER_HEADS, tm, LANES), BF16),
                        pltpu.VMEM((3, 2, nk, tm // 2), F32),
                        pltpu.VMEM((3, tm, nk), F32),
                        pltpu.VMEM((N_CAND, tm // 2), F32)],
        compiler_params=_cparams("arbitrary", "arbitrary"),
        name="peer",
    )(h2, h2, wq, k2, u, v, x1, g2, fg)


def _dup(w):
    return jnp.concatenate([w, w], axis=1)


def _win_layout(w_in):
    nq, nkv = NSA_HEADS * HEAD_DIM, NSA_KV_HEADS * HEAD_DIM
    sizes = (nq, nkv, nkv, nkv, nkv, nkv, nkv, 3 * NSA_HEADS, DSA_HEADS * HEAD_DIM, DSA_KV_RANK,
             IDX_HEADS * IDX_DIM, IDX_DIM, IDX_HEADS, 2 * D_MODEL)
    offs = [0]
    for s in sizes:
        offs.append(offs[-1] + s)
    q_n, kc, vc, ks, vs, kw, vw, g_n, q_d, ckv, qi, ki, wi, g_m = [w_in[:, offs[k]:offs[k + 1]] for k in range(14)]
    dup2 = lambda w: jnp.concatenate([_dup(w[:, :HEAD_DIM]), _dup(w[:, HEAD_DIM:])], axis=1)
    misc = jnp.concatenate([g_n, wi, jnp.zeros((w_in.shape[0], LANES - 32), w_in.dtype)], axis=1)
    cols = [q_n, dup2(ks), dup2(kw), q_d, qi, _dup(ki), kc, dup2(vs), dup2(vw), vc, ckv, misc, g_m]
    return jnp.concatenate(cols, axis=1).astype(BF16)


def _rope_tables(seq):
    half = HEAD_DIM // 2
    pos = jnp.arange(seq, dtype=F32)
    inv = ROPE_THETA ** (-jnp.arange(half, dtype=F32) / half)
    ang = pos[:, None] * inv[None, :]
    cos, sin, zero = jnp.cos(ang), jnp.sin(ang), jnp.zeros((seq, half), F32)
    c = jnp.concatenate([cos, cos, cos, cos], axis=1)
    s1 = jnp.concatenate([-sin, zero, -sin, zero], axis=1)
    s2 = jnp.concatenate([zero, sin, zero, sin], axis=1)
    return c, s1, s2


def _cmp_weights(pos, w1, w2):
    g = NSA_KV_HEADS
    eye = jnp.eye(g, dtype=F32)
    big = jnp.einsum("lde,gh->lgdhe", w1, eye).reshape(CMP_BLOCK, g * HEAD_DIM, g * HEAD_DIM)
    wa = big[:CMP_STRIDE].reshape(CMP_STRIDE * g * HEAD_DIM, g * HEAD_DIM)
    wb = big[CMP_STRIDE:].reshape(CMP_STRIDE * g * HEAD_DIM, g * HEAD_DIM)
    z = jnp.zeros_like(w2)
    w2d = jnp.concatenate([jnp.concatenate([w2, w2, z, z], axis=1),
                           jnp.concatenate([z, z, w2, w2], axis=1)], axis=0)
    pos_t = jnp.broadcast_to(pos[:, None, :], (CMP_BLOCK, g, HEAD_DIM))
    pa = pos_t[:CMP_STRIDE].reshape(1, -1)
    pb = pos_t[CMP_STRIDE:].reshape(1, -1)
    return wa.astype(BF16), wb.astype(BF16), w2d.astype(BF16), pa, pb


def _selection_constants(seq):
    ncmp = seq // CMP_STRIDE
    nslc = seq // SLC_BLOCK
    per = SLC_BLOCK // CMP_STRIDE
    n = jnp.arange(ncmp)[:, None]
    j = jnp.arange(LANES)[None, :]
    mimp = ((n // per == j).astype(F32) + ((n + 1) // per == j).astype(F32)) * (j < nslc)
    mimp = mimp * (n < ncmp - 1)
    key = jnp.arange(seq)[None, :]
    eexp = (key // SLC_BLOCK == jnp.arange(LANES)[:, None]).astype(F32)
    return mimp.T.astype(BF16), eexp.astype(BF16)


def _subkey_blocks(subkeys):
    z = jnp.zeros_like(subkeys[:, 0])
    top = jnp.concatenate([subkeys[:, 0], z], axis=2)
    bot = jnp.concatenate([z, subkeys[:, 1]], axis=2)
    return jnp.concatenate([top, bot], axis=1).astype(BF16)


def kernel(x, c, ada_w, ada_b, norm1_g, w_in, cmp_pos, cmp_w1_k, cmp_w2_k, cmp_w1_v, cmp_w2_v,
           kv_norm_g, w_uk, w_uv, w_br_nsa, w_br_dsa, w_out, norm2_g,
           peer_wq, peer_subkeys, peer_u, peer_v, final_g):
    bsz, seq, d = x.shape
    depth = ada_w.shape[0]
    t = bsz * seq
    cos, s1, s2 = _rope_tables(seq)
    mimp, eexp = _selection_constants(seq)
    xt = x.reshape(t, d)
    out = xt
    for l in range(depth):
        mod = _ada(c, ada_w[l], ada_b[l]).reshape(bsz, 1, 6 * d)
        sh1, sc1, g1, sh2, sc2, g2 = [mod[:, :, k * d:(k + 1) * d] for k in range(6)]
        a, kc, vv, vc, mm, gm = _proj(xt, sc1, sh1, norm1_g[l].reshape(1, d), _win_layout(w_in[l]),
                                      cos, s1, s2, seq)
        a3 = a.reshape(bsz, seq, NA)
        m3 = mm.reshape(bsz, seq, NM)
        ncmp = seq // CMP_STRIDE
        wak, wbk, w2k, pa, pb = _cmp_weights(cmp_pos[l], cmp_w1_k[l], cmp_w2_k[l])
        wav, wbv, w2v, _, _ = _cmp_weights(cmp_pos[l], cmp_w1_v[l], cmp_w2_v[l])
        o_nsa = _nsa(a3, vv.reshape(bsz, seq, NV), kc.reshape(bsz, ncmp, CMP_STRIDE * 128),
                     vc.reshape(bsz, ncmp, CMP_STRIDE * 128), m3,
                     (wak, wbk, w2k, wav, wbv, w2v, pa, pb, mimp, eexp), seq)
        o_dsa = _dsa(a3, m3, kv_norm_g[l].reshape(1, DSA_KV_RANK), _dup(w_uk[l]).astype(BF16),
                     _dup(w_uv[l]).astype(BF16), cos, s1, s2, seq)
        x1, h2 = _merge(o_nsa.reshape(t, 512), o_dsa.reshape(t, 512), gm, xt, g1, sc2, sh2,
                        norm2_g[l].reshape(1, d), w_br_nsa[l].astype(BF16), w_br_dsa[l].astype(BF16),
                        w_out[l].astype(BF16), seq)
        out = _peer(h2, peer_wq[l].astype(BF16), _subkey_blocks(peer_subkeys[l]),
                    peer_u[l].astype(BF16), peer_v[l].astype(BF16), x1, g2,
                    final_g.reshape(1, d), seq, final_norm=(l == depth - 1))
        xt = out
    return out.reshape(bsz, seq, d)
```

```python
import functools

import jax
import jax.numpy as jnp
from jax import lax
from jax.experimental import pallas as pl
from jax.experimental.pallas import tpu as pltpu

F32 = jnp.float32
BF16 = jnp.bfloat16
I32 = jnp.int32

D_MODEL = 1024
HEAD_DIM = 64
ROPE_THETA = 10000.0
NORM_EPS = 1e-6
NEG_INF = -1e30

NSA_HEADS = 8
NSA_KV_HEADS = 2
CMP_BLOCK = 32
CMP_STRIDE = 16
SLC_BLOCK = 64
SLC_TOPN = 16
WINDOW = 512
NSA_KEY_VARIANTS = 8

DSA_HEADS = 8
DSA_KV_RANK = 128
IDX_HEADS = 8
IDX_DIM = 64
DSA_TOPK = 256
DSA_KEY_VARIANTS = 8

PEER_HEADS = 8
PEER_KEYS = 128
PEER_TOPK = 16

LANES = 128
W3_PAD = 8
ADA_TN = 1024
PROJ_TM = 256
ATTN_TQ = 128
MERGE_TM = 512
PEER_TM = 512
PEER_NSUB = 8
GATE_GROUP = 64
VMEM_LIMIT = 56 * 1024 * 1024

NA = 2176
NV = 512
NM = 256
NG = 2 * D_MODEL
OFF_KC = NA
OFF_V = OFF_KC + 128
OFF_VC = OFF_V + NV
OFF_M = OFF_VC + 128
OFF_G = OFF_M + NM
NZ = OFF_G + NG
MISC_GATE = 0
MISC_WI = 24


def _dot(a, b):
    return jnp.dot(a, b, preferred_element_type=F32)


def _dot_nt(a, b):
    return lax.dot_general(a, b, (((1,), (1,)), ((), ())), preferred_element_type=F32)


def _cparams(*sem):
    return pltpu.CompilerParams(dimension_semantics=sem, vmem_limit_bytes=VMEM_LIMIT)


def _rope(z, c, s1, s2):
    return z * c + pltpu.roll(z, 96, 1) * s1 + pltpu.roll(z, 32, 1) * s2


def _col_reduce(x, op, reduce):
    n = x.shape[0]
    while n > 8 and (n // 8) % 2 == 0:
        n //= 2
        x = op(x[:n], x[n:])
    return reduce(x, axis=0, keepdims=True)


def _colsum(x):
    return _col_reduce(x, jnp.add, jnp.sum)


def _colmax(x):
    return _col_reduce(x, jnp.maximum, jnp.max)


def _half_mask(x, odd):
    lane = lax.broadcasted_iota(I32, x.shape, 1)
    keep = (lane >= HEAD_DIM) if odd else (lane < HEAD_DIM)
    return jnp.where(keep, x, jnp.zeros_like(x))


def _ada_kernel(c_ref, w_ref, b_ref, o_ref):
    c = c_ref[...]
    sc = c * jax.nn.sigmoid(c)
    o_ref[...] = _dot(sc.astype(BF16), w_ref[...].astype(BF16)) + b_ref[...]


def _ada(c, w, b):
    bsz, d = c.shape
    n = w.shape[1]
    tn = ADA_TN
    return pl.pallas_call(
        _ada_kernel,
        grid=(n // tn,),
        in_specs=[pl.BlockSpec((bsz, d), lambda j: (0, 0)),
                  pl.BlockSpec((d, tn), lambda j: (0, j)),
                  pl.BlockSpec((1, tn), lambda j: (0, j))],
        out_specs=pl.BlockSpec((bsz, tn), lambda j: (0, j)),
        out_shape=jax.ShapeDtypeStruct((bsz, n), F32),
        compiler_params=_cparams("arbitrary"),
        name="ada",
    )(c, w, b.reshape(1, n))


def _proj_kernel(x_ref, sc_ref, sh_ref, g_ref, w_ref, cos_ref, s1_ref, s2_ref,
                 a_ref, kc_ref, v_ref, vc_ref, m_ref, gm_ref):
    x = x_ref[...]
    y = x * lax.rsqrt(jnp.mean(x * x, axis=-1, keepdims=True) + NORM_EPS) * g_ref[...]
    h = y * (1.0 + sc_ref[...]) + sh_ref[...]
    z = _dot(h.astype(BF16), w_ref[...])
    c, s1, s2 = cos_ref[...], s1_ref[...], s2_ref[...]
    for k in range(NA // LANES):
        sl = slice(k * LANES, (k + 1) * LANES)
        a_ref[:, sl] = _rope(z[:, sl], c, s1, s2).astype(BF16)
    kc_ref[...] = _rope(z[:, OFF_KC:OFF_KC + 128], c, s1, s2)
    v_ref[...] = z[:, OFF_V:OFF_V + NV].astype(BF16)
    vc_ref[...] = z[:, OFF_VC:OFF_VC + 128]
    m_ref[...] = z[:, OFF_M:OFF_M + NM]
    gm_ref[...] = z[:, OFF_G:OFF_G + NG]


def _proj(x2, sc, sh, g, w, cos, s1, s2, seq):
    t, d = x2.shape
    tm = PROJ_TM
    nb = seq // tm
    row = lambda i: (i, 0)
    bat = lambda i: (i // nb, 0, 0)
    pos = lambda i: (i % nb, 0)
    full = lambda i: (0, 0)
    widths = (NA, 128, NV, 128, NM, NG)
    dtypes = (BF16, F32, BF16, F32, F32, F32)
    return pl.pallas_call(
        _proj_kernel,
        grid=(t // tm,),
        in_specs=[pl.BlockSpec((tm, d), row),
                  pl.BlockSpec((None, 1, d), bat),
                  pl.BlockSpec((None, 1, d), bat),
                  pl.BlockSpec((1, d), full),
                  pl.BlockSpec((d, NZ), full),
                  pl.BlockSpec((tm, LANES), pos),
                  pl.BlockSpec((tm, LANES), pos),
                  pl.BlockSpec((tm, LANES), pos)],
        out_specs=[pl.BlockSpec((tm, n), row) for n in widths],
        out_shape=[jax.ShapeDtypeStruct((t, n), dt) for n, dt in zip(widths, dtypes)],
        compiler_params=_cparams("arbitrary"),
        name="proj",
    )(x2, sc, sh, g, w, cos, s1, s2)


def _masked_attend(s, mask, v):
    s = jnp.where(mask, s, NEG_INF)
    m = jnp.max(s, axis=-1, keepdims=True)
    p = jnp.where(mask, jnp.exp(s - m), 0.0)
    den = jnp.maximum(jnp.sum(p, axis=-1, keepdims=True), 1e-30)
    return _dot(p.astype(BF16), v) / den, p, den


def _nsa_kernel(q_ref, ks_ref, kw_ref, vs_ref, vw_ref, kc_ref, vc_ref, misc_ref,
                wak_ref, wbk_ref, w2k_ref, wav_ref, wbv_ref, w2v_ref, posa_ref, posb_ref,
                mimp_ref, eexp_ref, o_ref, kcmp_ref, vcmp_ref, part_ref, *, tq, seq):
    i = pl.program_id(1)
    ncmp = seq // CMP_STRIDE
    scale = HEAD_DIM ** -0.5

    @pl.when(i == 0)
    def _():
        def compress(c_ref, wa_ref, wb_ref, w2_ref):
            c = c_ref[...]
            ha = _dot((c + posa_ref[...]).astype(BF16), wa_ref[...])
            hb = _dot((c + posb_ref[...]).astype(BF16), wb_ref[...])
            hid = ha + pltpu.roll(hb, ncmp - 1, 0)
            return _dot(jax.nn.gelu(hid).astype(BF16), w2_ref[...])
        kcmp_ref[...] = compress(kc_ref, wak_ref, wbk_ref, w2k_ref).astype(BF16)
        vcmp_ref[...] = compress(vc_ref, wav_ref, wbv_ref, w2v_ref).astype(BF16)

    t0 = i * tq
    trow = t0 + lax.broadcasted_iota(I32, (tq, LANES), 0)
    lane = lax.broadcasted_iota(I32, (tq, LANES), 1)
    sig = jax.nn.sigmoid(misc_ref[...])

    hpg = NSA_HEADS // NSA_KV_HEADS
    qs = [_half_mask(q_ref[:, (h // 2) * LANES:(h // 2 + 1) * LANES], h % 2) * scale for h in range(NSA_HEADS)]

    cmp_mask = (lane * CMP_STRIDE + (CMP_BLOCK - 1)) <= trow
    o_cmp, imp = [], []
    for h in range(NSA_HEADS):
        g = h // hpg
        s = _dot_nt(qs[h], kcmp_ref[:, g * LANES:(g + 1) * LANES])
        o, p, den = _masked_attend(s, cmp_mask, vcmp_ref[:, g * LANES:(g + 1) * LANES])
        o_cmp.append(o)
        pn = p / den
        if h % hpg == 0:
            imp.append(pn)
        else:
            imp[g] = imp[g] + pn

    chosen = []
    for g in range(NSA_KV_HEADS):
        pg = imp[g]
        p_hi = pg.astype(BF16)
        r1 = pg - p_hi.astype(F32)
        p_mid = r1.astype(BF16)
        p_lo = (r1 - p_mid.astype(F32)).astype(BF16)
        blk_t = (_dot_nt(mimp_ref[...], p_hi) + _dot_nt(mimp_ref[...], p_mid) + _dot_nt(mimp_ref[...], p_lo))
        nslc = seq // SLC_BLOCK
        jrow = lax.broadcasted_iota(I32, (nslc, tq), 0)
        tcol = t0 + lax.broadcasted_iota(I32, (nslc, tq), 1)
        cur = tcol // SLC_BLOCK
        forced = (jrow == 0) | (jrow == cur) | (jrow == cur - 1)
        admissible = jrow * SLC_BLOCK <= tcol
        val = jnp.where(forced, jnp.inf, jnp.where(admissible, blk_t[0:nslc, :], -jnp.inf))
        rank = jnp.zeros((nslc, tq), I32)
        for j in range(nslc):
            vj = val[j:j + 1, :]
            ahead = (vj > val) | ((vj == val) & (jrow > j))
            rank = rank + ahead.astype(I32)
        chosen_t = jnp.concatenate([(rank < SLC_TOPN).astype(F32), jnp.zeros((LANES - nslc, tq), F32)], axis=0)
        chosen.append(chosen_t.T.astype(BF16))

    def biased_attend(q, k, v, bias):
        s = _dot_nt(q, k) + bias
        p = jnp.exp(s - jnp.max(s, axis=-1, keepdims=True))
        den = jnp.maximum(jnp.sum(p, axis=-1, keepdims=True), 1e-30)
        return _dot(p.astype(BF16), v) / den

    kspan = WINDOW + tq
    kstart = pl.multiple_of(jnp.maximum(t0 - WINDOW, 0), tq)
    diff_w = (t0 + lax.broadcasted_iota(I32, (tq, kspan), 0)) - (kstart + lax.broadcasted_iota(I32, (tq, kspan), 1))
    win_bias = jnp.where((diff_w >= 0) & (diff_w < WINDOW), 0.0, NEG_INF)
    for h in range(NSA_HEADS):
        g = h // hpg
        kwd = kw_ref[pl.ds(kstart, kspan), g * LANES:(g + 1) * LANES]
        vwd = vw_ref[pl.ds(kstart, kspan), g * LANES:(g + 1) * LANES]
        o_w = biased_attend(qs[h], kwd, vwd, win_bias)
        c0 = MISC_GATE + 3 * h
        part_ref[h] = sig[:, c0:c0 + 1] * o_cmp[h] + sig[:, c0 + 2:c0 + 3] * o_w

    def selected_branch(nk):
        causal = lax.broadcasted_iota(I32, (tq, nk), 1) <= t0 + lax.broadcasted_iota(I32, (tq, nk), 0)
        pair = None
        for g in range(NSA_KV_HEADS):
            picked = _dot(chosen[g], eexp_ref[:, 0:nk]) > 0.5
            bias = jnp.where(picked & causal, 0.0, NEG_INF)
            ksd = ks_ref[0:nk, g * LANES:(g + 1) * LANES]
            vsd = vs_ref[0:nk, g * LANES:(g + 1) * LANES]
            for h in range(g * hpg, (g + 1) * hpg):
                c1 = MISC_GATE + 3 * h + 1
                o_h = part_ref[h] + sig[:, c1:c1 + 1] * biased_attend(qs[h], ksd, vsd, bias)
                if h % 2 == 0:
                    pair = o_h
                else:
                    o_ref[:, (h // 2) * LANES:(h // 2 + 1) * LANES] = (
                        jnp.where(lane < HEAD_DIM, pair, o_h).astype(BF16))

    step = seq // NSA_KEY_VARIANTS
    per = step // tq
    for v in range(NSA_KEY_VARIANTS):
        pl.when(i // per == v)(functools.partial(selected_branch, (v + 1) * step))


def _nsa(a3, v3, kcr, vcr, m3, wts, seq):
    bsz = a3.shape[0]
    tq = ATTN_TQ
    ncmp = seq // CMP_STRIDE
    per_b = lambda blk: (lambda b, i: (b, 0, blk))
    full2 = lambda b, i: (0, 0)
    w_specs = [pl.BlockSpec(w.shape, full2) for w in wts]
    return pl.pallas_call(
        functools.partial(_nsa_kernel, tq=tq, seq=seq),
        grid=(bsz, seq // tq),
        in_specs=[pl.BlockSpec((None, tq, 512), lambda b, i: (b, i, 0)),
                  pl.BlockSpec((None, seq, 256), per_b(2)),
                  pl.BlockSpec((None, seq, 256), per_b(3)),
                  pl.BlockSpec((None, seq, 256), per_b(0)),
                  pl.BlockSpec((None, seq, 256), per_b(1)),
                  pl.BlockSpec((None, ncmp, CMP_STRIDE * 128), per_b(0)),
                  pl.BlockSpec((None, ncmp, CMP_STRIDE * 128), per_b(0)),
                  pl.BlockSpec((None, tq, LANES), lambda b, i: (b, i, 1)),
                  ] + w_specs,
        out_specs=pl.BlockSpec((None, tq, 512), lambda b, i: (b, i, 0)),
        out_shape=jax.ShapeDtypeStruct((bsz, seq, 512), BF16),
        scratch_shapes=[pltpu.VMEM((ncmp, 256), BF16), pltpu.VMEM((ncmp, 256), BF16),
                        pltpu.VMEM((NSA_HEADS, tq, LANES), F32)],
        compiler_params=_cparams("arbitrary", "arbitrary"),
        name="nsa",
    )(a3, a3, a3, v3, v3, kcr, vcr, m3, *wts)


def _dsa_kernel(qd_ref, qi_ref, ki_ref, ckv_ref, misc_ref, gkv_ref, wuk_ref, wuv_ref,
                cos_ref, s1_ref, s2_ref, o_ref, kd_ref, vt_ref, *, tq, seq, topk):
    i = pl.program_id(1)

    @pl.when(i == 0)
    def _():
        ck = ckv_ref[...]
        cn = ck * lax.rsqrt(jnp.mean(ck * ck, axis=-1, keepdims=True) + NORM_EPS) * gkv_ref[...]
        cb = cn.astype(BF16)
        kd_ref[...] = _rope(_dot(cb, wuk_ref[...]), cos_ref[...], s1_ref[...], s2_ref[...]).astype(BF16)
        vt_ref[...] = _dot(cb, wuv_ref[...]).T.astype(BF16)

    t0 = i * tq
    misc_t = misc_ref[...].T
    int_min = jnp.int32(-2 ** 31)
    idx_bits = max(1, (seq - 1).bit_length()) + 1

    def attend(nk):
        kpos = lax.broadcasted_iota(I32, (nk, tq), 0)
        tpos = t0 + lax.broadcasted_iota(I32, (nk, tq), 1)
        causal = kpos <= tpos

        score = jnp.zeros((nk, tq), F32)
        for h in range(IDX_HEADS):
            qm = _half_mask(qi_ref[:, (h // 2) * LANES:(h // 2 + 1) * LANES], h % 2) * (IDX_DIM ** -0.5)
            logit = _dot_nt(ki_ref[0:nk, :], qm)
            w_h = misc_t[MISC_WI + h:MISC_WI + h + 1, :] * (IDX_HEADS ** -0.5)
            score = score + w_h * jnp.maximum(logit, 0.0)
        score = jnp.where(causal, score, -jnp.inf)

        def as_float(cu):
            ks = cu ^ int_min
            return lax.bitcast_convert_type(ks ^ ((ks >> 31) & jnp.int32(0x7FFFFFFF)), F32)

        def bit_step(it, tu):
            cu = tu | lax.shift_left(jnp.int32(1), 31 - it)
            cnt = _colsum((score >= as_float(cu)).astype(F32))
            below_neg_inf = (cu >= 0) & (cu < jnp.int32(0x007FFFFF))
            return jnp.where((cnt >= topk) | below_neg_inf, cu, tu)
        thr = as_float(lax.fori_loop(0, 32, bit_step, jnp.zeros((1, tq), I32)))
        above = score > thr
        need = topk - _colsum(above.astype(F32))
        tie = score == thr
        tie_pos = jnp.where(tie, kpos, jnp.int32(2 * seq))
        surplus = jnp.max(_colsum(tie.astype(F32)) - need)

        def lowest_ties():
            def tie_step(it, lim):
                cl = lim | lax.shift_left(jnp.int32(1), idx_bits - 1 - it)
                cnt = _colsum((tie_pos < cl).astype(F32))
                return jnp.where(cnt <= need, cl, lim)
            return lax.fori_loop(0, idx_bits, tie_step, jnp.zeros((1, tq), I32))

        lim = lax.cond(surplus > 0.0, lowest_ties, lambda: jnp.full((1, tq), 2 * seq, I32))
        sel = causal & (above | (tie_pos < lim))
        bias = jnp.where(sel, 0.0, NEG_INF)

        row = lax.broadcasted_iota(I32, (LANES, tq), 0)
        pair = None
        for h in range(DSA_HEADS):
            qm = _half_mask(qd_ref[:, (h // 2) * LANES:(h // 2 + 1) * LANES], h % 2) * (HEAD_DIM ** -0.5)
            s = _dot_nt(kd_ref[0:nk, :], qm) + bias
            p = jnp.exp(s - _colmax(s))
            den = jnp.maximum(_colsum(p), 1e-30)
            o_t = _dot(vt_ref[:, 0:nk], p.astype(BF16)) / den
            if h % 2 == 0:
                pair = o_t
            else:
                o_ref[:, (h // 2) * LANES:(h // 2 + 1) * LANES] = (
                    jnp.where(row < HEAD_DIM, pair, o_t).T.astype(BF16))

    step = seq // DSA_KEY_VARIANTS
    per = step // tq
    for v in range(DSA_KEY_VARIANTS):
        pl.when(i // per == v)(functools.partial(attend, (v + 1) * step))


def _dsa(a3, m3, gkv, wuk, wuv, cos, s1, s2, seq):
    bsz = a3.shape[0]
    tq = ATTN_TQ
    topk = min(DSA_TOPK, seq // 4)
    assert seq // DSA_KEY_VARIANTS >= topk, "the radix select needs at least top-k keys in every variant"
    full2 = lambda b, i: (0, 0)
    return pl.pallas_call(
        functools.partial(_dsa_kernel, tq=tq, seq=seq, topk=topk),
        grid=(bsz, seq // tq),
        in_specs=[pl.BlockSpec((None, tq, 512), lambda b, i: (b, i, 2)),
                  pl.BlockSpec((None, tq, 512), lambda b, i: (b, i, 3)),
                  pl.BlockSpec((None, seq, LANES), lambda b, i: (b, 0, 16)),
                  pl.BlockSpec((None, seq, LANES), lambda b, i: (b, 0, 0)),
                  pl.BlockSpec((None, tq, LANES), lambda b, i: (b, i, 1)),
                  pl.BlockSpec((1, LANES), full2),
                  pl.BlockSpec((LANES, LANES), full2),
                  pl.BlockSpec((LANES, LANES), full2),
                  pl.BlockSpec((seq, LANES), full2),
                  pl.BlockSpec((seq, LANES), full2),
                  pl.BlockSpec((seq, LANES), full2)],
        out_specs=pl.BlockSpec((None, tq, 512), lambda b, i: (b, i, 0)),
        out_shape=jax.ShapeDtypeStruct((bsz, seq, 512), BF16),
        scratch_shapes=[pltpu.VMEM((seq, LANES), BF16), pltpu.VMEM((LANES, seq), BF16)],
        compiler_params=_cparams("arbitrary", "arbitrary"),
        name="dsa",
    )(a3, a3, a3, m3, m3, gkv, wuk, wuv, cos, s1, s2)


def _merge_kernel(on_ref, od_ref, gm_ref, x_ref, g1_ref, sc_ref, sh_ref, n2_ref,
                  wn_ref, wd_ref, wo_ref, x1_ref, h2_ref):
    gm = gm_ref[...]
    merged = (jax.nn.sigmoid(gm[:, :D_MODEL]) * _dot(on_ref[...], wn_ref[...])
              + jax.nn.sigmoid(gm[:, D_MODEL:]) * _dot(od_ref[...], wd_ref[...]))
    x1 = x_ref[...] + g1_ref[...] * _dot(merged.astype(BF16), wo_ref[...])
    x1_ref[...] = x1
    y = x1 * lax.rsqrt(jnp.mean(x1 * x1, axis=-1, keepdims=True) + NORM_EPS) * n2_ref[...]
    h2_ref[...] = (y * (1.0 + sc_ref[...]) + sh_ref[...]).astype(BF16)


def _merge(o_nsa, o_dsa, gm, x2, g1, sc2, sh2, n2, wn, wd, wo, seq):
    t, d = x2.shape
    tm = MERGE_TM
    nb = seq // tm
    row = lambda i: (i, 0)
    bat = lambda i: (i // nb, 0, 0)
    full = lambda i: (0, 0)
    return pl.pallas_call(
        _merge_kernel,
        grid=(t // tm,),
        in_specs=[pl.BlockSpec((tm, 512), row), pl.BlockSpec((tm, 512), row),
                  pl.BlockSpec((tm, NG), row), pl.BlockSpec((tm, d), row),
                  pl.BlockSpec((None, 1, d), bat), pl.BlockSpec((None, 1, d), bat),
                  pl.BlockSpec((None, 1, d), bat), pl.BlockSpec((1, d), full),
                  pl.BlockSpec(wn.shape, full), pl.BlockSpec(wd.shape, full), pl.BlockSpec(wo.shape, full)],
        out_specs=[pl.BlockSpec((tm, d), row), pl.BlockSpec((tm, d), row)],
        out_shape=[jax.ShapeDtypeStruct((t, d), F32), jax.ShapeDtypeStruct((t, d), BF16)],
        compiler_params=_cparams("arbitrary"),
        name="merge",
    )(o_nsa, o_dsa, gm, x2, g1, sc2, sh2, n2, wn, wd, wo)


def _topk_rows(vals, k):
    n, t = vals.shape
    ridx = lax.broadcasted_iota(I32, (n, t), 0).astype(F32)
    slot = lax.broadcasted_iota(I32, (k, t), 0)
    top_v = jnp.zeros((k, t), F32)
    top_i = jnp.zeros((k, t), F32)
    for j in range(k):
        m = jnp.max(vals, axis=0, keepdims=True)
        idx = jnp.min(jnp.where(vals == m, ridx, float(n)), axis=0, keepdims=True)
        top_v = jnp.where(slot == j, m, top_v)
        top_i = jnp.where(slot == j, idx, top_i)
        vals = jnp.where(ridx == idx, -jnp.inf, vals)
    return top_v, top_i


N_CAND = -(-sum(PEER_TOPK // (a + 1) for a in range(PEER_TOPK)) // 8) * 8


def _route_head(s, cand_ref):
    t = s.shape[1]
    s0, i0 = _topk_rows(s[:PEER_KEYS], PEER_TOPK)
    s1, i1 = _topk_rows(s[PEER_KEYS:], PEER_TOPK)
    r0 = 0
    for a in range(PEER_TOPK):
        nb = PEER_TOPK // (a + 1)
        cand_ref[r0:r0 + nb, :] = s0[a:a + 1, :] + s1[0:nb, :]
        r0 += nb
    cand_ref[r0:, :] = jnp.full((N_CAND - r0, t), -jnp.inf, F32)
    top_s, top_r = _topk_rows(cand_ref[...], PEER_TOPK)
    pa = jnp.zeros((PEER_TOPK, t), F32)
    pb = jnp.zeros((PEER_TOPK, t), F32)
    r = 0
    for a in range(PEER_TOPK):
        for b in range(PEER_TOPK // (a + 1)):
            hit = top_r == r
            pa = jnp.where(hit, float(a), pa)
            pb = jnp.where(hit, float(b), pb)
            r += 1
    ii = jnp.zeros((PEER_TOPK, t), F32)
    jj = jnp.zeros((PEER_TOPK, t), F32)
    for a in range(PEER_TOPK):
        ii = jnp.where(pa == a, i0[a:a + 1, :], ii)
        jj = jnp.where(pb == a, i1[a:a + 1, :], jj)
    e = jnp.exp(top_s - jnp.max(top_s, axis=0, keepdims=True))
    return ii, jj, e / jnp.sum(e, axis=0, keepdims=True)


def _gelu_exact(a):
    return 0.5 * a * (1.0 + lax.erf(a * (2.0 ** -0.5)))


def _peer_kernel(h_ref, hn_ref, wq_ref, k2_ref, u_ref, v_ref, x1_ref, g2_ref, fg_ref, o_ref,
                 w3_ref, y_ref, acc_ref, q3_ref, rt_ref, cur_ref, cand_ref, *, tm, nsub, final_norm):
    i = pl.program_id(0)
    c = pl.program_id(1)
    half = PEER_KEYS // 2
    per_half = half // nsub
    pitch = tm + W3_PAD
    th = tm // 2
    groups = tm // GATE_GROUP

    def project_queries(x_ref):
        q = _dot(x_ref[...], wq_ref[...]).astype(BF16)
        for h in range(PEER_HEADS):
            q3_ref[h] = q[:, h * LANES:(h + 1) * LANES]

    def route_unit(unit):
        h = unit // 2
        part = unit % 2
        qh = q3_ref[h, pl.ds(pl.multiple_of(part * th, th), th), :]
        ii, jj, gate = _route_head(_dot_nt(k2_ref[h], qh), cand_ref)
        rows = pl.ds(pl.multiple_of(h * PEER_TOPK, PEER_TOPK), PEER_TOPK)
        rt_ref[0, part, rows, :] = ii
        rt_ref[1, part, rows, :] = jj
        rt_ref[2, part, rows, :] = gate

    @pl.when(c == 0)
    def _():
        acc_ref[...] = jnp.zeros_like(acc_ref)

        @pl.when(i == 0)
        def _():
            project_queries(h_ref)
            lax.fori_loop(0, 2 * PEER_HEADS, lambda unit, carry: (route_unit(unit), carry)[1], 0)

        for arr in range(3):
            for part in range(2):
                cur_ref[arr, part * th:(part + 1) * th, :] = rt_ref[arr, part].T
        project_queries(hn_ref)

    @pl.when(c % per_half == 0)
    def _():
        i0 = (c // per_half) * half
        sub_i = (lax.broadcasted_iota(I32, (half, LANES), 0) + i0).astype(F32)
        sub_j = lax.broadcasted_iota(I32, (PEER_KEYS, LANES), 0).astype(F32)

        def group(tg, carry):
            route_unit((c // per_half) * groups + tg)
            r0 = pl.multiple_of(tg * GATE_GROUP, GATE_GROUP)
            ii = cur_ref[0, pl.ds(r0, GATE_GROUP), :]
            jj = cur_ref[1, pl.ds(r0, GATE_GROUP), :]
            gg = cur_ref[2, pl.ds(r0, GATE_GROUP), :]
            for r in range(GATE_GROUP):
                a = (sub_i == ii[r:r + 1, :]).astype(BF16)
                rg = jnp.where(sub_j == jj[r:r + 1, :], gg[r:r + 1, :], 0.0).astype(BF16)
                w3_ref[pl.ds(r0 + r, half, stride=pitch), :] = _dot_nt(a, rg)
            return carry
        lax.fori_loop(0, tm // GATE_GROUP, group, 0)

    act = _dot_nt(h_ref[...], u_ref[...])
    for s in range(nsub):
        row_i = (c % per_half) * nsub + s
        w = w3_ref[pl.ds(pl.multiple_of(row_i * pitch, 8), tm), :]
        sl = slice(s * LANES, (s + 1) * LANES)
        y_ref[:, sl] = (w * _gelu_exact(act[:, sl])).astype(BF16)
    acc_ref[...] += _dot(y_ref[...], v_ref[...])

    @pl.when(c == pl.num_programs(1) - 1)
    def _():
        x2 = x1_ref[...] + g2_ref[...] * acc_ref[...]
        if final_norm:
            x2 = x2 * lax.rsqrt(jnp.mean(x2 * x2, axis=-1, keepdims=True) + NORM_EPS) * fg_ref[...]
        o_ref[...] = x2


def _peer(h2, wq, k2, u, v, x1, g2, fg, seq, final_norm):
    t, d = h2.shape
    tm = PEER_TM
    nsub = PEER_NSUB
    ne = nsub * PEER_KEYS
    nb = seq // tm
    ntile = t // tm
    builds = PEER_KEYS // (PEER_KEYS // 2)
    assert builds * (tm // GATE_GROUP) == 2 * PEER_HEADS, "one routing unit per gate-build iteration"
    row = lambda i, c: (i, 0)
    nxt = lambda i, c: (jnp.minimum(i + 1, ntile - 1), 0)
    chunk = lambda i, c: (c, 0)
    nk = PEER_HEADS * PEER_TOPK
    return pl.pallas_call(
        functools.partial(_peer_kernel, tm=tm, nsub=nsub, final_norm=final_norm),
        grid=(ntile, PEER_KEYS // nsub),
        in_specs=[pl.BlockSpec((tm, d), row), pl.BlockSpec((tm, d), nxt),
                  pl.BlockSpec(wq.shape, lambda i, c: (0, 0)),
                  pl.BlockSpec(k2.shape, lambda i, c: (0, 0, 0)),
                  pl.BlockSpec((ne, d), chunk), pl.BlockSpec((ne, d), chunk),
                  pl.BlockSpec((tm, d), row),
                  pl.BlockSpec((None, 1, d), lambda i, c: (i // nb, 0, 0)),
                  pl.BlockSpec((1, d), lambda i, c: (0, 0))],
        out_specs=pl.BlockSpec((tm, d), row),
        out_shape=jax.ShapeDtypeStruct((t, d), F32),
        scratch_shapes=[pltpu.VMEM((PEER_KEYS // 2 * (tm + W3_PAD), LANES), F32),
                        pltpu.VMEM((tm, ne), BF16),
                        pltpu.VMEM((tm, d), F32),
                        pltpu.VMEM((PEER_HEADS, tm, LANES), BF16),
                        pltpu.VMEM((3, 2, nk, tm // 2), F32),
                        pltpu.VMEM((3, tm, nk), F32),
                        pltpu.VMEM((N_CAND, tm // 2), F32)],
        compiler_params=_cparams("arbitrary", "arbitrary"),
        name="peer",
    )(h2, h2, wq, k2, u, v, x1, g2, fg)


def _dup(w):
    return jnp.concatenate([w, w], axis=1)


def _win_layout(w_in):
    nq, nkv = NSA_HEADS * HEAD_DIM, NSA_KV_HEADS * HEAD_DIM
    sizes = (nq, nkv, nkv, nkv, nkv, nkv, nkv, 3 * NSA_HEADS, DSA_HEADS * HEAD_DIM, DSA_KV_RANK,
             IDX_HEADS * IDX_DIM, IDX_DIM, IDX_HEADS, 2 * D_MODEL)
    offs = [0]
    for s in sizes:
        offs.append(offs[-1] + s)
    q_n, kc, vc, ks, vs, kw, vw, g_n, q_d, ckv, qi, ki, wi, g_m = [w_in[:, offs[k]:offs[k + 1]] for k in range(14)]
    dup2 = lambda w: jnp.concatenate([_dup(w[:, :HEAD_DIM]), _dup(w[:, HEAD_DIM:])], axis=1)
    misc = jnp.concatenate([g_n, wi, jnp.zeros((w_in.shape[0], LANES - 32), w_in.dtype)], axis=1)
    cols = [q_n, dup2(ks), dup2(kw), q_d, qi, _dup(ki), kc, dup2(vs), dup2(vw), vc, ckv, misc, g_m]
    return jnp.concatenate(cols, axis=1).astype(BF16)


def _rope_tables(seq):
    half = HEAD_DIM // 2
    pos = jnp.arange(seq, dtype=F32)
    inv = ROPE_THETA ** (-jnp.arange(half, dtype=F32) / half)
    ang = pos[:, None] * inv[None, :]
    cos, sin, zero = jnp.cos(ang), jnp.sin(ang), jnp.zeros((seq, half), F32)
    c = jnp.concatenate([cos, cos, cos, cos], axis=1)
    s1 = jnp.concatenate([-sin, zero, -sin, zero], axis=1)
    s2 = jnp.concatenate([zero, sin, zero, sin], axis=1)
    return c, s1, s2


def _cmp_weights(pos, w1, w2):
    g = NSA_KV_HEADS
    eye = jnp.eye(g, dtype=F32)
    big = jnp.einsum("lde,gh->lgdhe", w1, eye).reshape(CMP_BLOCK, g * HEAD_DIM, g * HEAD_DIM)
    wa = big[:CMP_STRIDE].reshape(CMP_STRIDE * g * HEAD_DIM, g * HEAD_DIM)
    wb = big[CMP_STRIDE:].reshape(CMP_STRIDE * g * HEAD_DIM, g * HEAD_DIM)
    z = jnp.zeros_like(w2)
    w2d = jnp.concatenate([jnp.concatenate([w2, w2, z, z], axis=1),
                           jnp.concatenate([z, z, w2, w2], axis=1)], axis=0)
    pos_t = jnp.broadcast_to(pos[:, None, :], (CMP_BLOCK, g, HEAD_DIM))
    pa = pos_t[:CMP_STRIDE].reshape(1, -1)
    pb = pos_t[CMP_STRIDE:].reshape(1, -1)
    return wa.astype(BF16), wb.astype(BF16), w2d.astype(BF16), pa, pb


def _selection_constants(seq):
    ncmp = seq // CMP_STRIDE
    nslc = seq // SLC_BLOCK
    per = SLC_BLOCK // CMP_STRIDE
    n = jnp.arange(ncmp)[:, None]
    j = jnp.arange(LANES)[None, :]
    mimp = ((n // per == j).astype(F32) + ((n + 1) // per == j).astype(F32)) * (j < nslc)
    mimp = mimp * (n < ncmp - 1)
    key = jnp.arange(seq)[None, :]
    eexp = (key // SLC_BLOCK == jnp.arange(LANES)[:, None]).astype(F32)
    return mimp.T.astype(BF16), eexp.astype(BF16)


def _subkey_blocks(subkeys):
    z = jnp.zeros_like(subkeys[:, 0])
    top = jnp.concatenate([subkeys[:, 0], z], axis=2)
    bot = jnp.concatenate([z, subkeys[:, 1]], axis=2)
    return jnp.concatenate([top, bot], axis=1).astype(BF16)


def kernel(x, c, ada_w, ada_b, norm1_g, w_in, cmp_pos, cmp_w1_k, cmp_w2_k, cmp_w1_v, cmp_w2_v,
           kv_norm_g, w_uk, w_uv, w_br_nsa, w_br_dsa, w_out, norm2_g,
           peer_wq, peer_subkeys, peer_u, peer_v, final_g):
    bsz, seq, d = x.shape
    depth = ada_w.shape[0]
    t = bsz * seq
    cos, s1, s2 = _rope_tables(seq)
    mimp, eexp = _selection_constants(seq)
    xt = x.reshape(t, d)
    out = xt
    for l in range(depth):
        mod = _ada(c, ada_w[l], ada_b[l]).reshape(bsz, 1, 6 * d)
        sh1, sc1, g1, sh2, sc2, g2 = [mod[:, :, k * d:(k + 1) * d] for k in range(6)]
        a, kc, vv, vc, mm, gm = _proj(xt, sc1, sh1, norm1_g[l].reshape(1, d), _win_layout(w_in[l]),
                                      cos, s1, s2, seq)
        a3 = a.reshape(bsz, seq, NA)
        m3 = mm.reshape(bsz, seq, NM)
        ncmp = seq // CMP_STRIDE
        wak, wbk, w2k, pa, pb = _cmp_weights(cmp_pos[l], cmp_w1_k[l], cmp_w2_k[l])
        wav, wbv, w2v, _, _ = _cmp_weights(cmp_pos[l], cmp_w1_v[l], cmp_w2_v[l])
        o_nsa = _nsa(a3, vv.reshape(bsz, seq, NV), kc.reshape(bsz, ncmp, CMP_STRIDE * 128),
                     vc.reshape(bsz, ncmp, CMP_STRIDE * 128), m3,
                     (wak, wbk, w2k, wav, wbv, w2v, pa, pb, mimp, eexp), seq)
        o_dsa = _dsa(a3, m3, kv_norm_g[l].reshape(1, DSA_KV_RANK), _dup(w_uk[l]).astype(BF16),
                     _dup(w_uv[l]).astype(BF16), cos, s1, s2, seq)
        x1, h2 = _merge(o_nsa.reshape(t, 512), o_dsa.reshape(t, 512), gm, xt, g1, sc2, sh2,
                        norm2_g[l].reshape(1, d), w_br_nsa[l].astype(BF16), w_br_dsa[l].astype(BF16),
                        w_out[l].astype(BF16), seq)
        out = _peer(h2, peer_wq[l].astype(BF16), _subkey_blocks(peer_subkeys[l]),
                    peer_u[l].astype(BF16), peer_v[l].astype(BF16), x1, g2,
                    final_g.reshape(1, d), seq, final_norm=(l == depth - 1))
        xt = out
    return out.reshape(bsz, seq, d)
```

```python
import functools

import jax
import jax.numpy as jnp
from jax import lax
from jax.experimental import pallas as pl
from jax.experimental.pallas import tpu as pltpu

F32 = jnp.float32
BF16 = jnp.bfloat16
I32 = jnp.int32

D_MODEL = 1024
HEAD_DIM = 64
ROPE_THETA = 10000.0
NORM_EPS = 1e-6
NEG_INF = -1e30

NSA_HEADS = 8
NSA_KV_HEADS = 2
CMP_BLOCK = 32
CMP_STRIDE = 16
SLC_BLOCK = 64
SLC_TOPN = 16
WINDOW = 512
NSA_KEY_VARIANTS = 8

DSA_HEADS = 8
DSA_KV_RANK = 128
IDX_HEADS = 8
IDX_DIM = 64
DSA_TOPK = 256
DSA_KEY_VARIANTS = 8

PEER_HEADS = 8
PEER_KEYS = 128
PEER_TOPK = 16

LANES = 128
W3_PAD = 8
ADA_TN = 1024
PROJ_TM = 256
ATTN_TQ = 128
MERGE_TM = 512
PEER_TM = 512
PEER_NSUB = 8
GATE_GROUP = 64
VMEM_LIMIT = 56 * 1024 * 1024

NA = 2176
NV = 512
NM = 256
NG = 2 * D_MODEL
OFF_KC = NA
OFF_V = OFF_KC + 128
OFF_VC = OFF_V + NV
OFF_M = OFF_VC + 128
OFF_G = OFF_M + NM
NZ = OFF_G + NG
MISC_GATE = 0
MISC_WI = 24


def _dot(a, b):
    return jnp.dot(a, b, preferred_element_type=F32)


def _dot_nt(a, b):
    return lax.dot_general(a, b, (((1,), (1,)), ((), ())), preferred_element_type=F32)


def _cparams(*sem):
    return pltpu.CompilerParams(dimension_semantics=sem, vmem_limit_bytes=VMEM_LIMIT)


def _rope(z, c, s1, s2):
    return z * c + pltpu.roll(z, 96, 1) * s1 + pltpu.roll(z, 32, 1) * s2


def _col_reduce(x, op, reduce):
    n = x.shape[0]
    while n > 8 and (n // 8) % 2 == 0:
        n //= 2
        x = op(x[:n], x[n:])
    return reduce(x, axis=0, keepdims=True)


def _colsum(x):
    return _col_reduce(x, jnp.add, jnp.sum)


def _colmax(x):
    return _col_reduce(x, jnp.maximum, jnp.max)


def _half_mask(x, odd):
    lane = lax.broadcasted_iota(I32, x.shape, 1)
    keep = (lane >= HEAD_DIM) if odd else (lane < HEAD_DIM)
    return jnp.where(keep, x, jnp.zeros_like(x))


def _ada_kernel(c_ref, w_ref, b_ref, o_ref):
    c = c_ref[...]
    sc = c * jax.nn.sigmoid(c)
    o_ref[...] = _dot(sc.astype(BF16), w_ref[...].astype(BF16)) + b_ref[...]


def _ada(c, w, b):
    bsz, d = c.shape
    n = w.shape[1]
    tn = ADA_TN
    return pl.pallas_call(
        _ada_kernel,
        grid=(n // tn,),
        in_specs=[pl.BlockSpec((bsz, d), lambda j: (0, 0)),
                  pl.BlockSpec((d, tn), lambda j: (0, j)),
                  pl.BlockSpec((1, tn), lambda j: (0, j))],
        out_specs=pl.BlockSpec((bsz, tn), lambda j: (0, j)),
        out_shape=jax.ShapeDtypeStruct((bsz, n), F32),
        compiler_params=_cparams("arbitrary"),
        name="ada",
    )(c, w, b.reshape(1, n))


def _modulated_norm(x, g, sc, sh):
    y = x * lax.rsqrt(jnp.mean(x * x, axis=-1, keepdims=True) + NORM_EPS) * g
    return y * (1.0 + sc) + sh


def _proj_kernel(x_ref, sc_ref, sh_ref, g_ref, w_ref, cos_ref, s1_ref, s2_ref,
                 a_ref, kc_ref, v_ref, vc_ref, m_ref):
    z = _dot(_modulated_norm(x_ref[...], g_ref[...], sc_ref[...], sh_ref[...]).astype(BF16), w_ref[...])
    c, s1, s2 = cos_ref[...], s1_ref[...], s2_ref[...]
    for k in range(NA // LANES):
        sl = slice(k * LANES, (k + 1) * LANES)
        a_ref[:, sl] = _rope(z[:, sl], c, s1, s2).astype(BF16)
    kc_ref[...] = _rope(z[:, OFF_KC:OFF_KC + 128], c, s1, s2)
    v_ref[...] = z[:, OFF_V:OFF_V + NV].astype(BF16)
    vc_ref[...] = z[:, OFF_VC:OFF_VC + 128]
    m_ref[...] = z[:, OFF_M:OFF_M + NM]


def _proj(x2, sc, sh, g, w, cos, s1, s2, seq):
    t, d = x2.shape
    tm = PROJ_TM
    nb = seq // tm
    row = lambda i: (i, 0)
    bat = lambda i: (i // nb, 0, 0)
    pos = lambda i: (i % nb, 0)
    full = lambda i: (0, 0)
    widths = (NA, 128, NV, 128, NM)
    dtypes = (BF16, F32, BF16, F32, F32)
    return pl.pallas_call(
        _proj_kernel,
        grid=(t // tm,),
        in_specs=[pl.BlockSpec((tm, d), row),
                  pl.BlockSpec((None, 1, d), bat),
                  pl.BlockSpec((None, 1, d), bat),
                  pl.BlockSpec((1, d), full),
                  pl.BlockSpec((d, OFF_G), full),
                  pl.BlockSpec((tm, LANES), pos),
                  pl.BlockSpec((tm, LANES), pos),
                  pl.BlockSpec((tm, LANES), pos)],
        out_specs=[pl.BlockSpec((tm, n), row) for n in widths],
        out_shape=[jax.ShapeDtypeStruct((t, n), dt) for n, dt in zip(widths, dtypes)],
        compiler_params=_cparams("arbitrary"),
        name="proj",
    )(x2, sc, sh, g, w, cos, s1, s2)


def _masked_attend(s, mask, v):
    s = jnp.where(mask, s, NEG_INF)
    m = jnp.max(s, axis=-1, keepdims=True)
    p = jnp.where(mask, jnp.exp(s - m), 0.0)
    den = jnp.maximum(jnp.sum(p, axis=-1, keepdims=True), 1e-30)
    return _dot(p.astype(BF16), v) / den, p, den


def _nsa_kernel(q_ref, ks_ref, kw_ref, vs_ref, vw_ref, kc_ref, vc_ref, misc_ref,
                wak_ref, wbk_ref, w2k_ref, wav_ref, wbv_ref, w2v_ref, posa_ref, posb_ref,
                mimp_ref, eexp_ref, o_ref, kcmp_ref, vcmp_ref, part_ref, *, tq, seq):
    i = pl.program_id(1)
    ncmp = seq // CMP_STRIDE
    scale = HEAD_DIM ** -0.5

    @pl.when(i == 0)
    def _():
        def compress(c_ref, wa_ref, wb_ref, w2_ref):
            c = c_ref[...]
            ha = _dot((c + posa_ref[...]).astype(BF16), wa_ref[...])
            hb = _dot((c + posb_ref[...]).astype(BF16), wb_ref[...])
            hid = ha + pltpu.roll(hb, ncmp - 1, 0)
            return _dot(jax.nn.gelu(hid).astype(BF16), w2_ref[...])
        kcmp_ref[...] = compress(kc_ref, wak_ref, wbk_ref, w2k_ref).astype(BF16)
        vcmp_ref[...] = compress(vc_ref, wav_ref, wbv_ref, w2v_ref).astype(BF16)

    t0 = i * tq
    trow = t0 + lax.broadcasted_iota(I32, (tq, LANES), 0)
    lane = lax.broadcasted_iota(I32, (tq, LANES), 1)
    sig = jax.nn.sigmoid(misc_ref[...])

    hpg = NSA_HEADS // NSA_KV_HEADS
    qs = [_half_mask(q_ref[:, (h // 2) * LANES:(h // 2 + 1) * LANES], h % 2) * scale for h in range(NSA_HEADS)]

    cmp_mask = (lane * CMP_STRIDE + (CMP_BLOCK - 1)) <= trow
    o_cmp, imp = [], []
    for h in range(NSA_HEADS):
        g = h // hpg
        s = _dot_nt(qs[h], kcmp_ref[:, g * LANES:(g + 1) * LANES])
        o, p, den = _masked_attend(s, cmp_mask, vcmp_ref[:, g * LANES:(g + 1) * LANES])
        o_cmp.append(o)
        pn = p / den
        if h % hpg == 0:
            imp.append(pn)
        else:
            imp[g] = imp[g] + pn

    chosen = []
    for g in range(NSA_KV_HEADS):
        pg = imp[g]
        p_hi = pg.astype(BF16)
        r1 = pg - p_hi.astype(F32)
        p_mid = r1.astype(BF16)
        p_lo = (r1 - p_mid.astype(F32)).astype(BF16)
        blk_t = (_dot_nt(mimp_ref[...], p_hi) + _dot_nt(mimp_ref[...], p_mid) + _dot_nt(mimp_ref[...], p_lo))
        nslc = seq // SLC_BLOCK
        jrow = lax.broadcasted_iota(I32, (nslc, tq), 0)
        tcol = t0 + lax.broadcasted_iota(I32, (nslc, tq), 1)
        cur = tcol // SLC_BLOCK
        forced = (jrow == 0) | (jrow == cur) | (jrow == cur - 1)
        admissible = jrow * SLC_BLOCK <= tcol
        val = jnp.where(forced, jnp.inf, jnp.where(admissible, blk_t[0:nslc, :], -jnp.inf))
        rank = jnp.zeros((nslc, tq), I32)
        for j in range(nslc):
            vj = val[j:j + 1, :]
            ahead = (vj > val) | ((vj == val) & (jrow > j))
            rank = rank + ahead.astype(I32)
        chosen_t = jnp.concatenate([(rank < SLC_TOPN).astype(F32), jnp.zeros((LANES - nslc, tq), F32)], axis=0)
        chosen.append(chosen_t.T.astype(BF16))

    def biased_attend(q, k, v, bias):
        s = _dot_nt(q, k) + bias
        p = jnp.exp(s - jnp.max(s, axis=-1, keepdims=True))
        den = jnp.maximum(jnp.sum(p, axis=-1, keepdims=True), 1e-30)
        return _dot(p.astype(BF16), v) / den

    kspan = WINDOW + tq
    kstart = pl.multiple_of(jnp.maximum(t0 - WINDOW, 0), tq)
    diff_w = (t0 + lax.broadcasted_iota(I32, (tq, kspan), 0)) - (kstart + lax.broadcasted_iota(I32, (tq, kspan), 1))
    win_bias = jnp.where((diff_w >= 0) & (diff_w < WINDOW), 0.0, NEG_INF)
    for h in range(NSA_HEADS):
        g = h // hpg
        kwd = kw_ref[pl.ds(kstart, kspan), g * LANES:(g + 1) * LANES]
        vwd = vw_ref[pl.ds(kstart, kspan), g * LANES:(g + 1) * LANES]
        o_w = biased_attend(qs[h], kwd, vwd, win_bias)
        c0 = MISC_GATE + 3 * h
        part_ref[h] = sig[:, c0:c0 + 1] * o_cmp[h] + sig[:, c0 + 2:c0 + 3] * o_w

    def selected_branch(nk):
        causal = lax.broadcasted_iota(I32, (tq, nk), 1) <= t0 + lax.broadcasted_iota(I32, (tq, nk), 0)
        pair = None
        for g in range(NSA_KV_HEADS):
            picked = _dot(chosen[g], eexp_ref[:, 0:nk]) > 0.5
            bias = jnp.where(picked & causal, 0.0, NEG_INF)
            ksd = ks_ref[0:nk, g * LANES:(g + 1) * LANES]
            vsd = vs_ref[0:nk, g * LANES:(g + 1) * LANES]
            for h in range(g * hpg, (g + 1) * hpg):
                c1 = MISC_GATE + 3 * h + 1
                o_h = part_ref[h] + sig[:, c1:c1 + 1] * biased_attend(qs[h], ksd, vsd, bias)
                if h % 2 == 0:
                    pair = o_h
                else:
                    o_ref[:, (h // 2) * LANES:(h // 2 + 1) * LANES] = (
                        jnp.where(lane < HEAD_DIM, pair, o_h).astype(BF16))

    step = seq // NSA_KEY_VARIANTS
    per = step // tq
    for v in range(NSA_KEY_VARIANTS):
        pl.when(i // per == v)(functools.partial(selected_branch, (v + 1) * step))


def _nsa(a3, v3, kcr, vcr, m3, wts, seq):
    bsz = a3.shape[0]
    tq = ATTN_TQ
    ncmp = seq // CMP_STRIDE
    per_b = lambda blk: (lambda b, i: (b, 0, blk))
    full2 = lambda b, i: (0, 0)
    w_specs = [pl.BlockSpec(w.shape, full2) for w in wts]
    return pl.pallas_call(
        functools.partial(_nsa_kernel, tq=tq, seq=seq),
        grid=(bsz, seq // tq),
        in_specs=[pl.BlockSpec((None, tq, 512), lambda b, i: (b, i, 0)),
                  pl.BlockSpec((None, seq, 256), per_b(2)),
                  pl.BlockSpec((None, seq, 256), per_b(3)),
                  pl.BlockSpec((None, seq, 256), per_b(0)),
                  pl.BlockSpec((None, seq, 256), per_b(1)),
                  pl.BlockSpec((None, ncmp, CMP_STRIDE * 128), per_b(0)),
                  pl.BlockSpec((None, ncmp, CMP_STRIDE * 128), per_b(0)),
                  pl.BlockSpec((None, tq, LANES), lambda b, i: (b, i, 1)),
                  ] + w_specs,
        out_specs=pl.BlockSpec((None, tq, 512), lambda b, i: (b, i, 0)),
        out_shape=jax.ShapeDtypeStruct((bsz, seq, 512), BF16),
        scratch_shapes=[pltpu.VMEM((ncmp, 256), BF16), pltpu.VMEM((ncmp, 256), BF16),
                        pltpu.VMEM((NSA_HEADS, tq, LANES), F32)],
        compiler_params=_cparams("arbitrary", "arbitrary"),
        name="nsa",
    )(a3, a3, a3, v3, v3, kcr, vcr, m3, *wts)


def _dsa_kernel(qd_ref, qi_ref, ki_ref, ckv_ref, misc_ref, gkv_ref, wuk_ref, wuv_ref,
                cos_ref, s1_ref, s2_ref, o_ref, kd_ref, vt_ref, *, tq, seq, topk):
    i = pl.program_id(1)

    @pl.when(i == 0)
    def _():
        ck = ckv_ref[...]
        cn = ck * lax.rsqrt(jnp.mean(ck * ck, axis=-1, keepdims=True) + NORM_EPS) * gkv_ref[...]
        cb = cn.astype(BF16)
        kd_ref[...] = _rope(_dot(cb, wuk_ref[...]), cos_ref[...], s1_ref[...], s2_ref[...]).astype(BF16)
        vt_ref[...] = _dot(cb, wuv_ref[...]).T.astype(BF16)

    t0 = i * tq
    misc_t = misc_ref[...].T
    int_min = jnp.int32(-2 ** 31)
    idx_bits = max(1, (seq - 1).bit_length()) + 1

    def attend(nk):
        kpos = lax.broadcasted_iota(I32, (nk, tq), 0)
        tpos = t0 + lax.broadcasted_iota(I32, (nk, tq), 1)
        causal = kpos <= tpos

        score = jnp.zeros((nk, tq), F32)
        for h in range(IDX_HEADS):
            qm = _half_mask(qi_ref[:, (h // 2) * LANES:(h // 2 + 1) * LANES], h % 2) * (IDX_DIM ** -0.5)
            logit = _dot_nt(ki_ref[0:nk, :], qm)
            w_h = misc_t[MISC_WI + h:MISC_WI + h + 1, :] * (IDX_HEADS ** -0.5)
            score = score + w_h * jnp.maximum(logit, 0.0)
        score = jnp.where(causal, score, -jnp.inf)

        def as_float(cu):
            ks = cu ^ int_min
            return lax.bitcast_convert_type(ks ^ ((ks >> 31) & jnp.int32(0x7FFFFFFF)), F32)

        def bit_step(it, tu):
            cu = tu | lax.shift_left(jnp.int32(1), 31 - it)
            cnt = _colsum((score >= as_float(cu)).astype(F32))
            below_neg_inf = (cu >= 0) & (cu < jnp.int32(0x007FFFFF))
            return jnp.where((cnt >= topk) | below_neg_inf, cu, tu)
        thr = as_float(lax.fori_loop(0, 32, bit_step, jnp.zeros((1, tq), I32)))
        above = score > thr
        need = topk - _colsum(above.astype(F32))
        tie = score == thr
        tie_pos = jnp.where(tie, kpos, jnp.int32(2 * seq))
        surplus = jnp.max(_colsum(tie.astype(F32)) - need)

        def lowest_ties():
            def tie_step(it, lim):
                cl = lim | lax.shift_left(jnp.int32(1), idx_bits - 1 - it)
                cnt = _colsum((tie_pos < cl).astype(F32))
                return jnp.where(cnt <= need, cl, lim)
            return lax.fori_loop(0, idx_bits, tie_step, jnp.zeros((1, tq), I32))

        lim = lax.cond(surplus > 0.0, lowest_ties, lambda: jnp.full((1, tq), 2 * seq, I32))
        sel = causal & (above | (tie_pos < lim))
        bias = jnp.where(sel, 0.0, NEG_INF)

        row = lax.broadcasted_iota(I32, (LANES, tq), 0)
        pair = None
        for h in range(DSA_HEADS):
            qm = _half_mask(qd_ref[:, (h // 2) * LANES:(h // 2 + 1) * LANES], h % 2) * (HEAD_DIM ** -0.5)
            s = _dot_nt(kd_ref[0:nk, :], qm) + bias
            p = jnp.exp(s - _colmax(s))
            den = jnp.maximum(_colsum(p), 1e-30)
            o_t = _dot(vt_ref[:, 0:nk], p.astype(BF16)) / den
            if h % 2 == 0:
                pair = o_t
            else:
                o_ref[:, (h // 2) * LANES:(h // 2 + 1) * LANES] = (
                    jnp.where(row < HEAD_DIM, pair, o_t).T.astype(BF16))

    step = seq // DSA_KEY_VARIANTS
    per = step // tq
    for v in range(DSA_KEY_VARIANTS):
        pl.when(i // per == v)(functools.partial(attend, (v + 1) * step))


def _dsa(a3, m3, gkv, wuk, wuv, cos, s1, s2, seq):
    bsz = a3.shape[0]
    tq = ATTN_TQ
    topk = min(DSA_TOPK, seq // 4)
    assert seq // DSA_KEY_VARIANTS >= topk, "the radix select needs at least top-k keys in every variant"
    full2 = lambda b, i: (0, 0)
    return pl.pallas_call(
        functools.partial(_dsa_kernel, tq=tq, seq=seq, topk=topk),
        grid=(bsz, seq // tq),
        in_specs=[pl.BlockSpec((None, tq, 512), lambda b, i: (b, i, 2)),
                  pl.BlockSpec((None, tq, 512), lambda b, i: (b, i, 3)),
                  pl.BlockSpec((None, seq, LANES), lambda b, i: (b, 0, 16)),
                  pl.BlockSpec((None, seq, LANES), lambda b, i: (b, 0, 0)),
                  pl.BlockSpec((None, tq, LANES), lambda b, i: (b, i, 1)),
                  pl.BlockSpec((1, LANES), full2),
                  pl.BlockSpec((LANES, LANES), full2),
                  pl.BlockSpec((LANES, LANES), full2),
                  pl.BlockSpec((seq, LANES), full2),
                  pl.BlockSpec((seq, LANES), full2),
                  pl.BlockSpec((seq, LANES), full2)],
        out_specs=pl.BlockSpec((None, tq, 512), lambda b, i: (b, i, 0)),
        out_shape=jax.ShapeDtypeStruct((bsz, seq, 512), BF16),
        scratch_shapes=[pltpu.VMEM((seq, LANES), BF16), pltpu.VMEM((LANES, seq), BF16)],
        compiler_params=_cparams("arbitrary", "arbitrary"),
        name="dsa",
    )(a3, a3, a3, m3, m3, gkv, wuk, wuv, cos, s1, s2)


def _merge_kernel(on_ref, od_ref, x_ref, n1_ref, sc1_ref, sh1_ref, wg_ref, g1_ref, sc_ref, sh_ref, n2_ref,
                  wn_ref, wd_ref, wo_ref, x1_ref, h2_ref):
    h1 = _modulated_norm(x_ref[...], n1_ref[...], sc1_ref[...], sh1_ref[...])
    gm = _dot(h1.astype(BF16), wg_ref[...])
    merged = (jax.nn.sigmoid(gm[:, :D_MODEL]) * _dot(on_ref[...], wn_ref[...])
              + jax.nn.sigmoid(gm[:, D_MODEL:]) * _dot(od_ref[...], wd_ref[...]))
    x1 = x_ref[...] + g1_ref[...] * _dot(merged.astype(BF16), wo_ref[...])
    x1_ref[...] = x1
    y = x1 * lax.rsqrt(jnp.mean(x1 * x1, axis=-1, keepdims=True) + NORM_EPS) * n2_ref[...]
    h2_ref[...] = (y * (1.0 + sc_ref[...]) + sh_ref[...]).astype(BF16)


def _merge(o_nsa, o_dsa, x2, n1, sc1, sh1, wg, g1, sc2, sh2, n2, wn, wd, wo, seq):
    t, d = x2.shape
    tm = MERGE_TM
    nb = seq // tm
    row = lambda i: (i, 0)
    bat = lambda i: (i // nb, 0, 0)
    full = lambda i: (0, 0)
    return pl.pallas_call(
        _merge_kernel,
        grid=(t // tm,),
        in_specs=[pl.BlockSpec((tm, 512), row), pl.BlockSpec((tm, 512), row),
                  pl.BlockSpec((tm, d), row), pl.BlockSpec((1, d), full),
                  pl.BlockSpec((None, 1, d), bat), pl.BlockSpec((None, 1, d), bat), pl.BlockSpec(wg.shape, full),
                  pl.BlockSpec((None, 1, d), bat), pl.BlockSpec((None, 1, d), bat),
                  pl.BlockSpec((None, 1, d), bat), pl.BlockSpec((1, d), full),
                  pl.BlockSpec(wn.shape, full), pl.BlockSpec(wd.shape, full), pl.BlockSpec(wo.shape, full)],
        out_specs=[pl.BlockSpec((tm, d), row), pl.BlockSpec((tm, d), row)],
        out_shape=[jax.ShapeDtypeStruct((t, d), F32), jax.ShapeDtypeStruct((t, d), BF16)],
        compiler_params=_cparams("arbitrary"),
        name="merge",
    )(o_nsa, o_dsa, x2, n1, sc1, sh1, wg, g1, sc2, sh2, n2, wn, wd, wo)


def _topk_rows(vals, k):
    n, t = vals.shape
    ridx = lax.broadcasted_iota(I32, (n, t), 0).astype(F32)
    slot = lax.broadcasted_iota(I32, (k, t), 0)
    top_v = jnp.zeros((k, t), F32)
    top_i = jnp.zeros((k, t), F32)
    for j in range(k):
        m = jnp.max(vals, axis=0, keepdims=True)
        idx = jnp.min(jnp.where(vals == m, ridx, float(n)), axis=0, keepdims=True)
        top_v = jnp.where(slot == j, m, top_v)
        top_i = jnp.where(slot == j, idx, top_i)
        vals = jnp.where(ridx == idx, -jnp.inf, vals)
    return top_v, top_i


N_CAND = -(-sum(PEER_TOPK // (a + 1) for a in range(PEER_TOPK)) // 8) * 8


def _route_head(s, cand_ref):
    t = s.shape[1]
    s0, i0 = _topk_rows(s[:PEER_KEYS], PEER_TOPK)
    s1, i1 = _topk_rows(s[PEER_KEYS:], PEER_TOPK)
    r0 = 0
    for a in range(PEER_TOPK):
        nb = PEER_TOPK // (a + 1)
        cand_ref[r0:r0 + nb, :] = s0[a:a + 1, :] + s1[0:nb, :]
        r0 += nb
    cand_ref[r0:, :] = jnp.full((N_CAND - r0, t), -jnp.inf, F32)
    top_s, top_r = _topk_rows(cand_ref[...], PEER_TOPK)
    pa = jnp.zeros((PEER_TOPK, t), F32)
    pb = jnp.zeros((PEER_TOPK, t), F32)
    r = 0
    for a in range(PEER_TOPK):
        for b in range(PEER_TOPK // (a + 1)):
            hit = top_r == r
            pa = jnp.where(hit, float(a), pa)
            pb = jnp.where(hit, float(b), pb)
            r += 1
    ii = jnp.zeros((PEER_TOPK, t), F32)
    jj = jnp.zeros((PEER_TOPK, t), F32)
    for a in range(PEER_TOPK):
        ii = jnp.where(pa == a, i0[a:a + 1, :], ii)
        jj = jnp.where(pb == a, i1[a:a + 1, :], jj)
    e = jnp.exp(top_s - jnp.max(top_s, axis=0, keepdims=True))
    return ii, jj, e / jnp.sum(e, axis=0, keepdims=True)


def _gelu_exact(a):
    return 0.5 * a * (1.0 + lax.erf(a * (2.0 ** -0.5)))


def _peer_kernel(h_ref, hn_ref, wq_ref, k2_ref, u_ref, v_ref, x1_ref, g2_ref, fg_ref, o_ref,
                 w3_ref, y_ref, acc_ref, q3_ref, rt_ref, cur_ref, cand_ref, *, tm, nsub, final_norm):
    i = pl.program_id(0)
    c = pl.program_id(1)
    half = PEER_KEYS // 2
    per_half = half // nsub
    pitch = tm + W3_PAD
    th = tm // 2
    groups = tm // GATE_GROUP

    def project_queries(x_ref):
        q = _dot(x_ref[...], wq_ref[...]).astype(BF16)
        for h in range(PEER_HEADS):
            q3_ref[h] = q[:, h * LANES:(h + 1) * LANES]

    def route_unit(unit):
        h = unit // 2
        part = unit % 2
        qh = q3_ref[h, pl.ds(pl.multiple_of(part * th, th), th), :]
        ii, jj, gate = _route_head(_dot_nt(k2_ref[h], qh), cand_ref)
        rows = pl.ds(pl.multiple_of(h * PEER_TOPK, PEER_TOPK), PEER_TOPK)
        rt_ref[0, part, rows, :] = ii
        rt_ref[1, part, rows, :] = jj
        rt_ref[2, part, rows, :] = gate

    @pl.when(c == 0)
    def _():
        acc_ref[...] = jnp.zeros_like(acc_ref)

        @pl.when(i == 0)
        def _():
            project_queries(h_ref)
            lax.fori_loop(0, 2 * PEER_HEADS, lambda unit, carry: (route_unit(unit), carry)[1], 0)

        for arr in range(3):
            for part in range(2):
                cur_ref[arr, part * th:(part + 1) * th, :] = rt_ref[arr, part].T
        project_queries(hn_ref)

    @pl.when(c % per_half == 0)
    def _():
        i0 = (c // per_half) * half
        sub_i = (lax.broadcasted_iota(I32, (half, LANES), 0) + i0).astype(F32)
        sub_j = lax.broadcasted_iota(I32, (PEER_KEYS, LANES), 0).astype(F32)

        def group(tg, carry):
            route_unit((c // per_half) * groups + tg)
            r0 = pl.multiple_of(tg * GATE_GROUP, GATE_GROUP)
            ii = cur_ref[0, pl.ds(r0, GATE_GROUP), :]
            jj = cur_ref[1, pl.ds(r0, GATE_GROUP), :]
            gg = cur_ref[2, pl.ds(r0, GATE_GROUP), :]
            for r in range(GATE_GROUP):
                a = (sub_i == ii[r:r + 1, :]).astype(BF16)
                rg = jnp.where(sub_j == jj[r:r + 1, :], gg[r:r + 1, :], 0.0).astype(BF16)
                w3_ref[pl.ds(r0 + r, half, stride=pitch), :] = _dot_nt(a, rg)
            return carry
        lax.fori_loop(0, tm // GATE_GROUP, group, 0)

    act = _dot_nt(h_ref[...], u_ref[...])
    for s in range(nsub):
        row_i = (c % per_half) * nsub + s
        w = w3_ref[pl.ds(pl.multiple_of(row_i * pitch, 8), tm), :]
        sl = slice(s * LANES, (s + 1) * LANES)
        y_ref[:, sl] = (w * _gelu_exact(act[:, sl])).astype(BF16)
    acc_ref[...] += _dot(y_ref[...], v_ref[...])

    @pl.when(c == pl.num_programs(1) - 1)
    def _():
        x2 = x1_ref[...] + g2_ref[...] * acc_ref[...]
        if final_norm:
            x2 = x2 * lax.rsqrt(jnp.mean(x2 * x2, axis=-1, keepdims=True) + NORM_EPS) * fg_ref[...]
        o_ref[...] = x2


def _peer(h2, wq, k2, u, v, x1, g2, fg, seq, final_norm):
    t, d = h2.shape
    tm = PEER_TM
    nsub = PEER_NSUB
    ne = nsub * PEER_KEYS
    nb = seq // tm
    ntile = t // tm
    builds = PEER_KEYS // (PEER_KEYS // 2)
    assert builds * (tm // GATE_GROUP) == 2 * PEER_HEADS, "one routing unit per gate-build iteration"
    row = lambda i, c: (i, 0)
    nxt = lambda i, c: (jnp.minimum(i + 1, ntile - 1), 0)
    chunk = lambda i, c: (c, 0)
    nk = PEER_HEADS * PEER_TOPK
    return pl.pallas_call(
        functools.partial(_peer_kernel, tm=tm, nsub=nsub, final_norm=final_norm),
        grid=(ntile, PEER_KEYS // nsub),
        in_specs=[pl.BlockSpec((tm, d), row), pl.BlockSpec((tm, d), nxt),
                  pl.BlockSpec(wq.shape, lambda i, c: (0, 0)),
                  pl.BlockSpec(k2.shape, lambda i, c: (0, 0, 0)),
                  pl.BlockSpec((ne, d), chunk), pl.BlockSpec((ne, d), chunk),
                  pl.BlockSpec((tm, d), row),
                  pl.BlockSpec((None, 1, d), lambda i, c: (i // nb, 0, 0)),
                  pl.BlockSpec((1, d), lambda i, c: (0, 0))],
        out_specs=pl.BlockSpec((tm, d), row),
        out_shape=jax.ShapeDtypeStruct((t, d), F32),
        scratch_shapes=[pltpu.VMEM((PEER_KEYS // 2 * (tm + W3_PAD), LANES), F32),
                        pltpu.VMEM((tm, ne), BF16),
                        pltpu.VMEM((tm, d), F32),
                        pltpu.VMEM((PEER_HEADS, tm, LANES), BF16),
                        pltpu.VMEM((3, 2, nk, tm // 2), F32),
                        pltpu.VMEM((3, tm, nk), F32),
                        pltpu.VMEM((N_CAND, tm // 2), F32)],
        compiler_params=_cparams("arbitrary", "arbitrary"),
        name="peer",
    )(h2, h2, wq, k2, u, v, x1, g2, fg)


def _dup(w):
    return jnp.concatenate([w, w], axis=1)


def _win_layout(w_in):
    nq, nkv = NSA_HEADS * HEAD_DIM, NSA_KV_HEADS * HEAD_DIM
    sizes = (nq, nkv, nkv, nkv, nkv, nkv, nkv, 3 * NSA_HEADS, DSA_HEADS * HEAD_DIM, DSA_KV_RANK,
             IDX_HEADS * IDX_DIM, IDX_DIM, IDX_HEADS, 2 * D_MODEL)
    offs = [0]
    for s in sizes:
        offs.append(offs[-1] + s)
    q_n, kc, vc, ks, vs, kw, vw, g_n, q_d, ckv, qi, ki, wi, g_m = [w_in[:, offs[k]:offs[k + 1]] for k in range(14)]
    dup2 = lambda w: jnp.concatenate([_dup(w[:, :HEAD_DIM]), _dup(w[:, HEAD_DIM:])], axis=1)
    misc = jnp.concatenate([g_n, wi, jnp.zeros((w_in.shape[0], LANES - 32), w_in.dtype)], axis=1)
    cols = [q_n, dup2(ks), dup2(kw), q_d, qi, _dup(ki), kc, dup2(vs), dup2(vw), vc, ckv, misc, g_m]
    return jnp.concatenate(cols, axis=1).astype(BF16)


def _rope_tables(seq):
    half = HEAD_DIM // 2
    pos = jnp.arange(seq, dtype=F32)
    inv = ROPE_THETA ** (-jnp.arange(half, dtype=F32) / half)
    ang = pos[:, None] * inv[None, :]
    cos, sin, zero = jnp.cos(ang), jnp.sin(ang), jnp.zeros((seq, half), F32)
    c = jnp.concatenate([cos, cos, cos, cos], axis=1)
    s1 = jnp.concatenate([-sin, zero, -sin, zero], axis=1)
    s2 = jnp.concatenate([zero, sin, zero, sin], axis=1)
    return c, s1, s2


def _cmp_weights(pos, w1, w2):
    g = NSA_KV_HEADS
    eye = jnp.eye(g, dtype=F32)
    big = jnp.einsum("lde,gh->lgdhe", w1, eye).reshape(CMP_BLOCK, g * HEAD_DIM, g * HEAD_DIM)
    wa = big[:CMP_STRIDE].reshape(CMP_STRIDE * g * HEAD_DIM, g * HEAD_DIM)
    wb = big[CMP_STRIDE:].reshape(CMP_STRIDE * g * HEAD_DIM, g * HEAD_DIM)
    z = jnp.zeros_like(w2)
    w2d = jnp.concatenate([jnp.concatenate([w2, w2, z, z], axis=1),
                           jnp.concatenate([z, z, w2, w2], axis=1)], axis=0)
    pos_t = jnp.broadcast_to(pos[:, None, :], (CMP_BLOCK, g, HEAD_DIM))
    pa = pos_t[:CMP_STRIDE].reshape(1, -1)
    pb = pos_t[CMP_STRIDE:].reshape(1, -1)
    return wa.astype(BF16), wb.astype(BF16), w2d.astype(BF16), pa, pb


def _selection_constants(seq):
    ncmp = seq // CMP_STRIDE
    nslc = seq // SLC_BLOCK
    per = SLC_BLOCK // CMP_STRIDE
    n = jnp.arange(ncmp)[:, None]
    j = jnp.arange(LANES)[None, :]
    mimp = ((n // per == j).astype(F32) + ((n + 1) // per == j).astype(F32)) * (j < nslc)
    mimp = mimp * (n < ncmp - 1)
    key = jnp.arange(seq)[None, :]
    eexp = (key // SLC_BLOCK == jnp.arange(LANES)[:, None]).astype(F32)
    return mimp.T.astype(BF16), eexp.astype(BF16)


def _subkey_blocks(subkeys):
    z = jnp.zeros_like(subkeys[:, 0])
    top = jnp.concatenate([subkeys[:, 0], z], axis=2)
    bot = jnp.concatenate([z, subkeys[:, 1]], axis=2)
    return jnp.concatenate([top, bot], axis=1).astype(BF16)


def kernel(x, c, ada_w, ada_b, norm1_g, w_in, cmp_pos, cmp_w1_k, cmp_w2_k, cmp_w1_v, cmp_w2_v,
           kv_norm_g, w_uk, w_uv, w_br_nsa, w_br_dsa, w_out, norm2_g,
           peer_wq, peer_subkeys, peer_u, peer_v, final_g):
    bsz, seq, d = x.shape
    depth = ada_w.shape[0]
    t = bsz * seq
    cos, s1, s2 = _rope_tables(seq)
    mimp, eexp = _selection_constants(seq)
    xt = x.reshape(t, d)
    out = xt
    for l in range(depth):
        mod = _ada(c, ada_w[l], ada_b[l]).reshape(bsz, 1, 6 * d)
        sh1, sc1, g1, sh2, sc2, g2 = [mod[:, :, k * d:(k + 1) * d] for k in range(6)]
        w_all = _win_layout(w_in[l])
        n1 = norm1_g[l].reshape(1, d)
        a, kc, vv, vc, mm = _proj(xt, sc1, sh1, n1, w_all[:, :OFF_G], cos, s1, s2, seq)
        a3 = a.reshape(bsz, seq, NA)
        m3 = mm.reshape(bsz, seq, NM)
        ncmp = seq // CMP_STRIDE
        wak, wbk, w2k, pa, pb = _cmp_weights(cmp_pos[l], cmp_w1_k[l], cmp_w2_k[l])
        wav, wbv, w2v, _, _ = _cmp_weights(cmp_pos[l], cmp_w1_v[l], cmp_w2_v[l])
        o_nsa = _nsa(a3, vv.reshape(bsz, seq, NV), kc.reshape(bsz, ncmp, CMP_STRIDE * 128),
                     vc.reshape(bsz, ncmp, CMP_STRIDE * 128), m3,
                     (wak, wbk, w2k, wav, wbv, w2v, pa, pb, mimp, eexp), seq)
        o_dsa = _dsa(a3, m3, kv_norm_g[l].reshape(1, DSA_KV_RANK), _dup(w_uk[l]).astype(BF16),
                     _dup(w_uv[l]).astype(BF16), cos, s1, s2, seq)
        x1, h2 = _merge(o_nsa.reshape(t, 512), o_dsa.reshape(t, 512), xt, n1, sc1, sh1, w_all[:, OFF_G:],
                        g1, sc2, sh2,
                        norm2_g[l].reshape(1, d), w_br_nsa[l].astype(BF16), w_br_dsa[l].astype(BF16),
                        w_out[l].astype(BF16), seq)
        out = _peer(h2, peer_wq[l].astype(BF16), _subkey_blocks(peer_subkeys[l]),
                    peer_u[l].astype(BF16), peer_v[l].astype(BF16), x1, g2,
                    final_g.reshape(1, d), seq, final_norm=(l == depth - 1))
        xt = out
    return out.reshape(bsz, seq, d)
```
